```python
import jax, jax.numpy as jnp
from jax import lax
import numpy as np

D_MODEL = 1024
BATCH = 4
SEQ = 4096
DEPTH = 4
DEC_BATCH = 128
DEC_SEQ = 4
PAST_LEN = 2048
PAGE_SIZE = 128

N_MIXERS = 3
EPS = 1e-6
CHUNK = 128
D_GATE = 2 * D_MODEL
SGU_GROUPS = 8
SGU_GDIM = D_GATE // SGU_GROUPS
D_CONV = D_MODEL
CONV_W = 3
HEAD_DIM = 64
HEADS_PER_GROUP = 4
WINDOWS = (128, 512, 2048)
DILATIONS = (1, 4, 16)
N_ATTN_GROUPS = 3
N_HEADS_C = HEADS_PER_GROUP * N_ATTN_GROUPS
D_ATTN = N_HEADS_C * HEAD_DIM
ROPE_DIM = HEAD_DIM // 4
ROPE_THETA = 500000.0
ATTN_BLOCK = 128
D_FF = 2816
N_EXPERTS = 8
TOP_K = 2
D_FF_EXPERT = 3584
N_A_LAYERS = (DEPTH + N_MIXERS - 1) // N_MIXERS
N_B_LAYERS = (DEPTH + N_MIXERS - 2) // N_MIXERS
N_C_LAYERS = DEPTH // N_MIXERS
N_DENSE_LAYERS = (DEPTH + 1) // 2
N_MOE_LAYERS = DEPTH // 2

kernel_name = 'hybrid_sgu_conv_dilated_attn_decoder_step'


def _rmsnorm(x, g):
    x32 = x.astype(jnp.float32)
    y = x32 * lax.rsqrt(jnp.mean(x32 * x32, axis=-1, keepdims=True) + EPS)
    return (y * g.astype(jnp.float32)).astype(x.dtype)


def _layernorm(x, g, b):
    x32 = x.astype(jnp.float32)
    mu = jnp.mean(x32, axis=-1, keepdims=True)
    xc = x32 - mu
    var = jnp.mean(xc * xc, axis=-1, keepdims=True)
    return (xc * lax.rsqrt(var + EPS) * g.astype(jnp.float32) + b.astype(jnp.float32)).astype(x.dtype)


def _rope(x, pos):
    inv_freq = jnp.power(ROPE_THETA, -jnp.arange(0, ROPE_DIM, 2, dtype=jnp.float32) / ROPE_DIM)
    ang = pos.astype(jnp.float32)[:, None] * inv_freq[None, :]
    cos = jnp.cos(ang)[None, :, None, :]
    sin = jnp.sin(ang)[None, :, None, :]
    xr = x[..., :ROPE_DIM].astype(jnp.float32)
    x1, x2 = xr[..., :ROPE_DIM // 2], xr[..., ROPE_DIM // 2:]
    rot = jnp.concatenate([x1 * cos - x2 * sin, x2 * cos + x1 * sin], axis=-1).astype(x.dtype)
    return jnp.concatenate([rot, x[..., ROPE_DIM:]], axis=-1)


def _sgu_mixer(h, w_in, ln_g, ln_b, w_s, b_s, w_out):
    B, T, _ = h.shape
    u, v = jnp.split(jax.nn.gelu(h @ w_in), 2, axis=-1)
    v = _layernorm(v, ln_g, ln_b)
    n = min(T, CHUNK)
    vc = v.reshape(B, T // n, n, SGU_GROUPS, SGU_GDIM)
    causal = jnp.tril(jnp.ones((n, n), dtype=w_s.dtype))
    ws = w_s[:, :n, :n] * causal[None]
    s = jnp.einsum('gij,bcjgd->bcigd', ws, vc) + b_s[:, :n].T[None, None, :, :, None]
    y = (u * s.reshape(B, T, D_GATE)) @ w_out
    return y, v


def _conv_mixer(h, prev, w_in, conv_w, w_out):
    T = h.shape[1]
    gate_b, gate_c, xi = jnp.split(h @ w_in, 3, axis=-1)
    z = gate_c * xi
    z_ext = jnp.concatenate([prev.astype(z.dtype), z], axis=1)
    conv = conv_w[CONV_W - 1] * z_ext[:, CONV_W - 1:CONV_W - 1 + T]
    for tap in range(CONV_W - 1):
        conv = conv + conv_w[tap] * z_ext[:, tap:tap + T]
    y = (gate_b * conv) @ w_out
    return y, z_ext[:, -(CONV_W - 1):]


def _qkv(h, pos, w_qkv, q_norm_g, k_norm_g):
    B, T, _ = h.shape
    q, k, v = jnp.split(h @ w_qkv, 3, axis=-1)
    shp = (B, T, N_HEADS_C, HEAD_DIM)
    q = _rope(_rmsnorm(q.reshape(shp), q_norm_g), pos)
    k = _rope(_rmsnorm(k.reshape(shp), k_norm_g), pos)
    return q, k, v.reshape(shp)


def _softmax_stats(s, mask):
    s = jnp.where(mask, s, -jnp.inf)
    m = jnp.max(s, axis=-1, keepdims=True)
    p = jnp.exp(s - m)
    l = jnp.sum(p, axis=-1, keepdims=True)
    return p / l, (m + jnp.log(l))[..., 0]


def _dilated_prompt(q, k, v, window, dilation):
    B, S, H, Dh = q.shape
    span = window // dilation
    L = S // dilation
    QB = ATTN_BLOCK
    nb = -(-L // QB)
    Lp = nb * QB

    def to_sub(x):
        x = x.reshape(B, L, dilation, H, Dh).transpose(0, 2, 1, 3, 4).reshape(B * dilation, L, H, Dh)
        return jnp.pad(x, ((0, 0), (0, Lp - L), (0, 0), (0, 0)))

    def band(x):
        cur = x.reshape(B * dilation, nb, QB, H, Dh)
        prev = jnp.pad(x, ((0, 0), (QB, 0), (0, 0), (0, 0)))[:, :Lp].reshape(B * dilation, nb, QB, H, Dh)
        return jnp.concatenate([prev, cur], axis=2)

    qb = to_sub(q).reshape(B * dilation, nb, QB, H, Dh)
    kb = band(to_sub(k))
    vb = band(to_sub(v))
    s = jnp.einsum('bnqhd,bnkhd->bnhqk', qb, kb).astype(jnp.float32) * (HEAD_DIM ** -0.5)
    qi = jnp.arange(QB)[:, None] + QB
    kj = jnp.arange(2 * QB)[None, :]
    dist = qi - kj
    blk = jnp.arange(nb)[:, None, None]
    mask = (dist >= 0) & (dist <= span) & (blk * QB + kj - QB >= 0)
    p, lse = _softmax_stats(s, mask[None, :, None])
    o = jnp.einsum('bnhqk,bnkhd->bnqhd', p.astype(v.dtype), vb)
    o = o.reshape(B * dilation, Lp, H, Dh)[:, :L].reshape(B, dilation, L, H, Dh)
    o = o.transpose(0, 2, 1, 3, 4).reshape(B, S, H, Dh)
    lse = lse.transpose(0, 1, 3, 2).reshape(B * dilation, Lp, H)[:, :L].reshape(B, dilation, L, H)
    lse = lse.transpose(0, 2, 1, 3).reshape(B, S, H)
    return o, lse


def _dilated_sample(q, k, v, cache_kv, window, dilation):
    T = q.shape[1]
    buf = cache_kv.shape[1]
    span = window // dilation
    k_all = jnp.concatenate([cache_kv[:, :, 0].astype(k.dtype), k], axis=1)
    v_all = jnp.concatenate([cache_kv[:, :, 1].astype(v.dtype), v], axis=1)
    idx = buf + jnp.arange(T)[:, None] - dilation * jnp.arange(span + 1)[None, :]
    valid = idx >= 0
    idx = jnp.maximum(idx, 0)
    kg = k_all[:, idx]
    vg = v_all[:, idx]
    s = jnp.einsum('bthd,btkhd->bhtk', q, kg).astype(jnp.float32) * (HEAD_DIM ** -0.5)
    p, lse = _softmax_stats(s, valid[None, None])
    o = jnp.einsum('bhtk,btkhd->bthd', p.astype(v.dtype), vg)
    return o, lse.transpose(0, 2, 1)


def _merge_groups(outs, lses, w_out):
    w = jax.nn.softmax(jnp.stack(lses, axis=0), axis=0)
    o = jnp.concatenate([outs[g] * w[g][..., None].astype(outs[g].dtype) for g in range(N_ATTN_GROUPS)], axis=2)
    B, T = o.shape[:2]
    return o.reshape(B, T, D_ATTN) @ w_out


def _swiglu(h, w_gu, w_down):
    g, u = jnp.split(h @ w_gu, 2, axis=-1)
    return (jax.nn.silu(g) * u) @ w_down


def _moe(h, router, w_gu, w_down):
    logits = (h @ router).astype(jnp.float32)
    vals, idx = lax.top_k(logits, TOP_K)
    gates = jax.nn.softmax(vals, axis=-1)
    weights = jnp.sum(jax.nn.one_hot(idx, N_EXPERTS, dtype=jnp.float32) * gates[..., None], axis=-2)
    y = jnp.zeros_like(h)
    for e in range(N_EXPERTS):
        y = y + weights[..., e:e + 1].astype(h.dtype) * _swiglu(h, w_gu[e], w_down[e])
    return y


def setup_inputs(seed: int = 0) -> dict:
    key = jax.random.key(seed)
    keys = jax.random.split(key, 32)

    def nrm(i, shape, scale):
        return jax.random.normal(keys[i], shape, jnp.float32) * scale

    def buf(w):
        return min(w, PAST_LEN)

    kv_shape = lambda w: (N_C_LAYERS, DEC_BATCH, buf(w), 2, HEADS_PER_GROUP, HEAD_DIM)
    return {
        'x_prompt': nrm(0, (BATCH, SEQ, D_MODEL), 1.0),
        'x_sample': nrm(1, (DEC_BATCH, DEC_SEQ, D_MODEL), 1.0),
        'state_conv': nrm(2, (N_B_LAYERS, DEC_BATCH, CONV_W - 1, D_CONV), 1.0),
        'cache_kv_w128': nrm(3, kv_shape(WINDOWS[0]), 1.0),
        'cache_kv_w512': nrm(4, kv_shape(WINDOWS[1]), 1.0),
        'cache_kv_w2048': nrm(5, kv_shape(WINDOWS[2]), 1.0),
        'norm_mix_g': 1.0 + nrm(6, (DEPTH, D_MODEL), 0.02),
        'norm_ffn_g': 1.0 + nrm(7, (DEPTH, D_MODEL), 0.02),
        'a_w_in': nrm(8, (N_A_LAYERS, D_MODEL, 2 * D_GATE), D_MODEL ** -0.5),
        'a_ln_g': 1.0 + nrm(9, (N_A_LAYERS, D_GATE), 0.02),
        'a_ln_b': nrm(10, (N_A_LAYERS, D_GATE), 0.02),
        'a_w_s': nrm(11, (N_A_LAYERS, SGU_GROUPS, CHUNK, CHUNK), CHUNK ** -0.5),
        'a_b_s': 1.0 + nrm(12, (N_A_LAYERS, SGU_GROUPS, CHUNK), 0.1),
        'a_w_out': nrm(13, (N_A_LAYERS, D_GATE, D_MODEL), D_GATE ** -0.5),
        'b_w_in': nrm(14, (N_B_LAYERS, D_MODEL, 3 * D_CONV), D_MODEL ** -0.5),
        'b_conv_w': nrm(15, (N_B_LAYERS, CONV_W, D_CONV), CONV_W ** -0.5),
        'b_w_out': nrm(16, (N_B_LAYERS, D_CONV, D_MODEL), D_CONV ** -0.5),
        'c_w_qkv': nrm(17, (N_C_LAYERS, D_MODEL, 3 * D_ATTN), D_MODEL ** -0.5),
        'c_q_norm_g': 1.0 + nrm(18, (N_C_LAYERS, HEAD_DIM), 0.02),
        'c_k_norm_g': 1.0 + nrm(19, (N_C_LAYERS, HEAD_DIM), 0.02),
        'c_w_out': nrm(20, (N_C_LAYERS, D_ATTN, D_MODEL), D_ATTN ** -0.5),
        'f_w_gu': nrm(21, (N_DENSE_LAYERS, D_MODEL, 2 * D_FF), D_MODEL ** -0.5),
        'f_w_down': nrm(22, (N_DENSE_LAYERS, D_FF, D_MODEL), D_FF ** -0.5),
        'm_router': nrm(23, (N_MOE_LAYERS, D_MODEL, N_EXPERTS), D_MODEL ** -0.5),
        'm_w_gu': nrm(24, (N_MOE_LAYERS, N_EXPERTS, D_MODEL, 2 * D_FF_EXPERT), D_MODEL ** -0.5),
        'm_w_down': nrm(25, (N_MOE_LAYERS, N_EXPERTS, D_FF_EXPERT, D_MODEL), D_FF_EXPERT ** -0.5),
    }


def reference(x_prompt, x_sample, state_conv, cache_kv_w128, cache_kv_w512, cache_kv_w2048,
              norm_mix_g, norm_ffn_g, a_w_in, a_ln_g, a_ln_b, a_w_s, a_b_s, a_w_out,
              b_w_in, b_conv_w, b_w_out, c_w_qkv, c_q_norm_g, c_k_norm_g, c_w_out,
              f_w_gu, f_w_down, m_router, m_w_gu, m_w_down):
    caches = (cache_kv_w128, cache_kv_w512, cache_kv_w2048)
    xp, xs = x_prompt, x_sample
    pos_p = jnp.arange(SEQ, dtype=jnp.int32)
    pos_s = PAST_LEN + jnp.arange(DEC_SEQ, dtype=jnp.int32)
    conv_p, conv_s, sgu_v_s = [], [], []
    kv_p = [[] for _ in range(N_ATTN_GROUPS)]
    kv_s = [[] for _ in range(N_ATTN_GROUPS)]
    for i in range(DEPTH):
        kind, j = i % N_MIXERS, i // N_MIXERS
        hp = _rmsnorm(xp, norm_mix_g[i])
        hs = _rmsnorm(xs, norm_mix_g[i])
        if kind == 0:
            yp, _ = _sgu_mixer(hp, a_w_in[j], a_ln_g[j], a_ln_b[j], a_w_s[j], a_b_s[j], a_w_out[j])
            ys, v_rows = _sgu_mixer(hs, a_w_in[j], a_ln_g[j], a_ln_b[j], a_w_s[j], a_b_s[j], a_w_out[j])
            sgu_v_s.append(v_rows)
        elif kind == 1:
            zeros_prev = jnp.zeros((BATCH, CONV_W - 1, D_CONV), hp.dtype)
            yp, cp = _conv_mixer(hp, zeros_prev, b_w_in[j], b_conv_w[j], b_w_out[j])
            ys, cs = _conv_mixer(hs, state_conv[j], b_w_in[j], b_conv_w[j], b_w_out[j])
            conv_p.append(cp)
            conv_s.append(cs)
        else:
            qp, kp, vp = _qkv(hp, pos_p, c_w_qkv[j], c_q_norm_g[j], c_k_norm_g[j])
            qs, ks, vs = _qkv(hs, pos_s, c_w_qkv[j], c_q_norm_g[j], c_k_norm_g[j])
            outs_p, lses_p, outs_s, lses_s = [], [], [], []
            for g in range(N_ATTN_GROUPS):
                w, d = WINDOWS[g], DILATIONS[g]
                sl = slice(g * HEADS_PER_GROUP, (g + 1) * HEADS_PER_GROUP)
                o, l = _dilated_prompt(qp[:, :, sl], kp[:, :, sl], vp[:, :, sl], w, d)
                outs_p.append(o)
                lses_p.append(l)
                o, l = _dilated_sample(qs[:, :, sl], ks[:, :, sl], vs[:, :, sl], caches[g][j], w, d)
                outs_s.append(o)
                lses_s.append(l)
                keep = min(w, SEQ)
                kv_p[g].append(jnp.stack([kp[:, -keep:, sl], vp[:, -keep:, sl]], axis=2))
                kv_s[g].append(jnp.stack([ks[:, :, sl], vs[:, :, sl]], axis=2))
            yp = _merge_groups(outs_p, lses_p, c_w_out[j])
            ys = _merge_groups(outs_s, lses_s, c_w_out[j])
        xp = xp + yp
        xs = xs + ys
        hp = _rmsnorm(xp, norm_ffn_g[i])
        hs = _rmsnorm(xs, norm_ffn_g[i])
        if i % 2 == 0:
            e = i // 2
            xp = xp + _swiglu(hp, f_w_gu[e], f_w_down[e])
            xs = xs + _swiglu(hs, f_w_gu[e], f_w_down[e])
        else:
            e = i // 2
            xp = xp + _moe(hp, m_router[e], m_w_gu[e], m_w_down[e])
            xs = xs + _moe(hs, m_router[e], m_w_gu[e], m_w_down[e])
    new_conv_prompt = jnp.stack(conv_p, axis=0)
    new_conv_sample = jnp.stack(conv_s, axis=0)
    new_kv_w128_prompt = jnp.stack(kv_p[0], axis=0)
    new_kv_w128_sample = jnp.stack(kv_s[0], axis=0)
    new_kv_w512_prompt = jnp.stack(kv_p[1], axis=0)
    new_kv_w512_sample = jnp.stack(kv_s[1], axis=0)
    new_kv_w2048_prompt = jnp.stack(kv_p[2], axis=0)
    new_kv_w2048_sample = jnp.stack(kv_s[2], axis=0)
    new_sgu_v_sample = jnp.stack(sgu_v_s, axis=0)
    return (xp, xs, new_conv_prompt, new_conv_sample, new_kv_w128_prompt, new_kv_w128_sample,
            new_kv_w512_prompt, new_kv_w512_sample, new_kv_w2048_prompt, new_kv_w2048_sample,
            new_sgu_v_sample)
```

```python
import functools

import numpy as np
import jax
import jax.numpy as jnp
from jax import lax
from jax.experimental import pallas as pl
from jax.experimental.pallas import tpu as pltpu

F32 = jnp.float32
BF16 = jnp.bfloat16

D_MODEL = 1024
EPS = 1e-6
CHUNK = 128
D_GATE = 2 * D_MODEL
SGU_GROUPS = 8
SGU_GDIM = D_GATE // SGU_GROUPS
D_CONV = D_MODEL
HEAD_DIM = 64
HEADS_PER_GROUP = 4
GROUP_DIM = HEADS_PER_GROUP * HEAD_DIM
WINDOWS = (128, 512, 2048)
DILATIONS = (1, 4, 16)
N_ATTN_GROUPS = 3
D_ATTN = N_ATTN_GROUPS * GROUP_DIM
ROPE_DIM = HEAD_DIM // 4
ROPE_THETA = 500000.0
ATTN_BLOCK = 128
D_FF = 2816
N_EXPERTS = 8
D_FF_EXPERT = 3584
DEC_SEQ = 4
PAST_LEN = 2048

SUBLANES = 8
LANES = 128
VMEM_LIMIT = 56 * 1024 * 1024

ROW_TILE = 512
FF_CHUNK = D_FF // 2
MOE_TILE = 1024
MOE_FF_CHUNK = 512
MOE_FF_STEPS = D_FF_EXPERT // MOE_FF_CHUNK


def _dot(a, b):
    return jnp.dot(a, b, preferred_element_type=F32)


def _dot_nt(a, b):
    return lax.dot_general(a, b, (((1,), (1,)), ((), ())), preferred_element_type=F32)


def _rms(x, g):
    return x * lax.rsqrt(jnp.mean(x * x, axis=-1, keepdims=True) + EPS) * g


def _gelu_tanh(x):
    c = np.float32(np.sqrt(2 / np.pi))
    return x * (0.5 * (1.0 + jnp.tanh(c * (x + np.float32(0.044715) * (x * x * x)))))


def _silu(x):
    return x * (1.0 / (1.0 + jnp.exp(-x)))


def _params(*sem):
    return pltpu.CompilerParams(dimension_semantics=sem, vmem_limit_bytes=VMEM_LIMIT)


def _resident(shape):
    zeros = (0,) * len(shape)
    return pl.BlockSpec(shape, lambda *_: zeros, pipeline_mode=pl.Buffered(1))


def _row_spec(width, first_tile=0):
    return pl.BlockSpec((ROW_TILE, width), lambda i: (i + first_tile, 0))


def _sgu_front(x, gmix, winv_ref, lng, lnb):
    h = _rms(x, gmix).astype(BF16)
    v = _gelu_tanh(_dot(h, winv_ref[...]))
    mu = jnp.mean(v, axis=-1, keepdims=True)
    vc = v - mu
    var = jnp.mean(vc * vc, axis=-1, keepdims=True)
    return h, vc * lax.rsqrt(var + EPS) * lng + lnb


def _sgu_back(x, h, winu_ref, s_ref, wout_ref, o_ref):
    u = _gelu_tanh(_dot(h, winu_ref[...]))
    o_ref[...] = x + _dot((u * s_ref[...]).astype(BF16), wout_ref[...])


def _sgu_prompt_body(x_ref, gmix_ref, winu_ref, winv_ref, lng_ref, lnb_ref, ws_ref, bs_ref,
                     wout_ref, o_ref, s_ref):
    x = x_ref[...]
    h, vn = _sgu_front(x, gmix_ref[...], winv_ref, lng_ref[...], lnb_ref[...])
    vb = vn.astype(BF16)
    row = lax.broadcasted_iota(jnp.int32, (CHUNK, CHUNK), 0)
    col = lax.broadcasted_iota(jnp.int32, (CHUNK, CHUNK), 1)
    for g in range(SGU_GROUPS):
        wm = jnp.where(row >= col, ws_ref[g], 0.0).astype(BF16)
        bias = jnp.concatenate([bs_ref[g]] * (SGU_GDIM // LANES), axis=1)
        cols = slice(g * SGU_GDIM, (g + 1) * SGU_GDIM)
        for c in range(ROW_TILE // CHUNK):
            rows = slice(c * CHUNK, (c + 1) * CHUNK)
            s_ref[rows, cols] = _dot(wm, vb[rows, cols]) + bias
    _sgu_back(x, h, winu_ref, s_ref, wout_ref, o_ref)


def _sgu_sample_body(w4_ref, b4_ref, x_ref, gmix_ref, winu_ref, winv_ref, lng_ref, lnb_ref,
                     wout_ref, o_ref, v_ref, s_ref):
    x = x_ref[...]
    nb = x.shape[0] // DEC_SEQ
    h, vn = _sgu_front(x, gmix_ref[...], winv_ref, lng_ref[...], lnb_ref[...])
    v_ref[...] = vn
    for g in range(SGU_GROUPS):
        cols = slice(g * SGU_GDIM, (g + 1) * SGU_GDIM)
        for t in range(DEC_SEQ):
            acc = w4_ref[g, t * DEC_SEQ] * vn[0:nb, cols]
            for j in range(1, t + 1):
                acc = acc + w4_ref[g, t * DEC_SEQ + j] * vn[j * nb:(j + 1) * nb, cols]
            s_ref[t * nb:(t + 1) * nb, cols] = acc + b4_ref[g, t]
    _sgu_back(x, h, winu_ref, s_ref, wout_ref, o_ref)


def _sgu_layer(x, n_prompt, gmix, w_in, ln_g, ln_b, w_s, b_s, w_out):
    n = x.shape[0]
    n_sample = n - n_prompt
    assert n_sample == ROW_TILE and n_prompt % ROW_TILE == 0
    winu = w_in[:, :D_GATE].astype(BF16)
    winv = w_in[:, D_GATE:].astype(BF16)
    wout = w_out.astype(BF16)
    gmix = gmix.reshape(1, D_MODEL)
    lng = ln_g.reshape(1, D_GATE)
    lnb = ln_b.reshape(1, D_GATE)
    bs = jnp.broadcast_to(b_s[:, :, None], (SGU_GROUPS, CHUNK, LANES))
    weights = [_resident((1, D_MODEL)), _resident((D_MODEL, D_GATE)), _resident((D_MODEL, D_GATE)),
               _resident((1, D_GATE)), _resident((1, D_GATE))]
    x = pl.pallas_call(
        _sgu_prompt_body,
        grid=(n_prompt // ROW_TILE,),
        in_specs=[_row_spec(D_MODEL)] + weights + [
            _resident((SGU_GROUPS, CHUNK, CHUNK)), _resident((SGU_GROUPS, CHUNK, LANES)),
            _resident((D_GATE, D_MODEL))],
        out_specs=_row_spec(D_MODEL),
        out_shape=jax.ShapeDtypeStruct((n, D_MODEL), F32),
        scratch_shapes=[pltpu.VMEM((ROW_TILE, D_GATE), F32)],
        input_output_aliases={0: 0},
        compiler_params=_params("parallel"),
        name="sgu_prompt",
    )(x, gmix, winu, winv, lng, lnb, w_s, bs, wout)
    w4 = w_s[:, :DEC_SEQ, :DEC_SEQ].reshape(SGU_GROUPS, DEC_SEQ * DEC_SEQ)
    b4 = b_s[:, :DEC_SEQ]
    smem = pl.BlockSpec(memory_space=pltpu.SMEM)
    first = n_prompt // ROW_TILE
    x, v_rows = pl.pallas_call(
        _sgu_sample_body,
        grid=(1,),
        in_specs=[smem, smem, _row_spec(D_MODEL, first)] + weights + [_resident((D_GATE, D_MODEL))],
        out_specs=[_row_spec(D_MODEL, first), pl.BlockSpec((ROW_TILE, D_GATE), lambda i: (0, 0))],
        out_shape=[jax.ShapeDtypeStruct((n, D_MODEL), F32),
                   jax.ShapeDtypeStruct((n_sample, D_GATE), F32)],
        scratch_shapes=[pltpu.VMEM((ROW_TILE, D_GATE), F32)],
        input_output_aliases={2: 0},
        compiler_params=_params("arbitrary"),
        name="sgu_sample",
    )(w4, b4, x, gmix, winu, winv, lng, lnb, wout)
    return x, v_rows


def _conv_front(x, gmix, win_ref):
    h = _rms(x, gmix).astype(BF16)
    gate_b = _dot(h, win_ref[:, 0:D_CONV])
    gate_c = _dot(h, win_ref[:, D_CONV:2 * D_CONV])
    xi = _dot(h, win_ref[:, 2 * D_CONV:3 * D_CONV])
    return gate_b, gate_c * xi


def _conv_prompt_body(x_ref, gmix_ref, win_ref, cw_ref, wout_ref, o_ref, tail_ref, zbuf, *,
                      tiles_per_seq):
    @pl.when(pl.program_id(0) % tiles_per_seq == 0)
    def _():
        zbuf[0:SUBLANES, :] = jnp.zeros((SUBLANES, D_CONV), F32)

    x = x_ref[...]
    gate_b, z = _conv_front(x, gmix_ref[...], win_ref)
    zbuf[SUBLANES:SUBLANES + ROW_TILE, :] = z
    conv = (cw_ref[2:3, :] * z
            + cw_ref[1:2, :] * zbuf[SUBLANES - 1:SUBLANES - 1 + ROW_TILE, :]
            + cw_ref[0:1, :] * zbuf[SUBLANES - 2:SUBLANES - 2 + ROW_TILE, :])
    o_ref[...] = x + _dot((gate_b * conv).astype(BF16), wout_ref[...])
    tail = zbuf[ROW_TILE + SUBLANES - 2:ROW_TILE + SUBLANES, :]
    tail_ref[0] = tail
    zbuf[SUBLANES - 2:SUBLANES, :] = tail


def _conv_sample_body(x_ref, gmix_ref, win_ref, cw_ref, st_ref, wout_ref, o_ref, tail_ref):
    x = x_ref[...]
    nb = x.shape[0] // DEC_SEQ
    gate_b, z = _conv_front(x, gmix_ref[...], win_ref)
    z1 = jnp.concatenate([st_ref[1], z[:(DEC_SEQ - 1) * nb]], axis=0)
    z2 = jnp.concatenate([st_ref[0], st_ref[1], z[:(DEC_SEQ - 2) * nb]], axis=0)
    conv = cw_ref[2:3, :] * z + cw_ref[1:2, :] * z1 + cw_ref[0:1, :] * z2
    o_ref[...] = x + _dot((gate_b * conv).astype(BF16), wout_ref[...])
    tail_ref[...] = z[(DEC_SEQ - 2) * nb:]


def _conv_layer(x, n_prompt, batch, state, gmix, w_in, conv_w, w_out):
    n = x.shape[0]
    n_sample = n - n_prompt
    nb = n_sample // DEC_SEQ
    assert n_sample == ROW_TILE
    tiles_per_seq = n_prompt // batch // ROW_TILE
    win = w_in.astype(BF16)
    wout = w_out.astype(BF16)
    gmix = gmix.reshape(1, D_MODEL)
    weights = [_resident((1, D_MODEL)), _resident((D_MODEL, 3 * D_CONV)), _resident((3, D_CONV))]
    x, tail_p = pl.pallas_call(
        functools.partial(_conv_prompt_body, tiles_per_seq=tiles_per_seq),
        grid=(n_prompt // ROW_TILE,),
        in_specs=[_row_spec(D_MODEL)] + weights + [_resident((D_CONV, D_MODEL))],
        out_specs=[_row_spec(D_MODEL),
                   pl.BlockSpec((1, 2, D_CONV), lambda i: (i // tiles_per_seq, 0, 0))],
        out_shape=[jax.ShapeDtypeStruct((n, D_MODEL), F32),
                   jax.ShapeDtypeStruct((batch, 2, D_CONV), F32)],
        scratch_shapes=[pltpu.VMEM((ROW_TILE + SUBLANES, D_CONV), F32)],
        input_output_aliases={0: 0},
        compiler_params=_params("arbitrary"),
        name="conv_prompt",
    )(x, gmix, win, conv_w, wout)
    first = n_prompt // ROW_TILE
    st = state.transpose(1, 0, 2)
    x, tail_s = pl.pallas_call(
        _conv_sample_body,
        grid=(1,),
        in_specs=[_row_spec(D_MODEL, first)] + weights + [
            _resident((2, nb, D_CONV)), _resident((D_CONV, D_MODEL))],
        out_specs=[_row_spec(D_MODEL, first), pl.BlockSpec((2 * nb, D_CONV), lambda i: (0, 0))],
        out_shape=[jax.ShapeDtypeStruct((n, D_MODEL), F32),
                   jax.ShapeDtypeStruct((2 * nb, D_CONV), F32)],
        input_output_aliases={0: 0},
        compiler_params=_params("arbitrary"),
        name="conv_sample",
    )(x, gmix, win, conv_w, st, wout)
    return x, tail_p, tail_s.reshape(2, nb, D_CONV).transpose(1, 0, 2)


def _rope_tables(seq, n_sample_rows):
    nb = n_sample_rows // DEC_SEQ
    pos = jnp.concatenate([jnp.arange(seq, dtype=jnp.int32),
                           PAST_LEN + jnp.arange(n_sample_rows, dtype=jnp.int32) // nb])
    inv_freq = jnp.power(ROPE_THETA, -jnp.arange(0, ROPE_DIM, 2, dtype=F32) / ROPE_DIM)
    d = np.arange(LANES) % HEAD_DIM
    half = ROPE_DIM // 2
    ang = pos.astype(F32)[:, None] * inv_freq[d % half][None, :]
    cos, sin = jnp.cos(ang), jnp.sin(ang)
    c = jnp.where(d[None, :] < ROPE_DIM, cos, 1.0)
    s1 = jnp.where(d[None, :] < half, -sin, 0.0)
    s2 = jnp.where((d[None, :] >= half) & (d[None, :] < ROPE_DIM), sin, 0.0)
    return c, s1, s2


def _qkv_body(x_ref, gmix_ref, w_ref, gq_ref, gk_ref, c_ref, s1_ref, s2_ref, bm_ref,
              q_ref, k_ref, v_ref, kf_ref, vf_ref):
    h = _rms(x_ref[...], gmix_ref[...]).astype(BF16)
    c, s1, s2 = c_ref[...], s1_ref[...], s2_ref[...]
    bm = bm_ref[...]

    def head_norm_rope(t, gain):
        slabs = []
        for a in range(D_ATTN // GROUP_DIM):
            ts = t[:, a * GROUP_DIM:(a + 1) * GROUP_DIM]
            sq = ts * ts
            hi = sq.astype(BF16)
            lo = (sq - hi.astype(F32)).astype(BF16)
            ms = _dot(hi, bm) + _dot(lo, bm)
            tn = ts * lax.rsqrt(ms + EPS) * gain[:, a * GROUP_DIM:(a + 1) * GROUP_DIM]
            for b in range(GROUP_DIM // LANES):
                th = tn[:, b * LANES:(b + 1) * LANES]
                slabs.append(th * c + pltpu.roll(th, LANES - ROPE_DIM // 2, 1) * s1
                             + pltpu.roll(th, ROPE_DIM // 2, 1) * s2)
        return jnp.concatenate(slabs, axis=1)

    q = head_norm_rope(_dot(h, w_ref[:, 0:D_ATTN]), gq_ref[...])
    q_ref[...] = (q * np.float32(HEAD_DIM ** -0.5)).astype(BF16)
    k = head_norm_rope(_dot(h, w_ref[:, D_ATTN:2 * D_ATTN]), gk_ref[...])
    kf_ref[...] = k
    k_ref[...] = k.astype(BF16)
    v = _dot(h, w_ref[:, 2 * D_ATTN:3 * D_ATTN])
    vf_ref[...] = v
    v_ref[...] = v.astype(BF16)


def _qkv(x, n_prompt, seq, gmix, w_qkv, q_norm_g, k_norm_g):
    n = x.shape[0]
    n_sample = n - n_prompt
    tiles_per_seq = seq // ROW_TILE
    prompt_tiles = n_prompt // ROW_TILE
    c, s1, s2 = _rope_tables(seq, n_sample)
    heads = D_ATTN // HEAD_DIM
    gq = jnp.tile(q_norm_g, heads).reshape(1, D_ATTN)
    gk = jnp.tile(k_norm_g, heads).reshape(1, D_ATTN)
    lane_head = np.arange(GROUP_DIM) // HEAD_DIM
    bm = jnp.asarray((lane_head[:, None] == lane_head[None, :]) / HEAD_DIM, BF16)
    table = pl.BlockSpec(
        (ROW_TILE, LANES),
        lambda i: (jnp.where(i < prompt_tiles, i % tiles_per_seq, tiles_per_seq), 0))
    return pl.pallas_call(
        _qkv_body,
        grid=(n // ROW_TILE,),
        in_specs=[_row_spec(D_MODEL), _resident((1, D_MODEL)), _resident((D_MODEL, 3 * D_ATTN)),
                  _resident((1, D_ATTN)), _resident((1, D_ATTN)), table, table, table,
                  _resident((GROUP_DIM, GROUP_DIM))],
        out_specs=[_row_spec(D_ATTN)] * 5,
        out_shape=[jax.ShapeDtypeStruct((n, D_ATTN), BF16)] * 3
        + [jax.ShapeDtypeStruct((n, D_ATTN), F32)] * 2,
        compiler_params=_params("parallel"),
        name="qkv",
    )(x, gmix.reshape(1, D_MODEL), w_qkv.astype(BF16), gq, gk, c, s1, s2, bm)


def _attn_prompt_body(q_ref, kc_ref, kp_ref, vc_ref, vp_ref, o_ref, l_ref, *, q_tile):
    first_key = jnp.where(pl.program_id(2) == 0, ATTN_BLOCK, 0)
    lane = lax.broadcasted_iota(jnp.int32, (1, GROUP_DIM), 1)
    head_masks = [lane // HEAD_DIM == hd for hd in range(HEADS_PER_GROUP)]
    qi = lax.broadcasted_iota(jnp.int32, (ATTN_BLOCK, 2 * ATTN_BLOCK), 0)
    kj = lax.broadcasted_iota(jnp.int32, (ATTN_BLOCK, 2 * ATTN_BLOCK), 1)
    dist = qi + ATTN_BLOCK - kj
    band = (dist >= 0) & (dist <= ATTN_BLOCK)
    band_first = band & (kj >= first_key)
    for j in range(q_tile // ATTN_BLOCK):
        rows = slice(j * ATTN_BLOCK, (j + 1) * ATTN_BLOCK)
        q = q_ref[rows, :]
        if j == 0:
            k_prev, v_prev, mask = kp_ref[...], vp_ref[...], band_first
        else:
            prev = slice((j - 1) * ATTN_BLOCK, j * ATTN_BLOCK)
            k_prev, v_prev, mask = kc_ref[prev, :], vc_ref[prev, :], band
        kk = jnp.concatenate([k_prev, kc_ref[rows, :]], axis=0)
        vv = jnp.concatenate([v_prev, vc_ref[rows, :]], axis=0)
        o_acc = jnp.zeros((ATTN_BLOCK, GROUP_DIM), F32)
        l_acc = jnp.zeros((ATTN_BLOCK, GROUP_DIM), F32)
        for hm in head_masks:
            s = _dot_nt(jnp.where(hm, q, jnp.zeros_like(q)), kk)
            s = jnp.where(mask, s, -jnp.inf)
            m = jnp.max(s, axis=-1, keepdims=True)
            p = jnp.exp(s - m)
            l = jnp.sum(p, axis=-1, keepdims=True)
            pv = _dot(p.astype(BF16), vv)
            o_acc = o_acc + jnp.where(hm, pv * (1.0 / l), 0.0)
            l_acc = l_acc + jnp.where(hm, m + jnp.log(l), 0.0)
        o_ref[rows, :] = o_acc
        l_ref[rows, :] = l_acc


def _attn_prompt(q, k, v, group, batch, seq):
    n = q.shape[0]
    dil = DILATIONS[group]
    assert WINDOWS[group] // dil == ATTN_BLOCK
    stream = seq // dil
    q_tile = min(ROW_TILE, stream)
    tiles = stream // q_tile
    blocks_per_tile = q_tile // ATTN_BLOCK
    view = lambda a: a.reshape(n // dil, dil * D_ATTN)
    col = lambda r: r * N_ATTN_GROUPS + group
    cur = pl.BlockSpec((q_tile, GROUP_DIM), lambda b, r, t: (b * tiles + t, col(r)))
    prev = pl.BlockSpec(
        (ATTN_BLOCK, GROUP_DIM),
        lambda b, r, t: (jnp.maximum((b * tiles + t) * blocks_per_tile - 1, 0), col(r)))
    out = pl.BlockSpec((q_tile, GROUP_DIM), lambda b, r, t: (b * tiles + t, r))
    shape = jax.ShapeDtypeStruct((batch * stream, dil * GROUP_DIM), F32)
    o, lse = pl.pallas_call(
        functools.partial(_attn_prompt_body, q_tile=q_tile),
        grid=(batch, dil, tiles),
        in_specs=[cur, cur, prev, cur, prev],
        out_specs=[out, out],
        out_shape=[shape, shape],
        compiler_params=_params("parallel", "parallel", "arbitrary"),
        name=f"attn_prompt_g{group}",
    )(view(q), view(k), view(k), view(v), view(v))
    return o.reshape(batch * seq, GROUP_DIM), lse.reshape(batch * seq, GROUP_DIM)


def _attn_sample_body(q_ref, kn_ref, vn_ref, c_ref, seg_ref, exp_ref, o_ref, lse_ref, *, residues):
    bb = q_ref.shape[0]
    seg, expand = seg_ref[...], exp_ref[...]
    row = lax.broadcasted_iota(jnp.int32, (1, ATTN_BLOCK, 1), 1)
    new_row = lax.broadcasted_iota(jnp.int32, (1, SUBLANES, 1), 1)
    o_ref[...] = jnp.zeros_like(o_ref)
    lse_ref[...] = jnp.zeros_like(lse_ref)
    for t in range(DEC_SEQ):
        r = 0 if residues == 1 else t
        base = r * 2 * GROUP_DIM
        q = q_ref[:, t:t + 1, :]
        prod = (c_ref[:, :, base:base + GROUP_DIM] * q).astype(BF16)
        s = _dot(prod.reshape(bb * ATTN_BLOCK, GROUP_DIM), seg).reshape(bb, ATTN_BLOCK, LANES)
        prod_n = (kn_ref[...] * q).astype(BF16)
        s_n = _dot(prod_n.reshape(bb * SUBLANES, GROUP_DIM), seg).reshape(bb, SUBLANES, LANES)
        if residues == 1:
            s = jnp.where(row >= t, s, -jnp.inf)
            s_n = jnp.where(new_row <= t, s_n, -jnp.inf)
        else:
            s_n = jnp.where(new_row == t, s_n, -jnp.inf)
        m = jnp.maximum(jnp.max(s, axis=1, keepdims=True), jnp.max(s_n, axis=1, keepdims=True))
        p = jnp.exp(s - m)
        p_n = jnp.exp(s_n - m)
        l = jnp.sum(p, axis=1, keepdims=True) + jnp.sum(p_n, axis=1, keepdims=True)
        inv = 1.0 / l
        pe = _dot((p * inv).astype(BF16).reshape(bb * ATTN_BLOCK, LANES), expand)
        pe_n = _dot((p_n * inv).astype(BF16).reshape(bb * SUBLANES, LANES), expand)
        val = c_ref[:, :, base + GROUP_DIM:base + 2 * GROUP_DIM]
        o = jnp.sum(pe.reshape(bb, ATTN_BLOCK, GROUP_DIM) * val, axis=1, keepdims=True)
        o = o + jnp.sum(pe_n.reshape(bb, SUBLANES, GROUP_DIM) * vn_ref[...], axis=1, keepdims=True)
        o_ref[:, t:t + 1, :] = o
        lse_ref[:, t:t + 1, :] = m + jnp.log(l)


def _attn_sample(qs, ks, vs, cache, group):
    nb, buf = cache.shape[0], cache.shape[1]
    dil = DILATIONS[group]
    assert buf == ATTN_BLOCK * dil and WINDOWS[group] == buf
    residues = 1 if dil == 1 else DEC_SEQ
    assert dil == 1 or dil >= DEC_SEQ
    width = residues * 2 * GROUP_DIM
    bb = 16 if residues == 1 else 4
    pad = lambda a: jnp.pad(a.transpose(1, 0, 2), ((0, 0), (0, SUBLANES - DEC_SEQ), (0, 0)))
    lanes_per_head = LANES // HEADS_PER_GROUP
    seg = jnp.asarray(np.arange(GROUP_DIM)[:, None] // HEAD_DIM
                      == np.arange(LANES)[None, :] // lanes_per_head, BF16)
    expand = jnp.asarray(np.arange(LANES)[:, None]
                         == (np.arange(GROUP_DIM)[None, :] // HEAD_DIM) * lanes_per_head, BF16)
    small = pl.BlockSpec((bb, SUBLANES, GROUP_DIM), lambda i: (i, 0, 0))
    o, lse = pl.pallas_call(
        functools.partial(_attn_sample_body, residues=residues),
        grid=(nb // bb,),
        in_specs=[small, small, small,
                  pl.BlockSpec((bb, ATTN_BLOCK, width), lambda i: (i, 0, 0)),
                  _resident((GROUP_DIM, LANES)), _resident((LANES, GROUP_DIM))],
        out_specs=[small, pl.BlockSpec((bb, SUBLANES, LANES), lambda i: (i, 0, 0))],
        out_shape=[jax.ShapeDtypeStruct((nb, SUBLANES, GROUP_DIM), F32),
                   jax.ShapeDtypeStruct((nb, SUBLANES, LANES), F32)],
        compiler_params=_params("parallel"),
        name=f"attn_sample_g{group}",
    )(pad(qs), pad(ks), pad(vs), cache.reshape(nb, ATTN_BLOCK, dil * 2 * GROUP_DIM), seg, expand)
    lse = jnp.repeat(lse[:, :DEC_SEQ, ::lanes_per_head], HEAD_DIM, axis=-1)
    return o[:, :DEC_SEQ], lse


def _merge_body(x_ref, o0_ref, o1_ref, o2_ref, l0_ref, l1_ref, l2_ref, w_ref, out_ref):
    l0, l1, l2 = l0_ref[...], l1_ref[...], l2_ref[...]
    m = jnp.maximum(jnp.maximum(l0, l1), l2)
    e0, e1, e2 = jnp.exp(l0 - m), jnp.exp(l1 - m), jnp.exp(l2 - m)
    inv = 1.0 / (e0 + e1 + e2)
    y = x_ref[...]
    for g, (o_ref, e) in enumerate(((o0_ref, e0), (o1_ref, e1), (o2_ref, e2))):
        y = y + _dot((o_ref[...] * (e * inv)).astype(BF16),
                     w_ref[g * GROUP_DIM:(g + 1) * GROUP_DIM, :])
    out_ref[...] = y


def _merge_out_proj(x, outs, lses, w_out, first_tile, tiles):
    n = x.shape[0]
    part = pl.BlockSpec((ROW_TILE, GROUP_DIM), lambda i: (i, 0))
    return pl.pallas_call(
        _merge_body,
        grid=(tiles,),
        in_specs=[_row_spec(D_MODEL, first_tile)] + [part] * 6 + [_resident((D_ATTN, D_MODEL))],
        out_specs=_row_spec(D_MODEL, first_tile),
        out_shape=jax.ShapeDtypeStruct((n, D_MODEL), F32),
        input_output_aliases={0: 0},
        compiler_params=_params("parallel"),
        name="attn_merge_out_proj",
    )(x, *outs, *lses, w_out)


def _attn_layer(x, n_prompt, batch, caches, gmix, w_qkv, q_norm_g, k_norm_g, w_out):
    n = x.shape[0]
    seq = n_prompt // batch
    nb = (n - n_prompt) // DEC_SEQ
    q, k, v, kf, vf = _qkv(x, n_prompt, seq, gmix, w_qkv, q_norm_g, k_norm_g)
    sample = lambda a: a[n_prompt:].reshape(DEC_SEQ, nb, D_ATTN)
    qs, ks, vs = sample(q).astype(F32), sample(kf), sample(vf)
    kp = kf[:n_prompt].reshape(batch, seq, D_ATTN)
    vp = vf[:n_prompt].reshape(batch, seq, D_ATTN)
    outs_p, lses_p, outs_s, lses_s, kv_p, kv_s = [], [], [], [], [], []
    for g in range(N_ATTN_GROUPS):
        cols = slice(g * GROUP_DIM, (g + 1) * GROUP_DIM)
        o, lse = _attn_prompt(q, k, v, g, batch, seq)
        outs_p.append(o)
        lses_p.append(lse)
        o, lse = _attn_sample(qs[:, :, cols], ks[:, :, cols], vs[:, :, cols], caches[g], g)
        outs_s.append(o.transpose(1, 0, 2).reshape(DEC_SEQ * nb, GROUP_DIM))
        lses_s.append(lse.transpose(1, 0, 2).reshape(DEC_SEQ * nb, GROUP_DIM))
        keep = min(WINDOWS[g], seq)
        heads = (HEADS_PER_GROUP, HEAD_DIM)
        kv_p.append(jnp.stack([kp[:, seq - keep:, cols], vp[:, seq - keep:, cols]], axis=2)
                    .reshape(batch, keep, 2, *heads))
        kv_s.append(jnp.stack([ks[:, :, cols], vs[:, :, cols]], axis=2)
                    .transpose(1, 0, 2, 3).reshape(nb, DEC_SEQ, 2, *heads))
    wout = w_out.astype(BF16)
    x = _merge_out_proj(x, outs_p, lses_p, wout, 0, n_prompt // ROW_TILE)
    x = _merge_out_proj(x, outs_s, lses_s, wout, n_prompt // ROW_TILE, (n - n_prompt) // ROW_TILE)
    return x, kv_p, kv_s


def _ffn_body(x_ref, g_ref, wg_ref, wu_ref, wd_ref, o_ref):
    x = x_ref[...]
    h = _rms(x, g_ref[...]).astype(BF16)
    y = x
    for c in range(D_FF // FF_CHUNK):
        cols = slice(c * FF_CHUNK, (c + 1) * FF_CHUNK)
        act = _silu(_dot(h, wg_ref[:, cols])) * _dot(h, wu_ref[:, cols])
        y = y + _dot(act.astype(BF16), wd_ref[cols, :])
    o_ref[...] = y


def _dense_ffn(x, g, w_gu, w_down):
    n = x.shape[0]
    wgu = w_gu.astype(BF16)
    half = pl.Buffered(1)
    return pl.pallas_call(
        _ffn_body,
        grid=(n // ROW_TILE,),
        in_specs=[_row_spec(D_MODEL), _resident((1, D_MODEL)),
                  pl.BlockSpec((D_MODEL, D_FF), lambda i: (0, 0), pipeline_mode=half),
                  pl.BlockSpec((D_MODEL, D_FF), lambda i: (0, 1), pipeline_mode=half),
                  _resident((D_FF, D_MODEL))],
        out_specs=_row_spec(D_MODEL),
        out_shape=jax.ShapeDtypeStruct((n, D_MODEL), F32),
        input_output_aliases={0: 0},
        compiler_params=_params("parallel"),
        name="dense_ffn",
    )(x, g.reshape(1, D_MODEL), wgu, wgu, w_down.astype(BF16))


def _router_body(x_ref, g_ref, r_ref, hb_ref, idx_ref, gate_ref):
    h = _rms(x_ref[...], g_ref[...])
    hb_ref[...] = h.astype(BF16)
    logits = jnp.dot(h, r_ref[...], precision=lax.Precision.HIGHEST, preferred_element_type=F32)
    lt = logits.T[:N_EXPERTS, :]
    e = lax.broadcasted_iota(jnp.int32, lt.shape, 0)
    m1 = jnp.max(lt, axis=0, keepdims=True)
    e1 = jnp.min(jnp.where(lt == m1, e, N_EXPERTS), axis=0, keepdims=True)
    rest = jnp.where(e == e1, -jnp.inf, lt)
    m2 = jnp.max(rest, axis=0, keepdims=True)
    e2 = jnp.min(jnp.where(rest == m2, e, N_EXPERTS), axis=0, keepdims=True)
    a = jnp.exp(m2 - m1)
    inv = 1.0 / (1.0 + a)
    idx_ref[...] = jnp.concatenate([e1, e2], axis=0)
    gate_ref[...] = jnp.concatenate([inv, a * inv], axis=0)


def _router(x, g, router):
    n = x.shape[0]
    r = jnp.pad(router, ((0, 0), (0, LANES - N_EXPERTS)))
    pair = pl.BlockSpec((2, ROW_TILE), lambda i: (0, i))
    return pl.pallas_call(
        _router_body,
        grid=(n // ROW_TILE,),
        in_specs=[_row_spec(D_MODEL), _resident((1, D_MODEL)), _resident((D_MODEL, LANES))],
        out_specs=[_row_spec(D_MODEL), pair, pair],
        out_shape=[jax.ShapeDtypeStruct((n, D_MODEL), BF16),
                   jax.ShapeDtypeStruct((2, n), jnp.int32),
                   jax.ShapeDtypeStruct((2, n), F32)],
        compiler_params=_params("parallel"),
        name="moe_router",
    )(x, g.reshape(1, D_MODEL), r)


def _moe_body(te_ref, tv_ref, xs_ref, wg_ref, wu_ref, wd_ref, gs_ref, y_ref, acc_ref):
    i, f = pl.program_id(0), pl.program_id(1)

    @pl.when(f == 0)
    def _():
        acc_ref[...] = jnp.zeros_like(acc_ref)

    @pl.when(tv_ref[i] > 0)
    def _():
        xb = xs_ref[...]
        act = _silu(_dot(xb, wg_ref[0].astype(BF16))) * _dot(xb, wu_ref[0].astype(BF16))
        acc_ref[...] += _dot(act.astype(BF16), wd_ref[0].astype(BF16))

    @pl.when(f == MOE_FF_STEPS - 1)
    def _():
        y_ref[...] = acc_ref[...] * gs_ref[...]


def _moe_experts(xs, gs, tile_expert, tile_valid, w_gu, w_down):
    rows = xs.shape[0]
    tiles = rows // MOE_TILE
    last = MOE_FF_STEPS - 1
    fstep = lambda i, f, tv: jnp.where(tv[i] > 0, f, last)
    grid_spec = pltpu.PrefetchScalarGridSpec(
        num_scalar_prefetch=2,
        grid=(tiles, MOE_FF_STEPS),
        in_specs=[
            pl.BlockSpec((MOE_TILE, D_MODEL), lambda i, f, te, tv: (i, 0)),
            pl.BlockSpec((1, D_MODEL, MOE_FF_CHUNK), lambda i, f, te, tv: (te[i], 0, fstep(i, f, tv))),
            pl.BlockSpec((1, D_MODEL, MOE_FF_CHUNK),
                         lambda i, f, te, tv: (te[i], 0, MOE_FF_STEPS + fstep(i, f, tv))),
            pl.BlockSpec((1, MOE_FF_CHUNK, D_MODEL), lambda i, f, te, tv: (te[i], fstep(i, f, tv), 0)),
            pl.BlockSpec((MOE_TILE, 1), lambda i, f, te, tv: (i, 0)),
        ],
        out_specs=pl.BlockSpec((MOE_TILE, D_MODEL), lambda i, f, te, tv: (i, 0)),
        scratch_shapes=[pltpu.VMEM((MOE_TILE, D_MODEL), F32)],
    )
    return pl.pallas_call(
        _moe_body,
        grid_spec=grid_spec,
        out_shape=jax.ShapeDtypeStruct((rows, D_MODEL), F32),
        compiler_params=_params("parallel", "arbitrary"),
        name="moe_experts",
    )(tile_expert, tile_valid, xs, w_gu, w_gu, w_down, gs)


def _route(idx, gates, n):
    assign = 2 * n
    tiles = -(-(assign + N_EXPERTS * (MOE_TILE - 1)) // MOE_TILE)
    ea = idx.reshape(assign)
    ga = gates.reshape(assign)
    onehot = (ea[:, None] == jnp.arange(N_EXPERTS, dtype=jnp.int32)[None, :]).astype(jnp.int32)
    csum = jnp.cumsum(onehot, axis=0)
    counts = csum[-1]
    rank = jnp.sum(onehot * csum, axis=1) - 1
    etiles = (counts + MOE_TILE - 1) // MOE_TILE
    tile_end = jnp.cumsum(etiles)
    tile_start = tile_end - etiles
    cnt_start = jnp.cumsum(counts) - counts
    pos = tile_start[ea] * MOE_TILE + rank
    order = jnp.argsort(ea, stable=True).astype(jnp.int32)
    ti = jnp.arange(tiles, dtype=jnp.int32)
    total = tile_end[-1]
    te_raw = jnp.minimum(jnp.sum((ti[:, None] >= tile_end[None, :]).astype(jnp.int32), axis=1),
                         N_EXPERTS - 1)
    active = ti < total
    te = jnp.where(active, te_raw, te_raw[jnp.maximum(total - 1, 0)])
    tv = jnp.where(active, jnp.clip(counts[te] - (ti - tile_start[te]) * MOE_TILE, 0, MOE_TILE), 0)
    p = jnp.arange(tiles * MOE_TILE, dtype=jnp.int32)
    pt = p // MOE_TILE
    pe = te[pt]
    off = p - tile_start[pe] * MOE_TILE
    valid = active[pt] & (off < counts[pe])
    a = order[jnp.clip(cnt_start[pe] + off, 0, assign - 1)]
    src = jnp.where(valid, a % n, 0)
    gs = jnp.where(valid, ga[a], 0.0)
    return src, gs.reshape(-1, 1), te.astype(jnp.int32), tv.astype(jnp.int32), pos


def _moe_layer(x, g, router, w_gu, w_down):
    n = x.shape[0]
    hb, idx, gates = _router(x, g, router)
    src, gs, te, tv, pos = _route(idx, gates, n)
    xs = jnp.take(hb, src, axis=0)
    y = _moe_experts(xs, gs, te, tv, w_gu, w_down)
    return x + (jnp.take(y, pos[:n], axis=0) + jnp.take(y, pos[n:], axis=0))


def kernel(x_prompt, x_sample, state_conv, cache_kv_w128, cache_kv_w512, cache_kv_w2048, norm_mix_g, norm_ffn_g, a_w_in, a_ln_g, a_ln_b, a_w_s, a_b_s, a_w_out, b_w_in, b_conv_w, b_w_out, c_w_qkv, c_q_norm_g, c_k_norm_g, c_w_out, f_w_gu, f_w_down, m_router, m_w_gu, m_w_down):
    batch, seq, _ = x_prompt.shape
    nb = x_sample.shape[0]
    n_prompt = batch * seq
    caches = (cache_kv_w128, cache_kv_w512, cache_kv_w2048)
    x = jnp.concatenate([x_prompt.reshape(n_prompt, D_MODEL),
                         x_sample.transpose(1, 0, 2).reshape(DEC_SEQ * nb, D_MODEL)], axis=0)
    depth = norm_mix_g.shape[0]
    sgu_v, conv_p, conv_s = [], [], []
    kv_p = [[] for _ in range(N_ATTN_GROUPS)]
    kv_s = [[] for _ in range(N_ATTN_GROUPS)]
    for i in range(depth):
        kind, j = i % 3, i // 3
        if kind == 0:
            x, v_rows = _sgu_layer(x, n_prompt, norm_mix_g[i], a_w_in[j], a_ln_g[j], a_ln_b[j],
                                   a_w_s[j], a_b_s[j], a_w_out[j])
            sgu_v.append(v_rows.reshape(DEC_SEQ, nb, D_GATE).transpose(1, 0, 2))
        elif kind == 1:
            x, tail_p, tail_s = _conv_layer(x, n_prompt, batch, state_conv[j], norm_mix_g[i],
                                            b_w_in[j], b_conv_w[j], b_w_out[j])
            conv_p.append(tail_p)
            conv_s.append(tail_s)
        else:
            x, kvp, kvs = _attn_layer(x, n_prompt, batch, [c[j] for c in caches], norm_mix_g[i],
                                      c_w_qkv[j], c_q_norm_g[j], c_k_norm_g[j], c_w_out[j])
            for g in range(N_ATTN_GROUPS):
                kv_p[g].append(kvp[g])
                kv_s[g].append(kvs[g])
        e = i // 2
        if i % 2 == 0:
            x = _dense_ffn(x, norm_ffn_g[i], f_w_gu[e], f_w_down[e])
        else:
            x = _moe_layer(x, norm_ffn_g[i], m_router[e], m_w_gu[e], m_w_down[e])
    y_prompt = x[:n_prompt].reshape(batch, seq, D_MODEL)
    y_sample = x[n_prompt:].reshape(DEC_SEQ, nb, D_MODEL).transpose(1, 0, 2)
    stack = lambda xs: jnp.stack(xs, axis=0)
    return (y_prompt, y_sample, stack(conv_p), stack(conv_s),
            stack(kv_p[0]), stack(kv_s[0]), stack(kv_p[1]), stack(kv_s[1]),
            stack(kv_p[2]), stack(kv_s[2]), stack(sgu_v))
```

```python
import functools

import numpy as np
import jax
import jax.numpy as jnp
from jax import lax
from jax.experimental import pallas as pl
from jax.experimental.pallas import tpu as pltpu

F32 = jnp.float32
BF16 = jnp.bfloat16

D_MODEL = 1024
EPS = 1e-6
CHUNK = 128
D_GATE = 2 * D_MODEL
SGU_GROUPS = 8
SGU_GDIM = D_GATE // SGU_GROUPS
D_CONV = D_MODEL
HEAD_DIM = 64
HEADS_PER_GROUP = 4
GROUP_DIM = HEADS_PER_GROUP * HEAD_DIM
WINDOWS = (128, 512, 2048)
DILATIONS = (1, 4, 16)
N_ATTN_GROUPS = 3
D_ATTN = N_ATTN_GROUPS * GROUP_DIM
ROPE_DIM = HEAD_DIM // 4
ROPE_THETA = 500000.0
ATTN_BLOCK = 128
D_FF = 2816
N_EXPERTS = 8
D_FF_EXPERT = 3584
DEC_SEQ = 4
PAST_LEN = 2048

SUBLANES = 8
LANES = 128
VMEM_LIMIT = 56 * 1024 * 1024

ROW_TILE = 512
FF_CHUNK = D_FF // 2
MOE_TILE = 1024
MOE_FF_CHUNK = 512
MOE_FF_STEPS = D_FF_EXPERT // MOE_FF_CHUNK


def _dot(a, b):
    return jnp.dot(a, b, preferred_element_type=F32)


def _dot_nt(a, b):
    return lax.dot_general(a, b, (((1,), (1,)), ((), ())), preferred_element_type=F32)


def _rms(x, g):
    return x * lax.rsqrt(jnp.mean(x * x, axis=-1, keepdims=True) + EPS) * g


def _gelu_tanh(x):
    c = np.float32(np.sqrt(2 / np.pi))
    return x * (0.5 * (1.0 + jnp.tanh(c * (x + np.float32(0.044715) * (x * x * x)))))


def _silu(x):
    return x * (1.0 / (1.0 + jnp.exp(-x)))


def _params(*sem):
    return pltpu.CompilerParams(dimension_semantics=sem, vmem_limit_bytes=VMEM_LIMIT)


def _resident(shape):
    zeros = (0,) * len(shape)
    return pl.BlockSpec(shape, lambda *_: zeros, pipeline_mode=pl.Buffered(1))


def _row_spec(width, first_tile=0):
    return pl.BlockSpec((ROW_TILE, width), lambda i: (i + first_tile, 0))


def _sgu_front(x, gmix, winv_ref, lng, lnb):
    h = _rms(x, gmix).astype(BF16)
    v = _gelu_tanh(_dot(h, winv_ref[...]))
    mu = jnp.mean(v, axis=-1, keepdims=True)
    vc = v - mu
    var = jnp.mean(vc * vc, axis=-1, keepdims=True)
    return h, vc * lax.rsqrt(var + EPS) * lng + lnb


def _sgu_back(x, h, winu_ref, s_ref, wout_ref, o_ref):
    u = _gelu_tanh(_dot(h, winu_ref[...]))
    o_ref[...] = x + _dot((u * s_ref[...]).astype(BF16), wout_ref[...])


def _sgu_prompt_body(x_ref, gmix_ref, winu_ref, winv_ref, lng_ref, lnb_ref, ws_ref, bs_ref,
                     wout_ref, o_ref, s_ref):
    x = x_ref[...]
    h, vn = _sgu_front(x, gmix_ref[...], winv_ref, lng_ref[...], lnb_ref[...])
    vb = vn.astype(BF16)
    row = lax.broadcasted_iota(jnp.int32, (CHUNK, CHUNK), 0)
    col = lax.broadcasted_iota(jnp.int32, (CHUNK, CHUNK), 1)
    for g in range(SGU_GROUPS):
        wm = jnp.where(row >= col, ws_ref[g], 0.0).astype(BF16)
        bias = jnp.concatenate([bs_ref[g]] * (SGU_GDIM // LANES), axis=1)
        cols = slice(g * SGU_GDIM, (g + 1) * SGU_GDIM)
        for c in range(ROW_TILE // CHUNK):
            rows = slice(c * CHUNK, (c + 1) * CHUNK)
            s_ref[rows, cols] = _dot(wm, vb[rows, cols]) + bias
    _sgu_back(x, h, winu_ref, s_ref, wout_ref, o_ref)


def _sgu_sample_body(w4_ref, b4_ref, x_ref, gmix_ref, winu_ref, winv_ref, lng_ref, lnb_ref,
                     wout_ref, o_ref, v_ref, s_ref):
    x = x_ref[...]
    nb = x.shape[0] // DEC_SEQ
    h, vn = _sgu_front(x, gmix_ref[...], winv_ref, lng_ref[...], lnb_ref[...])
    v_ref[...] = vn
    for g in range(SGU_GROUPS):
        cols = slice(g * SGU_GDIM, (g + 1) * SGU_GDIM)
        for t in range(DEC_SEQ):
            acc = w4_ref[g, t * DEC_SEQ] * vn[0:nb, cols]
            for j in range(1, t + 1):
                acc = acc + w4_ref[g, t * DEC_SEQ + j] * vn[j * nb:(j + 1) * nb, cols]
            s_ref[t * nb:(t + 1) * nb, cols] = acc + b4_ref[g, t]
    _sgu_back(x, h, winu_ref, s_ref, wout_ref, o_ref)


def _sgu_layer(x, n_prompt, gmix, w_in, ln_g, ln_b, w_s, b_s, w_out):
    n = x.shape[0]
    n_sample = n - n_prompt
    assert n_sample == ROW_TILE and n_prompt % ROW_TILE == 0
    winu = w_in[:, :D_GATE].astype(BF16)
    winv = w_in[:, D_GATE:].astype(BF16)
    wout = w_out.astype(BF16)
    gmix = gmix.reshape(1, D_MODEL)
    lng = ln_g.reshape(1, D_GATE)
    lnb = ln_b.reshape(1, D_GATE)
    bs = jnp.broadcast_to(b_s[:, :, None], (SGU_GROUPS, CHUNK, LANES))
    weights = [_resident((1, D_MODEL)), _resident((D_MODEL, D_GATE)), _resident((D_MODEL, D_GATE)),
               _resident((1, D_GATE)), _resident((1, D_GATE))]
    x = pl.pallas_call(
        _sgu_prompt_body,
        grid=(n_prompt // ROW_TILE,),
        in_specs=[_row_spec(D_MODEL)] + weights + [
            _resident((SGU_GROUPS, CHUNK, CHUNK)), _resident((SGU_GROUPS, CHUNK, LANES)),
            _resident((D_GATE, D_MODEL))],
        out_specs=_row_spec(D_MODEL),
        out_shape=jax.ShapeDtypeStruct((n, D_MODEL), F32),
        scratch_shapes=[pltpu.VMEM((ROW_TILE, D_GATE), F32)],
        input_output_aliases={0: 0},
        compiler_params=_params("parallel"),
        name="sgu_prompt",
    )(x, gmix, winu, winv, lng, lnb, w_s, bs, wout)
    w4 = w_s[:, :DEC_SEQ, :DEC_SEQ].reshape(SGU_GROUPS, DEC_SEQ * DEC_SEQ)
    b4 = b_s[:, :DEC_SEQ]
    smem = pl.BlockSpec(memory_space=pltpu.SMEM)
    first = n_prompt // ROW_TILE
    x, v_rows = pl.pallas_call(
        _sgu_sample_body,
        grid=(1,),
        in_specs=[smem, smem, _row_spec(D_MODEL, first)] + weights + [_resident((D_GATE, D_MODEL))],
        out_specs=[_row_spec(D_MODEL, first), pl.BlockSpec((ROW_TILE, D_GATE), lambda i: (0, 0))],
        out_shape=[jax.ShapeDtypeStruct((n, D_MODEL), F32),
                   jax.ShapeDtypeStruct((n_sample, D_GATE), F32)],
        scratch_shapes=[pltpu.VMEM((ROW_TILE, D_GATE), F32)],
        input_output_aliases={2: 0},
        compiler_params=_params("arbitrary"),
        name="sgu_sample",
    )(w4, b4, x, gmix, winu, winv, lng, lnb, wout)
    return x, v_rows


def _conv_front(x, gmix, win_ref):
    h = _rms(x, gmix).astype(BF16)
    gate_b = _dot(h, win_ref[:, 0:D_CONV])
    gate_c = _dot(h, win_ref[:, D_CONV:2 * D_CONV])
    xi = _dot(h, win_ref[:, 2 * D_CONV:3 * D_CONV])
    return gate_b, gate_c * xi


def _conv_prompt_body(x_ref, gmix_ref, win_ref, cw_ref, wout_ref, o_ref, tail_ref, zbuf, *,
                      tiles_per_seq):
    @pl.when(pl.program_id(0) % tiles_per_seq == 0)
    def _():
        zbuf[0:SUBLANES, :] = jnp.zeros((SUBLANES, D_CONV), F32)

    x = x_ref[...]
    gate_b, z = _conv_front(x, gmix_ref[...], win_ref)
    zbuf[SUBLANES:SUBLANES + ROW_TILE, :] = z
    conv = (cw_ref[2:3, :] * z
            + cw_ref[1:2, :] * zbuf[SUBLANES - 1:SUBLANES - 1 + ROW_TILE, :]
            + cw_ref[0:1, :] * zbuf[SUBLANES - 2:SUBLANES - 2 + ROW_TILE, :])
    o_ref[...] = x + _dot((gate_b * conv).astype(BF16), wout_ref[...])
    tail = zbuf[ROW_TILE + SUBLANES - 2:ROW_TILE + SUBLANES, :]
    tail_ref[0] = tail
    zbuf[SUBLANES - 2:SUBLANES, :] = tail


def _conv_sample_body(x_ref, gmix_ref, win_ref, cw_ref, st_ref, wout_ref, o_ref, tail_ref):
    x = x_ref[...]
    nb = x.shape[0] // DEC_SEQ
    gate_b, z = _conv_front(x, gmix_ref[...], win_ref)
    z1 = jnp.concatenate([st_ref[1], z[:(DEC_SEQ - 1) * nb]], axis=0)
    z2 = jnp.concatenate([st_ref[0], st_ref[1], z[:(DEC_SEQ - 2) * nb]], axis=0)
    conv = cw_ref[2:3, :] * z + cw_ref[1:2, :] * z1 + cw_ref[0:1, :] * z2
    o_ref[...] = x + _dot((gate_b * conv).astype(BF16), wout_ref[...])
    tail_ref[...] = z[(DEC_SEQ - 2) * nb:]


def _conv_layer(x, n_prompt, batch, state, gmix, w_in, conv_w, w_out):
    n = x.shape[0]
    n_sample = n - n_prompt
    nb = n_sample // DEC_SEQ
    assert n_sample == ROW_TILE
    tiles_per_seq = n_prompt // batch // ROW_TILE
    win = w_in.astype(BF16)
    wout = w_out.astype(BF16)
    gmix = gmix.reshape(1, D_MODEL)
    weights = [_resident((1, D_MODEL)), _resident((D_MODEL, 3 * D_CONV)), _resident((3, D_CONV))]
    x, tail_p = pl.pallas_call(
        functools.partial(_conv_prompt_body, tiles_per_seq=tiles_per_seq),
        grid=(n_prompt // ROW_TILE,),
        in_specs=[_row_spec(D_MODEL)] + weights + [_resident((D_CONV, D_MODEL))],
        out_specs=[_row_spec(D_MODEL),
                   pl.BlockSpec((1, 2, D_CONV), lambda i: (i // tiles_per_seq, 0, 0))],
        out_shape=[jax.ShapeDtypeStruct((n, D_MODEL), F32),
                   jax.ShapeDtypeStruct((batch, 2, D_CONV), F32)],
        scratch_shapes=[pltpu.VMEM((ROW_TILE + SUBLANES, D_CONV), F32)],
        input_output_aliases={0: 0},
        compiler_params=_params("arbitrary"),
        name="conv_prompt",
    )(x, gmix, win, conv_w, wout)
    first = n_prompt // ROW_TILE
    st = state.transpose(1, 0, 2)
    x, tail_s = pl.pallas_call(
        _conv_sample_body,
        grid=(1,),
        in_specs=[_row_spec(D_MODEL, first)] + weights + [
            _resident((2, nb, D_CONV)), _resident((D_CONV, D_MODEL))],
        out_specs=[_row_spec(D_MODEL, first), pl.BlockSpec((2 * nb, D_CONV), lambda i: (0, 0))],
        out_shape=[jax.ShapeDtypeStruct((n, D_MODEL), F32),
                   jax.ShapeDtypeStruct((2 * nb, D_CONV), F32)],
        input_output_aliases={0: 0},
        compiler_params=_params("arbitrary"),
        name="conv_sample",
    )(x, gmix, win, conv_w, st, wout)
    return x, tail_p, tail_s.reshape(2, nb, D_CONV).transpose(1, 0, 2)


def _rope_tables(seq, n_sample_rows):
    nb = n_sample_rows // DEC_SEQ
    pos = jnp.concatenate([jnp.arange(seq, dtype=jnp.int32),
                           PAST_LEN + jnp.arange(n_sample_rows, dtype=jnp.int32) // nb])
    inv_freq = jnp.power(ROPE_THETA, -jnp.arange(0, ROPE_DIM, 2, dtype=F32) / ROPE_DIM)
    d = np.arange(LANES) % HEAD_DIM
    half = ROPE_DIM // 2
    ang = pos.astype(F32)[:, None] * inv_freq[d % half][None, :]
    cos, sin = jnp.cos(ang), jnp.sin(ang)
    c = jnp.where(d[None, :] < ROPE_DIM, cos, 1.0)
    s1 = jnp.where(d[None, :] < half, -sin, 0.0)
    s2 = jnp.where((d[None, :] >= half) & (d[None, :] < ROPE_DIM), sin, 0.0)
    return c, s1, s2


def _qkv_body(x_ref, gmix_ref, w_ref, gq_ref, gk_ref, c_ref, s1_ref, s2_ref, bm_ref,
              q_ref, k_ref, v_ref):
    h = _rms(x_ref[...], gmix_ref[...]).astype(BF16)
    c, s1, s2 = c_ref[...], s1_ref[...], s2_ref[...]
    bm = bm_ref[...]

    def head_norm_rope(t, gain, out_ref, scale):
        for a in range(D_ATTN // GROUP_DIM):
            ts = t[:, a * GROUP_DIM:(a + 1) * GROUP_DIM]
            sq = ts * ts
            hi = sq.astype(BF16)
            lo = (sq - hi.astype(F32)).astype(BF16)
            ms = _dot(hi, bm) + _dot(lo, bm)
            tn = ts * lax.rsqrt(ms + EPS) * gain[:, a * GROUP_DIM:(a + 1) * GROUP_DIM]
            for b in range(GROUP_DIM // LANES):
                th = tn[:, b * LANES:(b + 1) * LANES]
                rot = (th * c + pltpu.roll(th, LANES - ROPE_DIM // 2, 1) * s1
                       + pltpu.roll(th, ROPE_DIM // 2, 1) * s2)
                out_ref[a * (GROUP_DIM // LANES) + b] = rot if scale is None else rot * scale

    head_norm_rope(_dot(h, w_ref[:, 0:D_ATTN]), gq_ref[...], q_ref, np.float32(HEAD_DIM ** -0.5))
    head_norm_rope(_dot(h, w_ref[:, D_ATTN:2 * D_ATTN]), gk_ref[...], k_ref, None)
    v = _dot(h, w_ref[:, 2 * D_ATTN:3 * D_ATTN])
    for s in range(D_ATTN // LANES):
        v_ref[s] = v[:, s * LANES:(s + 1) * LANES]


def _qkv(x, n_prompt, seq, gmix, w_qkv, q_norm_g, k_norm_g):
    n = x.shape[0]
    n_sample = n - n_prompt
    tiles_per_seq = seq // ROW_TILE
    prompt_tiles = n_prompt // ROW_TILE
    c, s1, s2 = _rope_tables(seq, n_sample)
    heads = D_ATTN // HEAD_DIM
    slabs = D_ATTN // LANES
    gq = jnp.tile(q_norm_g, heads).reshape(1, D_ATTN)
    gk = jnp.tile(k_norm_g, heads).reshape(1, D_ATTN)
    lane_head = np.arange(GROUP_DIM) // HEAD_DIM
    bm = jnp.asarray((lane_head[:, None] == lane_head[None, :]) / HEAD_DIM, BF16)
    table = pl.BlockSpec(
        (ROW_TILE, LANES),
        lambda i: (jnp.where(i < prompt_tiles, i % tiles_per_seq, tiles_per_seq), 0))
    return pl.pallas_call(
        _qkv_body,
        grid=(n // ROW_TILE,),
        in_specs=[_row_spec(D_MODEL), _resident((1, D_MODEL)), _resident((D_MODEL, 3 * D_ATTN)),
                  _resident((1, D_ATTN)), _resident((1, D_ATTN)), table, table, table,
                  _resident((GROUP_DIM, GROUP_DIM))],
        out_specs=[pl.BlockSpec((slabs, ROW_TILE, LANES), lambda i: (0, i, 0))] * 3,
        out_shape=[jax.ShapeDtypeStruct((slabs, n, LANES), F32)] * 3,
        compiler_params=_params("parallel"),
        name="qkv",
    )(x, gmix.reshape(1, D_MODEL), w_qkv.astype(BF16), gq, gk, c, s1, s2, bm)


GROUP_SLABS = GROUP_DIM // LANES


def _attn_prompt_body(q_ref, kc_ref, kp_ref, vc_ref, vp_ref, o_ref, l_ref, *, dil, blocks):
    first_key = jnp.where(pl.program_id(1) == 0, ATTN_BLOCK, 0)
    lane = lax.broadcasted_iota(jnp.int32, (1, GROUP_DIM), 1)
    head_masks = [lane // HEAD_DIM == hd for hd in range(HEADS_PER_GROUP)]
    qi = lax.broadcasted_iota(jnp.int32, (ATTN_BLOCK, 2 * ATTN_BLOCK), 0)
    kj = lax.broadcasted_iota(jnp.int32, (ATTN_BLOCK, 2 * ATTN_BLOCK), 1)
    dist = qi + ATTN_BLOCK - kj
    band = (dist >= 0) & (dist <= ATTN_BLOCK)
    band_first = band & (kj >= first_key)

    def stream_rows(r, j):
        if dil == 1:
            return pl.ds(j * ATTN_BLOCK, ATTN_BLOCK)
        return pl.ds(r + j * ATTN_BLOCK * dil, ATTN_BLOCK, stride=dil)

    def load(ref, rows):
        return jnp.concatenate([ref[s, rows, :] for s in range(GROUP_SLABS)], axis=1)

    for r in range(dil):
        for j in range(blocks):
            rows = stream_rows(r, j)
            q = load(q_ref, rows).astype(BF16)
            if j == 0:
                k_prev, v_prev, mask = load(kp_ref, stream_rows(r, 0)), load(vp_ref, stream_rows(r, 0)), band_first
            else:
                prev = stream_rows(r, j - 1)
                k_prev, v_prev, mask = load(kc_ref, prev), load(vc_ref, prev), band
            kk = jnp.concatenate([k_prev, load(kc_ref, rows)], axis=0).astype(BF16)
            vv = jnp.concatenate([v_prev, load(vc_ref, rows)], axis=0).astype(BF16)
            o_acc = jnp.zeros((ATTN_BLOCK, GROUP_DIM), F32)
            l_acc = jnp.zeros((ATTN_BLOCK, GROUP_DIM), F32)
            for hm in head_masks:
                s = _dot_nt(jnp.where(hm, q, jnp.zeros_like(q)), kk)
                s = jnp.where(mask, s, -jnp.inf)
                m = jnp.max(s, axis=-1, keepdims=True)
                p = jnp.exp(s - m)
                l = jnp.sum(p, axis=-1, keepdims=True)
                pv = _dot(p.astype(BF16), vv)
                o_acc = o_acc + jnp.where(hm, pv * (1.0 / l), 0.0)
                l_acc = l_acc + jnp.where(hm, m + jnp.log(l), 0.0)
            for s in range(GROUP_SLABS):
                o_ref[s, rows, :] = o_acc[:, s * LANES:(s + 1) * LANES]
                l_ref[s, rows, :] = l_acc[:, s * LANES:(s + 1) * LANES]


def _attn_prompt(q, k, v, group, batch, seq):
    dil = DILATIONS[group]
    assert WINDOWS[group] // dil == ATTN_BLOCK
    span = ATTN_BLOCK * dil
    tile = max(ROW_TILE, span)
    tiles = seq // tile
    cur = pl.BlockSpec((GROUP_SLABS, tile, LANES), lambda b, t: (group, b * tiles + t, 0))
    prev = pl.BlockSpec(
        (GROUP_SLABS, span, LANES),
        lambda b, t: (group, jnp.maximum((b * tiles + t) * (tile // span) - 1, 0), 0))
    out = pl.BlockSpec((GROUP_SLABS, tile, LANES), lambda b, t: (0, b * tiles + t, 0))
    shape = jax.ShapeDtypeStruct((GROUP_SLABS, batch * seq, LANES), F32)
    return pl.pallas_call(
        functools.partial(_attn_prompt_body, dil=dil, blocks=tile // span),
        grid=(batch, tiles),
        in_specs=[cur, cur, prev, cur, prev],
        out_specs=[out, out],
        out_shape=[shape, shape],
        compiler_params=_params("parallel", "arbitrary"),
        name=f"attn_prompt_g{group}",
    )(q, k, k, v, v)


def _attn_sample_body(q_ref, knt_ref, vnt_ref, c_ref, o_ref, lse_ref, *, dil, buf, seqs):
    width = buf + LANES
    t = lax.broadcasted_iota(jnp.int32, (SUBLANES, width), 0)
    c = lax.broadcasted_iota(jnp.int32, (SUBLANES, width), 1)
    tq = jnp.where(t < DEC_SEQ, t, t - DEC_SEQ)
    if dil == 1:
        vis = ((c < buf) & (c >= tq)) | ((c >= buf) & (c - buf <= tq))
    else:
        vis = ((c < buf) & (c % dil == tq)) | (c - buf == tq)

    def one(bi):
        for h in range(HEADS_PER_GROUP):
            kall = jnp.concatenate([c_ref[bi, 0, h].astype(BF16), knt_ref[bi, h]], axis=1)
            vall = jnp.concatenate([c_ref[bi, 1, h].astype(BF16), vnt_ref[bi, h]], axis=1)
            s = jnp.where(vis, _dot(q_ref[bi, h], kall), -jnp.inf)
            m = jnp.max(s, axis=1, keepdims=True)
            p = jnp.exp(s - m)
            l = jnp.sum(p, axis=1, keepdims=True)
            o_ref[bi, h] = _dot_nt(vall, (p * (1.0 / l)).astype(BF16))
            lse_ref[bi, h] = jnp.broadcast_to(m + jnp.log(l), (SUBLANES, LANES))

    def pair(i, carry):
        one(2 * i)
        one(2 * i + 1)
        return carry

    if seqs == 2:
        pair(0, 0)
    else:
        lax.fori_loop(0, seqs // 2, pair, 0)


def _attn_sample(q_rows, k_rows, v_rows, cache, group):
    nb, buf = cache.shape[0], cache.shape[1]
    dil = DILATIONS[group]
    assert buf == ATTN_BLOCK * dil and WINDOWS[group] == buf and (dil == 1 or dil >= DEC_SEQ)
    seqs = max(2, 2 * PAST_LEN // buf)
    heads = (HEADS_PER_GROUP, HEAD_DIM)
    per_head = lambda a: a.reshape(DEC_SEQ, nb, *heads)
    qh = jnp.pad(per_head(q_rows).transpose(1, 2, 0, 3),
                 ((0, 0), (0, 0), (0, SUBLANES - DEC_SEQ), (0, 0))).astype(BF16)
    fresh = lambda a: jnp.pad(per_head(a).transpose(1, 2, 3, 0),
                              ((0, 0), (0, 0), (0, 0), (0, LANES - DEC_SEQ))).astype(BF16)
    head_spec = lambda rows, cols: pl.BlockSpec((seqs, HEADS_PER_GROUP, rows, cols), lambda i: (i, 0, 0, 0))
    o_t, lse = pl.pallas_call(
        functools.partial(_attn_sample_body, dil=dil, buf=buf, seqs=seqs),
        grid=(nb // seqs,),
        in_specs=[head_spec(SUBLANES, HEAD_DIM), head_spec(HEAD_DIM, LANES), head_spec(HEAD_DIM, LANES),
                  pl.BlockSpec((seqs, 2, HEADS_PER_GROUP, HEAD_DIM, buf), lambda i: (i, 0, 0, 0, 0))],
        out_specs=[head_spec(HEAD_DIM, SUBLANES), head_spec(SUBLANES, LANES)],
        out_shape=[jax.ShapeDtypeStruct((nb, HEADS_PER_GROUP, HEAD_DIM, SUBLANES), F32),
                   jax.ShapeDtypeStruct((nb, HEADS_PER_GROUP, SUBLANES, LANES), F32)],
        compiler_params=_params("parallel"),
        name=f"attn_sample_g{group}",
    )(qh, fresh(k_rows), fresh(v_rows), cache.transpose(0, 2, 3, 4, 1))
    o = o_t[..., :DEC_SEQ].transpose(3, 0, 1, 2).reshape(DEC_SEQ * nb, GROUP_DIM)
    lse = jnp.repeat(lse[:, :, :DEC_SEQ, 0].transpose(2, 0, 1), HEAD_DIM, axis=-1)
    return o, lse.reshape(DEC_SEQ * nb, GROUP_DIM)


def _to_slabs(rows):
    return rows.reshape(rows.shape[0], GROUP_SLABS, LANES).transpose(1, 0, 2)


def _from_slabs(slabs, group):
    part = slabs[group * GROUP_SLABS:(group + 1) * GROUP_SLABS]
    return part.transpose(1, 0, 2).reshape(part.shape[1], GROUP_DIM)


def _merge_body(x_ref, o0_ref, o1_ref, o2_ref, l0_ref, l1_ref, l2_ref, w_ref, out_ref):
    rows = lambda ref: jnp.concatenate([ref[s] for s in range(GROUP_SLABS)], axis=1)
    l0, l1, l2 = rows(l0_ref), rows(l1_ref), rows(l2_ref)
    m = jnp.maximum(jnp.maximum(l0, l1), l2)
    e0, e1, e2 = jnp.exp(l0 - m), jnp.exp(l1 - m), jnp.exp(l2 - m)
    inv = 1.0 / (e0 + e1 + e2)
    y = x_ref[...]
    for g, (o_ref, e) in enumerate(((o0_ref, e0), (o1_ref, e1), (o2_ref, e2))):
        y = y + _dot((rows(o_ref) * (e * inv)).astype(BF16),
                     w_ref[g * GROUP_DIM:(g + 1) * GROUP_DIM, :])
    out_ref[...] = y


def _merge_out_proj(x, outs, lses, w_out, first_tile, tiles):
    n = x.shape[0]
    part = pl.BlockSpec((GROUP_SLABS, ROW_TILE, LANES), lambda i: (0, i, 0))
    return pl.pallas_call(
        _merge_body,
        grid=(tiles,),
        in_specs=[_row_spec(D_MODEL, first_tile)] + [part] * 6 + [_resident((D_ATTN, D_MODEL))],
        out_specs=_row_spec(D_MODEL, first_tile),
        out_shape=jax.ShapeDtypeStruct((n, D_MODEL), F32),
        input_output_aliases={0: 0},
        compiler_params=_params("parallel"),
        name="attn_merge_out_proj",
    )(x, *outs, *lses, w_out)


def _attn_layer(x, n_prompt, batch, caches, gmix, w_qkv, q_norm_g, k_norm_g, w_out):
    n = x.shape[0]
    seq = n_prompt // batch
    nb = (n - n_prompt) // DEC_SEQ
    q, k, v = _qkv(x, n_prompt, seq, gmix, w_qkv, q_norm_g, k_norm_g)
    qs, ks, vs = q[:, n_prompt:], k[:, n_prompt:], v[:, n_prompt:]
    outs_p, lses_p, outs_s, lses_s, kv_p, kv_s = [], [], [], [], [], []
    for g in range(N_ATTN_GROUPS):
        o, lse = _attn_prompt(q, k, v, g, batch, seq)
        outs_p.append(o)
        lses_p.append(lse)
        k_rows, v_rows = _from_slabs(ks, g), _from_slabs(vs, g)
        o, lse = _attn_sample(_from_slabs(qs, g), k_rows, v_rows, caches[g], g)
        outs_s.append(_to_slabs(o))
        lses_s.append(_to_slabs(lse))
        keep = min(WINDOWS[g], seq)
        heads = (HEADS_PER_GROUP, HEAD_DIM)
        tail = lambda a: _from_slabs(
            a[:, :n_prompt].reshape(-1, batch, seq, LANES)[:, :, seq - keep:].reshape(-1, batch * keep, LANES), g)
        kv_p.append(jnp.stack([tail(k), tail(v)], axis=1).reshape(batch, keep, 2, *heads))
        kv_s.append(jnp.stack([k_rows, v_rows], axis=1).reshape(DEC_SEQ, nb, 2, *heads)
                    .transpose(1, 0, 2, 3, 4))
    wout = w_out.astype(BF16)
    x = _merge_out_proj(x, outs_p, lses_p, wout, 0, n_prompt // ROW_TILE)
    x = _merge_out_proj(x, outs_s, lses_s, wout, n_prompt // ROW_TILE, (n - n_prompt) // ROW_TILE)
    return x, kv_p, kv_s


def _ffn_body(x_ref, g_ref, wg_ref, wu_ref, wd_ref, o_ref):
    x = x_ref[...]
    h = _rms(x, g_ref[...]).astype(BF16)
    y = x
    for c in range(D_FF // FF_CHUNK):
        cols = slice(c * FF_CHUNK, (c + 1) * FF_CHUNK)
        act = _silu(_dot(h, wg_ref[:, cols])) * _dot(h, wu_ref[:, cols])
        y = y + _dot(act.astype(BF16), wd_ref[cols, :])
    o_ref[...] = y


def _dense_ffn(x, g, w_gu, w_down):
    n = x.shape[0]
    wgu = w_gu.astype(BF16)
    half = pl.Buffered(1)
    return pl.pallas_call(
        _ffn_body,
        grid=(n // ROW_TILE,),
        in_specs=[_row_spec(D_MODEL), _resident((1, D_MODEL)),
                  pl.BlockSpec((D_MODEL, D_FF), lambda i: (0, 0), pipeline_mode=half),
                  pl.BlockSpec((D_MODEL, D_FF), lambda i: (0, 1), pipeline_mode=half),
                  _resident((D_FF, D_MODEL))],
        out_specs=_row_spec(D_MODEL),
        out_shape=jax.ShapeDtypeStruct((n, D_MODEL), F32),
        input_output_aliases={0: 0},
        compiler_params=_params("parallel"),
        name="dense_ffn",
    )(x, g.reshape(1, D_MODEL), wgu, wgu, w_down.astype(BF16))


def _router_body(x_ref, g_ref, r_ref, h_ref, idx_ref, gate_ref):
    h = _rms(x_ref[...], g_ref[...])
    h_ref[...] = h
    logits = jnp.dot(h, r_ref[...], precision=lax.Precision.HIGHEST, preferred_element_type=F32)
    lt = logits.T[:N_EXPERTS, :]
    e = lax.broadcasted_iota(jnp.int32, lt.shape, 0)
    m1 = jnp.max(lt, axis=0, keepdims=True)
    e1 = jnp.min(jnp.where(lt == m1, e, N_EXPERTS), axis=0, keepdims=True)
    rest = jnp.where(e == e1, -jnp.inf, lt)
    m2 = jnp.max(rest, axis=0, keepdims=True)
    e2 = jnp.min(jnp.where(rest == m2, e, N_EXPERTS), axis=0, keepdims=True)
    a = jnp.exp(m2 - m1)
    inv = 1.0 / (1.0 + a)
    idx_ref[...] = jnp.concatenate([e1, e2], axis=0)
    gate_ref[...] = jnp.concatenate([inv, a * inv], axis=0)


def _router(x, g, router):
    n = x.shape[0]
    r = jnp.pad(router, ((0, 0), (0, LANES - N_EXPERTS)))
    pair = pl.BlockSpec((2, ROW_TILE), lambda i: (0, i))
    return pl.pallas_call(
        _router_body,
        grid=(n // ROW_TILE,),
        in_specs=[_row_spec(D_MODEL), _resident((1, D_MODEL)), _resident((D_MODEL, LANES))],
        out_specs=[_row_spec(D_MODEL), pair, pair],
        out_shape=[jax.ShapeDtypeStruct((n, D_MODEL), F32),
                   jax.ShapeDtypeStruct((2, n), jnp.int32),
                   jax.ShapeDtypeStruct((2, n), F32)],
        compiler_params=_params("parallel"),
        name="moe_router",
    )(x, g.reshape(1, D_MODEL), r)


def _moe_body(te_ref, tv_ref, xs_ref, wg_ref, wu_ref, wd_ref, gs_ref, y_ref, acc_ref, xb_ref):
    i, f = pl.program_id(0), pl.program_id(1)

    @pl.when(f == 0)
    def _():
        acc_ref[...] = jnp.zeros_like(acc_ref)
        xb_ref[...] = xs_ref[...].astype(BF16)

    @pl.when(tv_ref[i] > 0)
    def _():
        xb = xb_ref[...]
        act = _silu(_dot(xb, wg_ref[0, 0].astype(BF16))) * _dot(xb, wu_ref[0, 0].astype(BF16))
        acc_ref[...] += _dot(act.astype(BF16), wd_ref[0, 0].astype(BF16))

    @pl.when(f == MOE_FF_STEPS - 1)
    def _():
        y_ref[...] = acc_ref[...] * gs_ref[...]


def _moe_experts(xs, gs, tile_expert, tile_valid, w_gu, w_down, layer):
    rows = xs.shape[0]
    tiles = rows // MOE_TILE
    last = MOE_FF_STEPS - 1
    fstep = lambda i, f, tv: jnp.where(tv[i] > 0, f, last)
    grid_spec = pltpu.PrefetchScalarGridSpec(
        num_scalar_prefetch=2,
        grid=(tiles, MOE_FF_STEPS),
        in_specs=[
            pl.BlockSpec((MOE_TILE, D_MODEL), lambda i, f, te, tv: (i, 0)),
            pl.BlockSpec((1, 1, D_MODEL, MOE_FF_CHUNK),
                         lambda i, f, te, tv: (layer, te[i], 0, fstep(i, f, tv))),
            pl.BlockSpec((1, 1, D_MODEL, MOE_FF_CHUNK),
                         lambda i, f, te, tv: (layer, te[i], 0, MOE_FF_STEPS + fstep(i, f, tv))),
            pl.BlockSpec((1, 1, MOE_FF_CHUNK, D_MODEL),
                         lambda i, f, te, tv: (layer, te[i], fstep(i, f, tv), 0)),
            pl.BlockSpec((MOE_TILE, 1), lambda i, f, te, tv: (i, 0)),
        ],
        out_specs=pl.BlockSpec((MOE_TILE, D_MODEL), lambda i, f, te, tv: (i, 0)),
        scratch_shapes=[pltpu.VMEM((MOE_TILE, D_MODEL), F32), pltpu.VMEM((MOE_TILE, D_MODEL), BF16)],
    )
    return pl.pallas_call(
        _moe_body,
        grid_spec=grid_spec,
        out_shape=jax.ShapeDtypeStruct((rows, D_MODEL), F32),
        compiler_params=_params("parallel", "arbitrary"),
        name="moe_experts",
    )(tile_expert, tile_valid, xs, w_gu, w_gu, w_down, gs)


def _route(idx, gates, n):
    assign = 2 * n
    tiles = -(-(assign + N_EXPERTS * (MOE_TILE - 1)) // MOE_TILE)
    experts = jnp.arange(N_EXPERTS, dtype=jnp.int32)
    ea = idx.reshape(assign)
    ga = gates.reshape(assign)
    onehot = (ea[:, None] == experts[None, :]).astype(jnp.int32)
    csum = jnp.cumsum(onehot, axis=0)
    counts = csum[-1]
    etiles = (counts + MOE_TILE - 1) // MOE_TILE
    tile_end = jnp.cumsum(etiles)
    tile_start = tile_end - etiles
    cnt_start = jnp.cumsum(counts) - counts
    pos = jnp.sum(onehot * ((tile_start * MOE_TILE)[None, :] + csum - 1), axis=1)
    token = jnp.arange(assign, dtype=jnp.int32) % n
    _, sorted_token, sorted_gate = lax.sort((ea, token, ga), num_keys=1, is_stable=True)
    ti = jnp.arange(tiles, dtype=jnp.int32)
    active = ti < tile_end[-1]
    te_raw = jnp.sum((ti[:, None] >= tile_end[None, :]).astype(jnp.int32), axis=1)
    last_expert = jnp.max(jnp.where(etiles > 0, experts, 0))
    te = jnp.where(active, jnp.minimum(te_raw, N_EXPERTS - 1), last_expert)
    tile_onehot = (te[:, None] == experts[None, :]).astype(jnp.int32)
    pick = lambda per_expert: jnp.sum(tile_onehot * per_expert[None, :], axis=1)
    tile_in_expert = ti - pick(tile_start)
    tv = jnp.where(active, jnp.clip(pick(counts) - tile_in_expert * MOE_TILE, 0, MOE_TILE), 0)
    first = pick(cnt_start) + tile_in_expert * MOE_TILE
    r = jnp.arange(MOE_TILE, dtype=jnp.int32)
    valid = (r[None, :] < tv[:, None]).reshape(-1)
    sidx = jnp.clip(first[:, None] + r[None, :], 0, assign - 1).reshape(-1)
    src = jnp.where(valid, jnp.take(sorted_token, sidx), 0)
    gs = jnp.where(valid, jnp.take(sorted_gate, sidx), 0.0)
    return src, gs.reshape(-1, 1), te, tv, pos


def _moe_layer(x, g, router, w_gu, w_down, layer):
    n = x.shape[0]
    h, idx, gates = _router(x, g, router)
    src, gs, te, tv, pos = _route(idx, gates, n)
    xs = jnp.take(h, src, axis=0)
    y = _moe_experts(xs, gs, te, tv, w_gu, w_down, layer)
    return x + (jnp.take(y, pos[:n], axis=0) + jnp.take(y, pos[n:], axis=0))


def kernel(x_prompt, x_sample, state_conv, cache_kv_w128, cache_kv_w512, cache_kv_w2048, norm_mix_g, norm_ffn_g, a_w_in, a_ln_g, a_ln_b, a_w_s, a_b_s, a_w_out, b_w_in, b_conv_w, b_w_out, c_w_qkv, c_q_norm_g, c_k_norm_g, c_w_out, f_w_gu, f_w_down, m_router, m_w_gu, m_w_down):
    batch, seq, _ = x_prompt.shape
    nb = x_sample.shape[0]
    n_prompt = batch * seq
    caches = (cache_kv_w128, cache_kv_w512, cache_kv_w2048)
    x = jnp.concatenate([x_prompt.reshape(n_prompt, D_MODEL),
                         x_sample.transpose(1, 0, 2).reshape(DEC_SEQ * nb, D_MODEL)], axis=0)
    depth = norm_mix_g.shape[0]
    sgu_v, conv_p, conv_s = [], [], []
    kv_p = [[] for _ in range(N_ATTN_GROUPS)]
    kv_s = [[] for _ in range(N_ATTN_GROUPS)]
    for i in range(depth):
        kind, j = i % 3, i // 3
        if kind == 0:
            x, v_rows = _sgu_layer(x, n_prompt, norm_mix_g[i], a_w_in[j], a_ln_g[j], a_ln_b[j],
                                   a_w_s[j], a_b_s[j], a_w_out[j])
            sgu_v.append(v_rows.reshape(DEC_SEQ, nb, D_GATE).transpose(1, 0, 2))
        elif kind == 1:
            x, tail_p, tail_s = _conv_layer(x, n_prompt, batch, state_conv[j], norm_mix_g[i],
                                            b_w_in[j], b_conv_w[j], b_w_out[j])
            conv_p.append(tail_p)
            conv_s.append(tail_s)
        else:
            x, kvp, kvs = _attn_layer(x, n_prompt, batch, [c[j] for c in caches], norm_mix_g[i],
                                      c_w_qkv[j], c_q_norm_g[j], c_k_norm_g[j], c_w_out[j])
            for g in range(N_ATTN_GROUPS):
                kv_p[g].append(kvp[g])
                kv_s[g].append(kvs[g])
        e = i // 2
        if i % 2 == 0:
            x = _dense_ffn(x, norm_ffn_g[i], f_w_gu[e], f_w_down[e])
        else:
            x = _moe_layer(x, norm_ffn_g[i], m_router[e], m_w_gu, m_w_down, e)
    y_prompt = x[:n_prompt].reshape(batch, seq, D_MODEL)
    y_sample = x[n_prompt:].reshape(DEC_SEQ, nb, D_MODEL).transpose(1, 0, 2)
    stack = lambda xs: jnp.stack(xs, axis=0)
    return (y_prompt, y_sample, stack(conv_p), stack(conv_s),
            stack(kv_p[0]), stack(kv_s[0]), stack(kv_p[1]), stack(kv_s[1]),
            stack(kv_p[2]), stack(kv_s[2]), stack(sgu_v))
```

```python
import functools

import numpy as np
import jax
import jax.numpy as jnp
from jax import lax
from jax.experimental import pallas as pl
from jax.experimental.pallas import tpu as pltpu

F32 = jnp.float32
BF16 = jnp.bfloat16

D_MODEL = 1024
EPS = 1e-6
CHUNK = 128
D_GATE = 2 * D_MODEL
SGU_GROUPS = 8
SGU_GDIM = D_GATE // SGU_GROUPS
D_CONV = D_MODEL
HEAD_DIM = 64
HEADS_PER_GROUP = 4
GROUP_DIM = HEADS_PER_GROUP * HEAD_DIM
WINDOWS = (128, 512, 2048)
DILATIONS = (1, 4, 16)
N_ATTN_GROUPS = 3
D_ATTN = N_ATTN_GROUPS * GROUP_DIM
ROPE_DIM = HEAD_DIM // 4
ROPE_THETA = 500000.0
ATTN_BLOCK = 128
D_FF = 2816
N_EXPERTS = 8
D_FF_EXPERT = 3584
DEC_SEQ = 4
PAST_LEN = 2048

SUBLANES = 8
LANES = 128
VMEM_LIMIT = 56 * 1024 * 1024

ROW_TILE = 512
FF_CHUNK = D_FF // 2
MOE_TILE = 1024
MOE_SUB = 256
MOE_SUBS = MOE_TILE // MOE_SUB
MOE_FF_CHUNK = 512
MOE_FF_STEPS = D_FF_EXPERT // MOE_FF_CHUNK


def _dot(a, b):
    return jnp.dot(a, b, preferred_element_type=F32)


def _dot_nt(a, b):
    return lax.dot_general(a, b, (((1,), (1,)), ((), ())), preferred_element_type=F32)


def _rms(x, g):
    return x * lax.rsqrt(jnp.mean(x * x, axis=-1, keepdims=True) + EPS) * g


def _gelu_tanh(x):
    c = np.float32(np.sqrt(2 / np.pi))
    return x * (0.5 * (1.0 + jnp.tanh(c * (x + np.float32(0.044715) * (x * x * x)))))


def _silu(x):
    return x * (1.0 / (1.0 + jnp.exp(-x)))


def _params(*sem):
    return pltpu.CompilerParams(dimension_semantics=sem, vmem_limit_bytes=VMEM_LIMIT)


def _resident(shape):
    zeros = (0,) * len(shape)
    return pl.BlockSpec(shape, lambda *_: zeros, pipeline_mode=pl.Buffered(1))


def _row_spec(width, first_tile=0):
    return pl.BlockSpec((ROW_TILE, width), lambda i: (i + first_tile, 0))


def _sgu_front(x, gmix, winv_ref, lng, lnb):
    h = _rms(x, gmix).astype(BF16)
    v = _gelu_tanh(_dot(h, winv_ref[...]))
    mu = jnp.mean(v, axis=-1, keepdims=True)
    vc = v - mu
    var = jnp.mean(vc * vc, axis=-1, keepdims=True)
    return h, vc * lax.rsqrt(var + EPS) * lng + lnb


def _sgu_back(x, h, winu_ref, s_ref, wout_ref, o_ref):
    u = _gelu_tanh(_dot(h, winu_ref[...]))
    o_ref[...] = x + _dot((u * s_ref[...]).astype(BF16), wout_ref[...])


def _sgu_prompt_body(x_ref, gmix_ref, winu_ref, winv_ref, lng_ref, lnb_ref, ws_ref, bs_ref,
                     wout_ref, o_ref, s_ref):
    x = x_ref[...]
    h, vn = _sgu_front(x, gmix_ref[...], winv_ref, lng_ref[...], lnb_ref[...])
    vb = vn.astype(BF16)
    row = lax.broadcasted_iota(jnp.int32, (CHUNK, CHUNK), 0)
    col = lax.broadcasted_iota(jnp.int32, (CHUNK, CHUNK), 1)
    for g in range(SGU_GROUPS):
        wm = jnp.where(row >= col, ws_ref[g], 0.0).astype(BF16)
        bias = jnp.concatenate([bs_ref[g]] * (SGU_GDIM // LANES), axis=1)
        cols = slice(g * SGU_GDIM, (g + 1) * SGU_GDIM)
        for c in range(ROW_TILE // CHUNK):
            rows = slice(c * CHUNK, (c + 1) * CHUNK)
            s_ref[rows, cols] = _dot(wm, vb[rows, cols]) + bias
    _sgu_back(x, h, winu_ref, s_ref, wout_ref, o_ref)


def _sgu_sample_body(w4_ref, b4_ref, x_ref, gmix_ref, winu_ref, winv_ref, lng_ref, lnb_ref,
                     wout_ref, o_ref, v_ref, s_ref):
    x = x_ref[...]
    nb = x.shape[0] // DEC_SEQ
    h, vn = _sgu_front(x, gmix_ref[...], winv_ref, lng_ref[...], lnb_ref[...])
    v_ref[...] = vn
    for g in range(SGU_GROUPS):
        cols = slice(g * SGU_GDIM, (g + 1) * SGU_GDIM)
        for t in range(DEC_SEQ):
            acc = w4_ref[g, t * DEC_SEQ] * vn[0:nb, cols]
            for j in range(1, t + 1):
                acc = acc + w4_ref[g, t * DEC_SEQ + j] * vn[j * nb:(j + 1) * nb, cols]
            s_ref[t * nb:(t + 1) * nb, cols] = acc + b4_ref[g, t]
    _sgu_back(x, h, winu_ref, s_ref, wout_ref, o_ref)


def _sgu_layer(x, n_prompt, gmix, w_in, ln_g, ln_b, w_s, b_s, w_out):
    n = x.shape[0]
    n_sample = n - n_prompt
    assert n_sample == ROW_TILE and n_prompt % ROW_TILE == 0
    winu = w_in[:, :D_GATE].astype(BF16)
    winv = w_in[:, D_GATE:].astype(BF16)
    wout = w_out.astype(BF16)
    gmix = gmix.reshape(1, D_MODEL)
    lng = ln_g.reshape(1, D_GATE)
    lnb = ln_b.reshape(1, D_GATE)
    bs = jnp.broadcast_to(b_s[:, :, None], (SGU_GROUPS, CHUNK, LANES))
    weights = [_resident((1, D_MODEL)), _resident((D_MODEL, D_GATE)), _resident((D_MODEL, D_GATE)),
               _resident((1, D_GATE)), _resident((1, D_GATE))]
    x = pl.pallas_call(
        _sgu_prompt_body,
        grid=(n_prompt // ROW_TILE,),
        in_specs=[_row_spec(D_MODEL)] + weights + [
            _resident((SGU_GROUPS, CHUNK, CHUNK)), _resident((SGU_GROUPS, CHUNK, LANES)),
            _resident((D_GATE, D_MODEL))],
        out_specs=_row_spec(D_MODEL),
        out_shape=jax.ShapeDtypeStruct((n, D_MODEL), F32),
        scratch_shapes=[pltpu.VMEM((ROW_TILE, D_GATE), F32)],
        input_output_aliases={0: 0},
        compiler_params=_params("parallel"),
        name="sgu_prompt",
    )(x, gmix, winu, winv, lng, lnb, w_s, bs, wout)
    w4 = w_s[:, :DEC_SEQ, :DEC_SEQ].reshape(SGU_GROUPS, DEC_SEQ * DEC_SEQ)
    b4 = b_s[:, :DEC_SEQ]
    smem = pl.BlockSpec(memory_space=pltpu.SMEM)
    first = n_prompt // ROW_TILE
    x, v_rows = pl.pallas_call(
        _sgu_sample_body,
        grid=(1,),
        in_specs=[smem, smem, _row_spec(D_MODEL, first)] + weights + [_resident((D_GATE, D_MODEL))],
        out_specs=[_row_spec(D_MODEL, first), pl.BlockSpec((ROW_TILE, D_GATE), lambda i: (0, 0))],
        out_shape=[jax.ShapeDtypeStruct((n, D_MODEL), F32),
                   jax.ShapeDtypeStruct((n_sample, D_GATE), F32)],
        scratch_shapes=[pltpu.VMEM((ROW_TILE, D_GATE), F32)],
        input_output_aliases={2: 0},
        compiler_params=_params("arbitrary"),
        name="sgu_sample",
    )(w4, b4, x, gmix, winu, winv, lng, lnb, wout)
    return x, v_rows


def _conv_front(x, gmix, win_ref):
    h = _rms(x, gmix).astype(BF16)
    gate_b = _dot(h, win_ref[:, 0:D_CONV])
    gate_c = _dot(h, win_ref[:, D_CONV:2 * D_CONV])
    xi = _dot(h, win_ref[:, 2 * D_CONV:3 * D_CONV])
    return gate_b, gate_c * xi


def _conv_prompt_body(x_ref, gmix_ref, win_ref, cw_ref, wout_ref, o_ref, tail_ref, zbuf, *,
                      tiles_per_seq):
    @pl.when(pl.program_id(0) % tiles_per_seq == 0)
    def _():
        zbuf[0:SUBLANES, :] = jnp.zeros((SUBLANES, D_CONV), F32)

    x = x_ref[...]
    gate_b, z = _conv_front(x, gmix_ref[...], win_ref)
    zbuf[SUBLANES:SUBLANES + ROW_TILE, :] = z
    conv = (cw_ref[2:3, :] * z
            + cw_ref[1:2, :] * zbuf[SUBLANES - 1:SUBLANES - 1 + ROW_TILE, :]
            + cw_ref[0:1, :] * zbuf[SUBLANES - 2:SUBLANES - 2 + ROW_TILE, :])
    o_ref[...] = x + _dot((gate_b * conv).astype(BF16), wout_ref[...])
    tail = zbuf[ROW_TILE + SUBLANES - 2:ROW_TILE + SUBLANES, :]
    tail_ref[0] = tail
    zbuf[SUBLANES - 2:SUBLANES, :] = tail


def _conv_sample_body(x_ref, gmix_ref, win_ref, cw_ref, st_ref, wout_ref, o_ref, tail_ref):
    x = x_ref[...]
    nb = x.shape[0] // DEC_SEQ
    gate_b, z = _conv_front(x, gmix_ref[...], win_ref)
    z1 = jnp.concatenate([st_ref[1], z[:(DEC_SEQ - 1) * nb]], axis=0)
    z2 = jnp.concatenate([st_ref[0], st_ref[1], z[:(DEC_SEQ - 2) * nb]], axis=0)
    conv = cw_ref[2:3, :] * z + cw_ref[1:2, :] * z1 + cw_ref[0:1, :] * z2
    o_ref[...] = x + _dot((gate_b * conv).astype(BF16), wout_ref[...])
    tail_ref[...] = z[(DEC_SEQ - 2) * nb:]


def _conv_layer(x, n_prompt, batch, state, gmix, w_in, conv_w, w_out):
    n = x.shape[0]
    n_sample = n - n_prompt
    nb = n_sample // DEC_SEQ
    assert n_sample == ROW_TILE
    tiles_per_seq = n_prompt // batch // ROW_TILE
    win = w_in.astype(BF16)
    wout = w_out.astype(BF16)
    gmix = gmix.reshape(1, D_MODEL)
    weights = [_resident((1, D_MODEL)), _resident((D_MODEL, 3 * D_CONV)), _resident((3, D_CONV))]
    x, tail_p = pl.pallas_call(
        functools.partial(_conv_prompt_body, tiles_per_seq=tiles_per_seq),
        grid=(n_prompt // ROW_TILE,),
        in_specs=[_row_spec(D_MODEL)] + weights + [_resident((D_CONV, D_MODEL))],
        out_specs=[_row_spec(D_MODEL),
                   pl.BlockSpec((1, 2, D_CONV), lambda i: (i // tiles_per_seq, 0, 0))],
        out_shape=[jax.ShapeDtypeStruct((n, D_MODEL), F32),
                   jax.ShapeDtypeStruct((batch, 2, D_CONV), F32)],
        scratch_shapes=[pltpu.VMEM((ROW_TILE + SUBLANES, D_CONV), F32)],
        input_output_aliases={0: 0},
        compiler_params=_params("arbitrary"),
        name="conv_prompt",
    )(x, gmix, win, conv_w, wout)
    first = n_prompt // ROW_TILE
    st = state.transpose(1, 0, 2)
    x, tail_s = pl.pallas_call(
        _conv_sample_body,
        grid=(1,),
        in_specs=[_row_spec(D_MODEL, first)] + weights + [
            _resident((2, nb, D_CONV)), _resident((D_CONV, D_MODEL))],
        out_specs=[_row_spec(D_MODEL, first), pl.BlockSpec((2 * nb, D_CONV), lambda i: (0, 0))],
        out_shape=[jax.ShapeDtypeStruct((n, D_MODEL), F32),
                   jax.ShapeDtypeStruct((2 * nb, D_CONV), F32)],
        input_output_aliases={0: 0},
        compiler_params=_params("arbitrary"),
        name="conv_sample",
    )(x, gmix, win, conv_w, st, wout)
    return x, tail_p, tail_s.reshape(2, nb, D_CONV).transpose(1, 0, 2)


def _rope_tables(seq, n_sample_rows):
    nb = n_sample_rows // DEC_SEQ
    pos = jnp.concatenate([jnp.arange(seq, dtype=jnp.int32),
                           PAST_LEN + jnp.arange(n_sample_rows, dtype=jnp.int32) // nb])
    inv_freq = jnp.power(ROPE_THETA, -jnp.arange(0, ROPE_DIM, 2, dtype=F32) / ROPE_DIM)
    d = np.arange(LANES) % HEAD_DIM
    half = ROPE_DIM // 2
    ang = pos.astype(F32)[:, None] * inv_freq[d % half][None, :]
    cos, sin = jnp.cos(ang), jnp.sin(ang)
    c = jnp.where(d[None, :] < ROPE_DIM, cos, 1.0)
    s1 = jnp.where(d[None, :] < half, -sin, 0.0)
    s2 = jnp.where((d[None, :] >= half) & (d[None, :] < ROPE_DIM), sin, 0.0)
    return c, s1, s2


def _qkv_body(x_ref, gmix_ref, w_ref, gq_ref, gk_ref, c_ref, s1_ref, s2_ref, bm_ref,
              q_ref, k_ref, v_ref):
    h = _rms(x_ref[...], gmix_ref[...]).astype(BF16)
    c, s1, s2 = c_ref[...], s1_ref[...], s2_ref[...]
    bm = bm_ref[...]

    def head_norm_rope(t, gain, out_ref, scale):
        for a in range(D_ATTN // GROUP_DIM):
            ts = t[:, a * GROUP_DIM:(a + 1) * GROUP_DIM]
            sq = ts * ts
            hi = sq.astype(BF16)
            lo = (sq - hi.astype(F32)).astype(BF16)
            ms = _dot(hi, bm) + _dot(lo, bm)
            tn = ts * lax.rsqrt(ms + EPS) * gain[:, a * GROUP_DIM:(a + 1) * GROUP_DIM]
            for b in range(GROUP_DIM // LANES):
                th = tn[:, b * LANES:(b + 1) * LANES]
                rot = (th * c + pltpu.roll(th, LANES - ROPE_DIM // 2, 1) * s1
                       + pltpu.roll(th, ROPE_DIM // 2, 1) * s2)
                out_ref[a * (GROUP_DIM // LANES) + b] = rot if scale is None else rot * scale

    head_norm_rope(_dot(h, w_ref[:, 0:D_ATTN]), gq_ref[...], q_ref, np.float32(HEAD_DIM ** -0.5))
    head_norm_rope(_dot(h, w_ref[:, D_ATTN:2 * D_ATTN]), gk_ref[...], k_ref, None)
    v = _dot(h, w_ref[:, 2 * D_ATTN:3 * D_ATTN])
    for s in range(D_ATTN // LANES):
        v_ref[s] = v[:, s * LANES:(s + 1) * LANES]


def _qkv(x, n_prompt, seq, gmix, w_qkv, q_norm_g, k_norm_g):
    n = x.shape[0]
    n_sample = n - n_prompt
    tiles_per_seq = seq // ROW_TILE
    prompt_tiles = n_prompt // ROW_TILE
    c, s1, s2 = _rope_tables(seq, n_sample)
    heads = D_ATTN // HEAD_DIM
    slabs = D_ATTN // LANES
    gq = jnp.tile(q_norm_g, heads).reshape(1, D_ATTN)
    gk = jnp.tile(k_norm_g, heads).reshape(1, D_ATTN)
    lane_head = np.arange(GROUP_DIM) // HEAD_DIM
    bm = jnp.asarray((lane_head[:, None] == lane_head[None, :]) / HEAD_DIM, BF16)
    table = pl.BlockSpec(
        (ROW_TILE, LANES),
        lambda i: (jnp.where(i < prompt_tiles, i % tiles_per_seq, tiles_per_seq), 0))
    return pl.pallas_call(
        _qkv_body,
        grid=(n // ROW_TILE,),
        in_specs=[_row_spec(D_MODEL), _resident((1, D_MODEL)), _resident((D_MODEL, 3 * D_ATTN)),
                  _resident((1, D_ATTN)), _resident((1, D_ATTN)), table, table, table,
                  _resident((GROUP_DIM, GROUP_DIM))],
        out_specs=[pl.BlockSpec((slabs, ROW_TILE, LANES), lambda i: (0, i, 0))] * 3,
        out_shape=[jax.ShapeDtypeStruct((slabs, n, LANES), F32)] * 3,
        compiler_params=_params("parallel"),
        name="qkv",
    )(x, gmix.reshape(1, D_MODEL), w_qkv.astype(BF16), gq, gk, c, s1, s2, bm)


GROUP_SLABS = GROUP_DIM // LANES


def _attn_prompt_body(q_ref, kc_ref, kp_ref, vc_ref, vp_ref, o_ref, l_ref, *, dil, blocks):
    first_key = jnp.where(pl.program_id(1) == 0, ATTN_BLOCK, 0)
    lane = lax.broadcasted_iota(jnp.int32, (1, GROUP_DIM), 1)
    head_masks = [lane // HEAD_DIM == hd for hd in range(HEADS_PER_GROUP)]
    qi = lax.broadcasted_iota(jnp.int32, (ATTN_BLOCK, 2 * ATTN_BLOCK), 0)
    kj = lax.broadcasted_iota(jnp.int32, (ATTN_BLOCK, 2 * ATTN_BLOCK), 1)
    dist = qi + ATTN_BLOCK - kj
    band = (dist >= 0) & (dist <= ATTN_BLOCK)
    band_first = band & (kj >= first_key)

    def stream_rows(r, j):
        if dil == 1:
            return pl.ds(j * ATTN_BLOCK, ATTN_BLOCK)
        return pl.ds(r + j * ATTN_BLOCK * dil, ATTN_BLOCK, stride=dil)

    def load(ref, rows):
        return jnp.concatenate([ref[s, rows, :] for s in range(GROUP_SLABS)], axis=1)

    for r in range(dil):
        for j in range(blocks):
            rows = stream_rows(r, j)
            q = load(q_ref, rows).astype(BF16)
            if j == 0:
                k_prev, v_prev, mask = load(kp_ref, stream_rows(r, 0)), load(vp_ref, stream_rows(r, 0)), band_first
            else:
                prev = stream_rows(r, j - 1)
                k_prev, v_prev, mask = load(kc_ref, prev), load(vc_ref, prev), band
            kk = jnp.concatenate([k_prev, load(kc_ref, rows)], axis=0).astype(BF16)
            vv = jnp.concatenate([v_prev, load(vc_ref, rows)], axis=0).astype(BF16)
            o_acc = jnp.zeros((ATTN_BLOCK, GROUP_DIM), F32)
            l_acc = jnp.zeros((ATTN_BLOCK, GROUP_DIM), F32)
            for hm in head_masks:
                s = _dot_nt(jnp.where(hm, q, jnp.zeros_like(q)), kk)
                s = jnp.where(mask, s, -jnp.inf)
                m = jnp.max(s, axis=-1, keepdims=True)
                p = jnp.exp(s - m)
                l = jnp.sum(p, axis=-1, keepdims=True)
                pv = _dot(p.astype(BF16), vv)
                o_acc = o_acc + jnp.where(hm, pv * (1.0 / l), 0.0)
                l_acc = l_acc + jnp.where(hm, m + jnp.log(l), 0.0)
            for s in range(GROUP_SLABS):
                o_ref[s, rows, :] = o_acc[:, s * LANES:(s + 1) * LANES]
                l_ref[s, rows, :] = l_acc[:, s * LANES:(s + 1) * LANES]


def _attn_prompt(q, k, v, group, batch, seq):
    dil = DILATIONS[group]
    assert WINDOWS[group] // dil == ATTN_BLOCK
    span = ATTN_BLOCK * dil
    tile = max(ROW_TILE, span)
    tiles = seq // tile
    cur = pl.BlockSpec((GROUP_SLABS, tile, LANES), lambda b, t: (group, b * tiles + t, 0))
    prev = pl.BlockSpec(
        (GROUP_SLABS, span, LANES),
        lambda b, t: (group, jnp.maximum((b * tiles + t) * (tile // span) - 1, 0), 0))
    out = pl.BlockSpec((GROUP_SLABS, tile, LANES), lambda b, t: (0, b * tiles + t, 0))
    shape = jax.ShapeDtypeStruct((GROUP_SLABS, batch * seq, LANES), F32)
    return pl.pallas_call(
        functools.partial(_attn_prompt_body, dil=dil, blocks=tile // span),
        grid=(batch, tiles),
        in_specs=[cur, cur, prev, cur, prev],
        out_specs=[out, out],
        out_shape=[shape, shape],
        compiler_params=_params("parallel", "arbitrary"),
        name=f"attn_prompt_g{group}",
    )(q, k, k, v, v)


def _attn_sample_body(q_ref, knt_ref, vnt_ref, c_ref, o_ref, lse_ref, *, dil, buf, seqs):
    width = buf + LANES
    t = lax.broadcasted_iota(jnp.int32, (SUBLANES, width), 0)
    c = lax.broadcasted_iota(jnp.int32, (SUBLANES, width), 1)
    tq = jnp.where(t < DEC_SEQ, t, t - DEC_SEQ)
    if dil == 1:
        vis = ((c < buf) & (c >= tq)) | ((c >= buf) & (c - buf <= tq))
    else:
        vis = ((c < buf) & (c % dil == tq)) | (c - buf == tq)

    def one(bi):
        for h in range(HEADS_PER_GROUP):
            kall = jnp.concatenate([c_ref[bi, 0, h].astype(BF16), knt_ref[bi, h]], axis=1)
            vall = jnp.concatenate([c_ref[bi, 1, h].astype(BF16), vnt_ref[bi, h]], axis=1)
            s = jnp.where(vis, _dot(q_ref[bi, h], kall), -jnp.inf)
            m = jnp.max(s, axis=1, keepdims=True)
            p = jnp.exp(s - m)
            l = jnp.sum(p, axis=1, keepdims=True)
            o_ref[bi, h] = _dot_nt(vall, (p * (1.0 / l)).astype(BF16))
            lse_ref[bi, h] = jnp.broadcast_to(m + jnp.log(l), (SUBLANES, LANES))

    def pair(i, carry):
        one(2 * i)
        one(2 * i + 1)
        return carry

    if seqs == 2:
        pair(0, 0)
    else:
        lax.fori_loop(0, seqs // 2, pair, 0)


def _attn_sample(q_rows, k_rows, v_rows, cache, group):
    nb, buf = cache.shape[0], cache.shape[1]
    dil = DILATIONS[group]
    assert buf == ATTN_BLOCK * dil and WINDOWS[group] == buf and (dil == 1 or dil >= DEC_SEQ)
    seqs = max(2, 2 * PAST_LEN // buf)
    heads = (HEADS_PER_GROUP, HEAD_DIM)
    per_head = lambda a: a.reshape(DEC_SEQ, nb, *heads)
    qh = jnp.pad(per_head(q_rows).transpose(1, 2, 0, 3),
                 ((0, 0), (0, 0), (0, SUBLANES - DEC_SEQ), (0, 0))).astype(BF16)
    fresh = lambda a: jnp.pad(per_head(a).transpose(1, 2, 3, 0),
                              ((0, 0), (0, 0), (0, 0), (0, LANES - DEC_SEQ))).astype(BF16)
    head_spec = lambda rows, cols: pl.BlockSpec((seqs, HEADS_PER_GROUP, rows, cols), lambda i: (i, 0, 0, 0))
    o_t, lse = pl.pallas_call(
        functools.partial(_attn_sample_body, dil=dil, buf=buf, seqs=seqs),
        grid=(nb // seqs,),
        in_specs=[head_spec(SUBLANES, HEAD_DIM), head_spec(HEAD_DIM, LANES), head_spec(HEAD_DIM, LANES),
                  pl.BlockSpec((seqs, 2, HEADS_PER_GROUP, HEAD_DIM, buf), lambda i: (i, 0, 0, 0, 0))],
        out_specs=[head_spec(HEAD_DIM, SUBLANES), head_spec(SUBLANES, LANES)],
        out_shape=[jax.ShapeDtypeStruct((nb, HEADS_PER_GROUP, HEAD_DIM, SUBLANES), F32),
                   jax.ShapeDtypeStruct((nb, HEADS_PER_GROUP, SUBLANES, LANES), F32)],
        compiler_params=_params("parallel"),
        name=f"attn_sample_g{group}",
    )(qh, fresh(k_rows), fresh(v_rows), cache.transpose(0, 2, 3, 4, 1))
    o = o_t[..., :DEC_SEQ].transpose(3, 0, 1, 2).reshape(DEC_SEQ * nb, GROUP_DIM)
    lse = jnp.repeat(lse[:, :, :DEC_SEQ, 0].transpose(2, 0, 1), HEAD_DIM, axis=-1)
    return o, lse.reshape(DEC_SEQ * nb, GROUP_DIM)


def _to_slabs(rows):
    return rows.reshape(rows.shape[0], GROUP_SLABS, LANES).transpose(1, 0, 2)


def _from_slabs(slabs, group):
    part = slabs[group * GROUP_SLABS:(group + 1) * GROUP_SLABS]
    return part.transpose(1, 0, 2).reshape(part.shape[1], GROUP_DIM)


def _merge_body(x_ref, o0_ref, o1_ref, o2_ref, l0_ref, l1_ref, l2_ref, w_ref, out_ref):
    rows = lambda ref: jnp.concatenate([ref[s] for s in range(GROUP_SLABS)], axis=1)
    l0, l1, l2 = rows(l0_ref), rows(l1_ref), rows(l2_ref)
    m = jnp.maximum(jnp.maximum(l0, l1), l2)
    e0, e1, e2 = jnp.exp(l0 - m), jnp.exp(l1 - m), jnp.exp(l2 - m)
    inv = 1.0 / (e0 + e1 + e2)
    y = x_ref[...]
    for g, (o_ref, e) in enumerate(((o0_ref, e0), (o1_ref, e1), (o2_ref, e2))):
        y = y + _dot((rows(o_ref) * (e * inv)).astype(BF16),
                     w_ref[g * GROUP_DIM:(g + 1) * GROUP_DIM, :])
    out_ref[...] = y


def _merge_out_proj(x, outs, lses, w_out, first_tile, tiles):
    n = x.shape[0]
    part = pl.BlockSpec((GROUP_SLABS, ROW_TILE, LANES), lambda i: (0, i, 0))
    return pl.pallas_call(
        _merge_body,
        grid=(tiles,),
        in_specs=[_row_spec(D_MODEL, first_tile)] + [part] * 6 + [_resident((D_ATTN, D_MODEL))],
        out_specs=_row_spec(D_MODEL, first_tile),
        out_shape=jax.ShapeDtypeStruct((n, D_MODEL), F32),
        input_output_aliases={0: 0},
        compiler_params=_params("parallel"),
        name="attn_merge_out_proj",
    )(x, *outs, *lses, w_out)


def _attn_layer(x, n_prompt, batch, caches, gmix, w_qkv, q_norm_g, k_norm_g, w_out):
    n = x.shape[0]
    seq = n_prompt // batch
    nb = (n - n_prompt) // DEC_SEQ
    q, k, v = _qkv(x, n_prompt, seq, gmix, w_qkv, q_norm_g, k_norm_g)
    qs, ks, vs = q[:, n_prompt:], k[:, n_prompt:], v[:, n_prompt:]
    outs_p, lses_p, outs_s, lses_s, kv_p, kv_s = [], [], [], [], [], []
    for g in range(N_ATTN_GROUPS):
        o, lse = _attn_prompt(q, k, v, g, batch, seq)
        outs_p.append(o)
        lses_p.append(lse)
        k_rows, v_rows = _from_slabs(ks, g), _from_slabs(vs, g)
        o, lse = _attn_sample(_from_slabs(qs, g), k_rows, v_rows, caches[g], g)
        outs_s.append(_to_slabs(o))
        lses_s.append(_to_slabs(lse))
        keep = min(WINDOWS[g], seq)
        heads = (HEADS_PER_GROUP, HEAD_DIM)
        tail = lambda a: _from_slabs(jnp.concatenate(
            [a[g * GROUP_SLABS:(g + 1) * GROUP_SLABS, (b + 1) * seq - keep:(b + 1) * seq]
             for b in range(batch)], axis=1), 0)
        kv_p.append(jnp.stack([tail(k), tail(v)], axis=1).reshape(batch, keep, 2, *heads))
        kv_s.append(jnp.stack([k_rows, v_rows], axis=1).reshape(DEC_SEQ, nb, 2, *heads)
                    .transpose(1, 0, 2, 3, 4))
    wout = w_out.astype(BF16)
    x = _merge_out_proj(x, outs_p, lses_p, wout, 0, n_prompt // ROW_TILE)
    x = _merge_out_proj(x, outs_s, lses_s, wout, n_prompt // ROW_TILE, (n - n_prompt) // ROW_TILE)
    return x, kv_p, kv_s


def _ffn_body(x_ref, g_ref, wg_ref, wu_ref, wd_ref, o_ref):
    x = x_ref[...]
    h = _rms(x, g_ref[...]).astype(BF16)
    y = x
    for c in range(D_FF // FF_CHUNK):
        cols = slice(c * FF_CHUNK, (c + 1) * FF_CHUNK)
        act = _silu(_dot(h, wg_ref[:, cols])) * _dot(h, wu_ref[:, cols])
        y = y + _dot(act.astype(BF16), wd_ref[cols, :])
    o_ref[...] = y


def _dense_ffn(x, g, w_gu, w_down):
    n = x.shape[0]
    wgu = w_gu.astype(BF16)
    half = pl.Buffered(1)
    return pl.pallas_call(
        _ffn_body,
        grid=(n // ROW_TILE,),
        in_specs=[_row_spec(D_MODEL), _resident((1, D_MODEL)),
                  pl.BlockSpec((D_MODEL, D_FF), lambda i: (0, 0), pipeline_mode=half),
                  pl.BlockSpec((D_MODEL, D_FF), lambda i: (0, 1), pipeline_mode=half),
                  _resident((D_FF, D_MODEL))],
        out_specs=_row_spec(D_MODEL),
        out_shape=jax.ShapeDtypeStruct((n, D_MODEL), F32),
        input_output_aliases={0: 0},
        compiler_params=_params("parallel"),
        name="dense_ffn",
    )(x, g.reshape(1, D_MODEL), wgu, wgu, w_down.astype(BF16))


def _router_body(x_ref, g_ref, r_ref, h_ref, idx_ref, gate_ref):
    h = _rms(x_ref[...], g_ref[...])
    h_ref[...] = h
    logits = jnp.dot(h, r_ref[...], precision=lax.Precision.HIGHEST, preferred_element_type=F32)
    lt = logits.T[:N_EXPERTS, :]
    e = lax.broadcasted_iota(jnp.int32, lt.shape, 0)
    m1 = jnp.max(lt, axis=0, keepdims=True)
    e1 = jnp.min(jnp.where(lt == m1, e, N_EXPERTS), axis=0, keepdims=True)
    rest = jnp.where(e == e1, -jnp.inf, lt)
    m2 = jnp.max(rest, axis=0, keepdims=True)
    e2 = jnp.min(jnp.where(rest == m2, e, N_EXPERTS), axis=0, keepdims=True)
    a = jnp.exp(m2 - m1)
    inv = 1.0 / (1.0 + a)
    idx_ref[...] = jnp.concatenate([e1, e2], axis=0)
    gate_ref[...] = jnp.concatenate([inv, a * inv], axis=0)


def _router(x, g, router):
    n = x.shape[0]
    r = jnp.pad(router, ((0, 0), (0, LANES - N_EXPERTS)))
    pair = pl.BlockSpec((2, ROW_TILE), lambda i: (0, i))
    return pl.pallas_call(
        _router_body,
        grid=(n // ROW_TILE,),
        in_specs=[_row_spec(D_MODEL), _resident((1, D_MODEL)), _resident((D_MODEL, LANES))],
        out_specs=[_row_spec(D_MODEL), pair, pair],
        out_shape=[jax.ShapeDtypeStruct((n, D_MODEL), F32),
                   jax.ShapeDtypeStruct((2, n), jnp.int32),
                   jax.ShapeDtypeStruct((2, n), F32)],
        compiler_params=_params("parallel"),
        name="moe_router",
    )(x, g.reshape(1, D_MODEL), r)


def _moe_body(te_ref, nsub_ref, src_ref, h_hbm, wg_ref, wu_ref, wd_ref, gs_ref, y_ref,
              acc_ref, xbuf, xb_ref, wgb, wub, wdb, sem):
    i, f = pl.program_id(0), pl.program_id(1)
    slot = i % 2
    del te_ref

    def sub_rows(j):
        return pl.ds(j * MOE_SUB, MOE_SUB)

    def fetch(tile, slot):
        for j in range(MOE_SUBS):
            @pl.when(j < nsub_ref[tile])
            def _():
                def row(r, carry):
                    tok = src_ref[tile * MOE_TILE + j * MOE_SUB + r]
                    pltpu.make_async_copy(h_hbm.at[pl.ds(tok, 1)],
                                          xbuf.at[slot, pl.ds(j * MOE_SUB + r, 1)],
                                          sem.at[slot, j]).start()
                    return carry
                lax.fori_loop(0, MOE_SUB, row, 0, unroll=8)

    @pl.when((i == 0) & (f == 0))
    def _():
        fetch(0, 0)

    @pl.when(f == 0)
    def _():
        for j in range(MOE_SUBS):
            @pl.when(j < nsub_ref[i])
            def _():
                blk = xbuf.at[slot, sub_rows(j)]
                pltpu.make_async_copy(blk, blk, sem.at[slot, j]).wait()
                xb_ref[sub_rows(j), :] = xbuf[slot, sub_rows(j), :].astype(BF16)
                acc_ref[sub_rows(j), :] = jnp.zeros((MOE_SUB, D_MODEL), F32)

        @pl.when(i + 1 < pl.num_programs(0))
        def _():
            fetch(i + 1, 1 - slot)

    @pl.when(nsub_ref[i] > 0)
    def _():
        wgb[...] = wg_ref[0, 0].astype(BF16)
        wub[...] = wu_ref[0, 0].astype(BF16)
        wdb[...] = wd_ref[0, 0].astype(BF16)
        for j in range(MOE_SUBS):
            @pl.when(j < nsub_ref[i])
            def _():
                xb = xb_ref[sub_rows(j), :]
                act = _silu(_dot(xb, wgb[...])) * _dot(xb, wub[...])
                acc_ref[sub_rows(j), :] += _dot(act.astype(BF16), wdb[...])

    @pl.when(f == MOE_FF_STEPS - 1)
    def _():
        for j in range(MOE_SUBS):
            @pl.when(j < nsub_ref[i])
            def _():
                y_ref[sub_rows(j), :] = acc_ref[sub_rows(j), :] * gs_ref[sub_rows(j), :]

            @pl.when(j >= nsub_ref[i])
            def _():
                y_ref[sub_rows(j), :] = jnp.zeros((MOE_SUB, D_MODEL), F32)


def _moe_experts(h, src, gs, tile_expert, tile_subs, w_gu, w_down, layer):
    rows = src.shape[0]
    tiles = rows // MOE_TILE
    last = MOE_FF_STEPS - 1
    fstep = lambda i, f, ns: jnp.where(ns[i] > 0, f, last)
    grid_spec = pltpu.PrefetchScalarGridSpec(
        num_scalar_prefetch=3,
        grid=(tiles, MOE_FF_STEPS),
        in_specs=[
            pl.BlockSpec(memory_space=pl.ANY),
            pl.BlockSpec((1, 1, D_MODEL, MOE_FF_CHUNK),
                         lambda i, f, te, ns, sr: (layer, te[i], 0, fstep(i, f, ns))),
            pl.BlockSpec((1, 1, D_MODEL, MOE_FF_CHUNK),
                         lambda i, f, te, ns, sr: (layer, te[i], 0, MOE_FF_STEPS + fstep(i, f, ns))),
            pl.BlockSpec((1, 1, MOE_FF_CHUNK, D_MODEL),
                         lambda i, f, te, ns, sr: (layer, te[i], fstep(i, f, ns), 0)),
            pl.BlockSpec((MOE_TILE, 1), lambda i, f, te, ns, sr: (i, 0)),
        ],
        out_specs=pl.BlockSpec((MOE_TILE, D_MODEL), lambda i, f, te, ns, sr: (i, 0)),
        scratch_shapes=[
            pltpu.VMEM((MOE_TILE, D_MODEL), F32),
            pltpu.VMEM((2, MOE_TILE, D_MODEL), F32),
            pltpu.VMEM((MOE_TILE, D_MODEL), BF16),
            pltpu.VMEM((D_MODEL, MOE_FF_CHUNK), BF16),
            pltpu.VMEM((D_MODEL, MOE_FF_CHUNK), BF16),
            pltpu.VMEM((MOE_FF_CHUNK, D_MODEL), BF16),
            pltpu.SemaphoreType.DMA((2, MOE_SUBS)),
        ],
    )
    return pl.pallas_call(
        _moe_body,
        grid_spec=grid_spec,
        out_shape=jax.ShapeDtypeStruct((rows, D_MODEL), F32),
        compiler_params=_params("arbitrary", "arbitrary"),
        name="moe_experts",
    )(tile_expert, tile_subs, src, h, w_gu, w_gu, w_down, gs)


def _route(idx, gates, n):
    assign = 2 * n
    tiles = -(-(assign + N_EXPERTS * (MOE_TILE - 1)) // MOE_TILE)
    experts = jnp.arange(N_EXPERTS, dtype=jnp.int32)
    ea = idx.reshape(assign)
    ga = gates.reshape(assign)
    onehot = (ea[:, None] == experts[None, :]).astype(jnp.int32)
    csum = jnp.cumsum(onehot, axis=0)
    counts = csum[-1]
    etiles = (counts + MOE_TILE - 1) // MOE_TILE
    tile_end = jnp.cumsum(etiles)
    tile_start = tile_end - etiles
    cnt_start = jnp.cumsum(counts) - counts
    pos = jnp.sum(onehot * ((tile_start * MOE_TILE)[None, :] + csum - 1), axis=1)
    token = jnp.arange(assign, dtype=jnp.int32) % n
    _, sorted_token, sorted_gate = lax.sort((ea, token, ga), num_keys=1, is_stable=True)
    ti = jnp.arange(tiles, dtype=jnp.int32)
    active = ti < tile_end[-1]
    te_raw = jnp.sum((ti[:, None] >= tile_end[None, :]).astype(jnp.int32), axis=1)
    last_expert = jnp.max(jnp.where(etiles > 0, experts, 0))
    te = jnp.where(active, jnp.minimum(te_raw, N_EXPERTS - 1), last_expert)
    tile_onehot = (te[:, None] == experts[None, :]).astype(jnp.int32)
    pick = lambda per_expert: jnp.sum(tile_onehot * per_expert[None, :], axis=1)
    tile_in_expert = ti - pick(tile_start)
    tv = jnp.where(active, jnp.clip(pick(counts) - tile_in_expert * MOE_TILE, 0, MOE_TILE), 0)
    first = pick(cnt_start) + tile_in_expert * MOE_TILE
    r = jnp.arange(MOE_TILE, dtype=jnp.int32)
    valid = (r[None, :] < tv[:, None]).reshape(-1)
    sidx = jnp.clip(first[:, None] + r[None, :], 0, assign - 1).reshape(-1)
    src = jnp.where(valid, jnp.take(sorted_token, sidx, mode="clip"), 0)
    gs = jnp.where(valid, jnp.take(sorted_gate, sidx, mode="clip"), 0.0)
    return src, gs.reshape(-1, 1), te, (tv + MOE_SUB - 1) // MOE_SUB, pos


def _moe_layer(x, g, router, w_gu, w_down, layer, split=None):
    n = x.shape[0]
    h, idx, gates = _router(x, g, router)
    src, gs, te, subs, pos = _route(idx, gates, n)
    y = _moe_experts(h, src, gs, te, subs, w_gu, w_down, layer)
    y1 = jnp.take(y, pos[:n], axis=0, mode="clip")
    y2 = jnp.take(y, pos[n:], axis=0, mode="clip")
    if split is None:
        return x + (y1 + y2)
    return x[:split] + (y1[:split] + y2[:split]), x[split:] + (y1[split:] + y2[split:])


def kernel(x_prompt, x_sample, state_conv, cache_kv_w128, cache_kv_w512, cache_kv_w2048, norm_mix_g, norm_ffn_g, a_w_in, a_ln_g, a_ln_b, a_w_s, a_b_s, a_w_out, b_w_in, b_conv_w, b_w_out, c_w_qkv, c_q_norm_g, c_k_norm_g, c_w_out, f_w_gu, f_w_down, m_router, m_w_gu, m_w_down):
    batch, seq, _ = x_prompt.shape
    nb = x_sample.shape[0]
    n_prompt = batch * seq
    caches = (cache_kv_w128, cache_kv_w512, cache_kv_w2048)
    x = jnp.concatenate([x_prompt.reshape(n_prompt, D_MODEL),
                         x_sample.transpose(1, 0, 2).reshape(DEC_SEQ * nb, D_MODEL)], axis=0)
    depth = norm_mix_g.shape[0]
    sgu_v, conv_p, conv_s = [], [], []
    kv_p = [[] for _ in range(N_ATTN_GROUPS)]
    kv_s = [[] for _ in range(N_ATTN_GROUPS)]
    for i in range(depth):
        kind, j = i % 3, i // 3
        if kind == 0:
            x, v_rows = _sgu_layer(x, n_prompt, norm_mix_g[i], a_w_in[j], a_ln_g[j], a_ln_b[j],
                                   a_w_s[j], a_b_s[j], a_w_out[j])
            sgu_v.append(v_rows.reshape(DEC_SEQ, nb, D_GATE).transpose(1, 0, 2))
        elif kind == 1:
            x, tail_p, tail_s = _conv_layer(x, n_prompt, batch, state_conv[j], norm_mix_g[i],
                                            b_w_in[j], b_conv_w[j], b_w_out[j])
            conv_p.append(tail_p)
            conv_s.append(tail_s)
        else:
            x, kvp, kvs = _attn_layer(x, n_prompt, batch, [c[j] for c in caches], norm_mix_g[i],
                                      c_w_qkv[j], c_q_norm_g[j], c_k_norm_g[j], c_w_out[j])
            for g in range(N_ATTN_GROUPS):
                kv_p[g].append(kvp[g])
                kv_s[g].append(kvs[g])
        e = i // 2
        if i % 2 == 0:
            x = _dense_ffn(x, norm_ffn_g[i], f_w_gu[e], f_w_down[e])
        else:
            x = _moe_layer(x, norm_ffn_g[i], m_router[e], m_w_gu, m_w_down, e,
                           split=n_prompt if i == depth - 1 else None)
    xp, xs = x if isinstance(x, tuple) else (x[:n_prompt], x[n_prompt:])
    y_prompt = xp.reshape(batch, seq, D_MODEL)
    y_sample = xs.reshape(DEC_SEQ, nb, D_MODEL).transpose(1, 0, 2)
    stack = lambda xs: jnp.stack(xs, axis=0)
    return (y_prompt, y_sample, stack(conv_p), stack(conv_s),
            stack(kv_p[0]), stack(kv_s[0]), stack(kv_p[1]), stack(kv_s[1]),
            stack(kv_p[2]), stack(kv_s[2]), stack(sgu_v))
```

```python
import functools

import numpy as np
import jax
import jax.numpy as jnp
from jax import lax
from jax.experimental import pallas as pl
from jax.experimental.pallas import tpu as pltpu

F32 = jnp.float32
BF16 = jnp.bfloat16

D_MODEL = 1024
EPS = 1e-6
CHUNK = 128
D_GATE = 2 * D_MODEL
SGU_GROUPS = 8
SGU_GDIM = D_GATE // SGU_GROUPS
D_CONV = D_MODEL
HEAD_DIM = 64
HEADS_PER_GROUP = 4
GROUP_DIM = HEADS_PER_GROUP * HEAD_DIM
WINDOWS = (128, 512, 2048)
DILATIONS = (1, 4, 16)
N_ATTN_GROUPS = 3
D_ATTN = N_ATTN_GROUPS * GROUP_DIM
ROPE_DIM = HEAD_DIM // 4
ROPE_THETA = 500000.0
ATTN_BLOCK = 128
D_FF = 2816
N_EXPERTS = 8
D_FF_EXPERT = 3584
DEC_SEQ = 4
PAST_LEN = 2048

SUBLANES = 8
LANES = 128
VMEM_LIMIT = 56 * 1024 * 1024

ROW_TILE = 512
FF_CHUNK = D_FF // 2
MOE_TILE = 1024
MOE_SUB = 256
MOE_SUBS = MOE_TILE // MOE_SUB
MOE_FF_CHUNK = 512
MOE_FF_STEPS = D_FF_EXPERT // MOE_FF_CHUNK
MOE_FETCH = MOE_TILE // (MOE_FF_STEPS + 1)
assert MOE_FETCH * (MOE_FF_STEPS + 1) == MOE_TILE and MOE_SUB % MOE_FETCH == 0


def _dot(a, b):
    return jnp.dot(a, b, preferred_element_type=F32)


def _dot_nt(a, b):
    return lax.dot_general(a, b, (((1,), (1,)), ((), ())), preferred_element_type=F32)


def _rms(x, g):
    return x * lax.rsqrt(jnp.mean(x * x, axis=-1, keepdims=True) + EPS) * g


def _gelu_tanh(x):
    c = np.float32(np.sqrt(2 / np.pi))
    return x * (0.5 * (1.0 + jnp.tanh(c * (x + np.float32(0.044715) * (x * x * x)))))


def _silu(x):
    return x * (1.0 / (1.0 + jnp.exp(-x)))


def _params(*sem):
    return pltpu.CompilerParams(dimension_semantics=sem, vmem_limit_bytes=VMEM_LIMIT)


def _resident(shape):
    zeros = (0,) * len(shape)
    return pl.BlockSpec(shape, lambda *_: zeros, pipeline_mode=pl.Buffered(1))


def _row_spec(width, first_tile=0):
    return pl.BlockSpec((ROW_TILE, width), lambda i: (i + first_tile, 0))


def _sgu_front(x, gmix, winv_ref, lng, lnb):
    h = _rms(x, gmix).astype(BF16)
    v = _gelu_tanh(_dot(h, winv_ref[...]))
    mu = jnp.mean(v, axis=-1, keepdims=True)
    vc = v - mu
    var = jnp.mean(vc * vc, axis=-1, keepdims=True)
    return h, vc * lax.rsqrt(var + EPS) * lng + lnb


def _sgu_back(x, h, winu_ref, s_ref, wout_ref, o_ref):
    u = _gelu_tanh(_dot(h, winu_ref[...]))
    o_ref[...] = x + _dot((u * s_ref[...]).astype(BF16), wout_ref[...])


def _sgu_prompt_body(x_ref, gmix_ref, winu_ref, winv_ref, lng_ref, lnb_ref, ws_ref, bs_ref,
                     wout_ref, o_ref, s_ref):
    x = x_ref[...]
    h, vn = _sgu_front(x, gmix_ref[...], winv_ref, lng_ref[...], lnb_ref[...])
    vb = vn.astype(BF16)
    row = lax.broadcasted_iota(jnp.int32, (CHUNK, CHUNK), 0)
    col = lax.broadcasted_iota(jnp.int32, (CHUNK, CHUNK), 1)
    for g in range(SGU_GROUPS):
        wm = jnp.where(row >= col, ws_ref[g], 0.0).astype(BF16)
        bias = jnp.concatenate([bs_ref[g]] * (SGU_GDIM // LANES), axis=1)
        cols = slice(g * SGU_GDIM, (g + 1) * SGU_GDIM)
        for c in range(ROW_TILE // CHUNK):
            rows = slice(c * CHUNK, (c + 1) * CHUNK)
            s_ref[rows, cols] = _dot(wm, vb[rows, cols]) + bias
    _sgu_back(x, h, winu_ref, s_ref, wout_ref, o_ref)


def _sgu_sample_body(w4_ref, b4_ref, x_ref, gmix_ref, winu_ref, winv_ref, lng_ref, lnb_ref,
                     wout_ref, o_ref, v_ref, s_ref):
    x = x_ref[...]
    nb = x.shape[0] // DEC_SEQ
    h, vn = _sgu_front(x, gmix_ref[...], winv_ref, lng_ref[...], lnb_ref[...])
    v_ref[...] = vn
    for g in range(SGU_GROUPS):
        cols = slice(g * SGU_GDIM, (g + 1) * SGU_GDIM)
        for t in range(DEC_SEQ):
            acc = w4_ref[g, t * DEC_SEQ] * vn[0:nb, cols]
            for j in range(1, t + 1):
                acc = acc + w4_ref[g, t * DEC_SEQ + j] * vn[j * nb:(j + 1) * nb, cols]
            s_ref[t * nb:(t + 1) * nb, cols] = acc + b4_ref[g, t]
    _sgu_back(x, h, winu_ref, s_ref, wout_ref, o_ref)


def _sgu_layer(x, n_prompt, gmix, w_in, ln_g, ln_b, w_s, b_s, w_out):
    n = x.shape[0]
    n_sample = n - n_prompt
    assert n_sample == ROW_TILE and n_prompt % ROW_TILE == 0
    winu = w_in[:, :D_GATE].astype(BF16)
    winv = w_in[:, D_GATE:].astype(BF16)
    wout = w_out.astype(BF16)
    gmix = gmix.reshape(1, D_MODEL)
    lng = ln_g.reshape(1, D_GATE)
    lnb = ln_b.reshape(1, D_GATE)
    bs = jnp.broadcast_to(b_s[:, :, None], (SGU_GROUPS, CHUNK, LANES))
    weights = [_resident((1, D_MODEL)), _resident((D_MODEL, D_GATE)), _resident((D_MODEL, D_GATE)),
               _resident((1, D_GATE)), _resident((1, D_GATE))]
    x = pl.pallas_call(
        _sgu_prompt_body,
        grid=(n_prompt // ROW_TILE,),
        in_specs=[_row_spec(D_MODEL)] + weights + [
            _resident((SGU_GROUPS, CHUNK, CHUNK)), _resident((SGU_GROUPS, CHUNK, LANES)),
            _resident((D_GATE, D_MODEL))],
        out_specs=_row_spec(D_MODEL),
        out_shape=jax.ShapeDtypeStruct((n, D_MODEL), F32),
        scratch_shapes=[pltpu.VMEM((ROW_TILE, D_GATE), F32)],
        input_output_aliases={0: 0},
        compiler_params=_params("parallel"),
        name="sgu_prompt",
    )(x, gmix, winu, winv, lng, lnb, w_s, bs, wout)
    w4 = w_s[:, :DEC_SEQ, :DEC_SEQ].reshape(SGU_GROUPS, DEC_SEQ * DEC_SEQ)
    b4 = b_s[:, :DEC_SEQ]
    smem = pl.BlockSpec(memory_space=pltpu.SMEM)
    first = n_prompt // ROW_TILE
    x, v_rows = pl.pallas_call(
        _sgu_sample_body,
        grid=(1,),
        in_specs=[smem, smem, _row_spec(D_MODEL, first)] + weights + [_resident((D_GATE, D_MODEL))],
        out_specs=[_row_spec(D_MODEL, first), pl.BlockSpec((ROW_TILE, D_GATE), lambda i: (0, 0))],
        out_shape=[jax.ShapeDtypeStruct((n, D_MODEL), F32),
                   jax.ShapeDtypeStruct((n_sample, D_GATE), F32)],
        scratch_shapes=[pltpu.VMEM((ROW_TILE, D_GATE), F32)],
        input_output_aliases={2: 0},
        compiler_params=_params("arbitrary"),
        name="sgu_sample",
    )(w4, b4, x, gmix, winu, winv, lng, lnb, wout)
    return x, v_rows


def _conv_front(x, gmix, win_ref):
    h = _rms(x, gmix).astype(BF16)
    gate_b = _dot(h, win_ref[:, 0:D_CONV])
    gate_c = _dot(h, win_ref[:, D_CONV:2 * D_CONV])
    xi = _dot(h, win_ref[:, 2 * D_CONV:3 * D_CONV])
    return gate_b, gate_c * xi


def _conv_prompt_body(x_ref, gmix_ref, win_ref, cw_ref, wout_ref, o_ref, tail_ref, zbuf, *,
                      tiles_per_seq):
    @pl.when(pl.program_id(0) % tiles_per_seq == 0)
    def _():
        zbuf[0:SUBLANES, :] = jnp.zeros((SUBLANES, D_CONV), F32)

    x = x_ref[...]
    gate_b, z = _conv_front(x, gmix_ref[...], win_ref)
    zbuf[SUBLANES:SUBLANES + ROW_TILE, :] = z
    conv = (cw_ref[2:3, :] * z
            + cw_ref[1:2, :] * zbuf[SUBLANES - 1:SUBLANES - 1 + ROW_TILE, :]
            + cw_ref[0:1, :] * zbuf[SUBLANES - 2:SUBLANES - 2 + ROW_TILE, :])
    o_ref[...] = x + _dot((gate_b * conv).astype(BF16), wout_ref[...])
    tail = zbuf[ROW_TILE + SUBLANES - 2:ROW_TILE + SUBLANES, :]
    tail_ref[0] = tail
    zbuf[SUBLANES - 2:SUBLANES, :] = tail


def _conv_sample_body(x_ref, gmix_ref, win_ref, cw_ref, st_ref, wout_ref, o_ref, tail_ref):
    x = x_ref[...]
    nb = x.shape[0] // DEC_SEQ
    gate_b, z = _conv_front(x, gmix_ref[...], win_ref)
    z1 = jnp.concatenate([st_ref[1], z[:(DEC_SEQ - 1) * nb]], axis=0)
    z2 = jnp.concatenate([st_ref[0], st_ref[1], z[:(DEC_SEQ - 2) * nb]], axis=0)
    conv = cw_ref[2:3, :] * z + cw_ref[1:2, :] * z1 + cw_ref[0:1, :] * z2
    o_ref[...] = x + _dot((gate_b * conv).astype(BF16), wout_ref[...])
    tail_ref[...] = z[(DEC_SEQ - 2) * nb:]


def _conv_layer(x, n_prompt, batch, state, gmix, w_in, conv_w, w_out):
    n = x.shape[0]
    n_sample = n - n_prompt
    nb = n_sample // DEC_SEQ
    assert n_sample == ROW_TILE
    tiles_per_seq = n_prompt // batch // ROW_TILE
    win = w_in.astype(BF16)
    wout = w_out.astype(BF16)
    gmix = gmix.reshape(1, D_MODEL)
    weights = [_resident((1, D_MODEL)), _resident((D_MODEL, 3 * D_CONV)), _resident((3, D_CONV))]
    x, tail_p = pl.pallas_call(
        functools.partial(_conv_prompt_body, tiles_per_seq=tiles_per_seq),
        grid=(n_prompt // ROW_TILE,),
        in_specs=[_row_spec(D_MODEL)] + weights + [_resident((D_CONV, D_MODEL))],
        out_specs=[_row_spec(D_MODEL),
                   pl.BlockSpec((1, 2, D_CONV), lambda i: (i // tiles_per_seq, 0, 0))],
        out_shape=[jax.ShapeDtypeStruct((n, D_MODEL), F32),
                   jax.ShapeDtypeStruct((batch, 2, D_CONV), F32)],
        scratch_shapes=[pltpu.VMEM((ROW_TILE + SUBLANES, D_CONV), F32)],
        input_output_aliases={0: 0},
        compiler_params=_params("arbitrary"),
        name="conv_prompt",
    )(x, gmix, win, conv_w, wout)
    first = n_prompt // ROW_TILE
    st = state.transpose(1, 0, 2)
    x, tail_s = pl.pallas_call(
        _conv_sample_body,
        grid=(1,),
        in_specs=[_row_spec(D_MODEL, first)] + weights + [
            _resident((2, nb, D_CONV)), _resident((D_CONV, D_MODEL))],
        out_specs=[_row_spec(D_MODEL, first), pl.BlockSpec((2 * nb, D_CONV), lambda i: (0, 0))],
        out_shape=[jax.ShapeDtypeStruct((n, D_MODEL), F32),
                   jax.ShapeDtypeStruct((2 * nb, D_CONV), F32)],
        input_output_aliases={0: 0},
        compiler_params=_params("arbitrary"),
        name="conv_sample",
    )(x, gmix, win, conv_w, st, wout)
    return x, tail_p, tail_s.reshape(2, nb, D_CONV).transpose(1, 0, 2)


def _rope_tables(seq, n_sample_rows):
    nb = n_sample_rows // DEC_SEQ
    pos = jnp.concatenate([jnp.arange(seq, dtype=jnp.int32),
                           PAST_LEN + jnp.arange(n_sample_rows, dtype=jnp.int32) // nb])
    inv_freq = jnp.power(ROPE_THETA, -jnp.arange(0, ROPE_DIM, 2, dtype=F32) / ROPE_DIM)
    d = np.arange(LANES) % HEAD_DIM
    half = ROPE_DIM // 2
    ang = pos.astype(F32)[:, None] * inv_freq[d % half][None, :]
    cos, sin = jnp.cos(ang), jnp.sin(ang)
    c = jnp.where(d[None, :] < ROPE_DIM, cos, 1.0)
    s1 = jnp.where(d[None, :] < half, -sin, 0.0)
    s2 = jnp.where((d[None, :] >= half) & (d[None, :] < ROPE_DIM), sin, 0.0)
    return c, s1, s2


def _qkv_body(x_ref, gmix_ref, w_ref, gq_ref, gk_ref, c_ref, s1_ref, s2_ref, bm_ref,
              q_ref, k_ref, v_ref):
    h = _rms(x_ref[...], gmix_ref[...]).astype(BF16)
    c, s1, s2 = c_ref[...], s1_ref[...], s2_ref[...]
    bm = bm_ref[...]

    def head_norm_rope(t, gain, out_ref, scale):
        for a in range(D_ATTN // GROUP_DIM):
            ts = t[:, a * GROUP_DIM:(a + 1) * GROUP_DIM]
            sq = ts * ts
            hi = sq.astype(BF16)
            lo = (sq - hi.astype(F32)).astype(BF16)
            ms = _dot(hi, bm) + _dot(lo, bm)
            tn = ts * lax.rsqrt(ms + EPS) * gain[:, a * GROUP_DIM:(a + 1) * GROUP_DIM]
            for b in range(GROUP_DIM // LANES):
                th = tn[:, b * LANES:(b + 1) * LANES]
                rot = (th * c + pltpu.roll(th, LANES - ROPE_DIM // 2, 1) * s1
                       + pltpu.roll(th, ROPE_DIM // 2, 1) * s2)
                out_ref[a * (GROUP_DIM // LANES) + b] = rot if scale is None else rot * scale

    head_norm_rope(_dot(h, w_ref[:, 0:D_ATTN]), gq_ref[...], q_ref, np.float32(HEAD_DIM ** -0.5))
    head_norm_rope(_dot(h, w_ref[:, D_ATTN:2 * D_ATTN]), gk_ref[...], k_ref, None)
    v = _dot(h, w_ref[:, 2 * D_ATTN:3 * D_ATTN])
    for s in range(D_ATTN // LANES):
        v_ref[s] = v[:, s * LANES:(s + 1) * LANES]


def _qkv(x, n_prompt, seq, gmix, w_qkv, q_norm_g, k_norm_g):
    n = x.shape[0]
    n_sample = n - n_prompt
    tiles_per_seq = seq // ROW_TILE
    prompt_tiles = n_prompt // ROW_TILE
    c, s1, s2 = _rope_tables(seq, n_sample)
    heads = D_ATTN // HEAD_DIM
    slabs = D_ATTN // LANES
    gq = jnp.tile(q_norm_g, heads).reshape(1, D_ATTN)
    gk = jnp.tile(k_norm_g, heads).reshape(1, D_ATTN)
    lane_head = np.arange(GROUP_DIM) // HEAD_DIM
    bm = jnp.asarray((lane_head[:, None] == lane_head[None, :]) / HEAD_DIM, BF16)
    table = pl.BlockSpec(
        (ROW_TILE, LANES),
        lambda i: (jnp.where(i < prompt_tiles, i % tiles_per_seq, tiles_per_seq), 0))
    return pl.pallas_call(
        _qkv_body,
        grid=(n // ROW_TILE,),
        in_specs=[_row_spec(D_MODEL), _resident((1, D_MODEL)), _resident((D_MODEL, 3 * D_ATTN)),
                  _resident((1, D_ATTN)), _resident((1, D_ATTN)), table, table, table,
                  _resident((GROUP_DIM, GROUP_DIM))],
        out_specs=[pl.BlockSpec((slabs, ROW_TILE, LANES), lambda i: (0, i, 0))] * 3,
        out_shape=[jax.ShapeDtypeStruct((slabs, n, LANES), F32)] * 3,
        compiler_params=_params("parallel"),
        name="qkv",
    )(x, gmix.reshape(1, D_MODEL), w_qkv.astype(BF16), gq, gk, c, s1, s2, bm)


GROUP_SLABS = GROUP_DIM // LANES


def _attn_prompt_body(q_ref, kc_ref, kp_ref, vc_ref, vp_ref, o_ref, l_ref, *, dil, blocks):
    first_key = jnp.where(pl.program_id(1) == 0, ATTN_BLOCK, 0)
    lane = lax.broadcasted_iota(jnp.int32, (1, GROUP_DIM), 1)
    head_masks = [lane // HEAD_DIM == hd for hd in range(HEADS_PER_GROUP)]
    qi = lax.broadcasted_iota(jnp.int32, (ATTN_BLOCK, 2 * ATTN_BLOCK), 0)
    kj = lax.broadcasted_iota(jnp.int32, (ATTN_BLOCK, 2 * ATTN_BLOCK), 1)
    dist = qi + ATTN_BLOCK - kj
    band = (dist >= 0) & (dist <= ATTN_BLOCK)
    band_first = band & (kj >= first_key)

    def stream_rows(r, j):
        if dil == 1:
            return pl.ds(j * ATTN_BLOCK, ATTN_BLOCK)
        return pl.ds(r + j * ATTN_BLOCK * dil, ATTN_BLOCK, stride=dil)

    def load(ref, rows):
        return jnp.concatenate([ref[s, rows, :] for s in range(GROUP_SLABS)], axis=1)

    for r in range(dil):
        for j in range(blocks):
            rows = stream_rows(r, j)
            q = load(q_ref, rows).astype(BF16)
            if j == 0:
                k_prev, v_prev, mask = load(kp_ref, stream_rows(r, 0)), load(vp_ref, stream_rows(r, 0)), band_first
            else:
                prev = stream_rows(r, j - 1)
                k_prev, v_prev, mask = load(kc_ref, prev), load(vc_ref, prev), band
            kk = jnp.concatenate([k_prev, load(kc_ref, rows)], axis=0).astype(BF16)
            vv = jnp.concatenate([v_prev, load(vc_ref, rows)], axis=0).astype(BF16)
            o_acc = jnp.zeros((ATTN_BLOCK, GROUP_DIM), F32)
            l_acc = jnp.zeros((ATTN_BLOCK, GROUP_DIM), F32)
            for hm in head_masks:
                s = _dot_nt(jnp.where(hm, q, jnp.zeros_like(q)), kk)
                s = jnp.where(mask, s, -jnp.inf)
                m = jnp.max(s, axis=-1, keepdims=True)
                p = jnp.exp(s - m)
                l = jnp.sum(p, axis=-1, keepdims=True)
                pv = _dot(p.astype(BF16), vv)
                o_acc = o_acc + jnp.where(hm, pv * (1.0 / l), 0.0)
                l_acc = l_acc + jnp.where(hm, m + jnp.log(l), 0.0)
            for s in range(GROUP_SLABS):
                o_ref[s, rows, :] = o_acc[:, s * LANES:(s + 1) * LANES]
                l_ref[s, rows, :] = l_acc[:, s * LANES:(s + 1) * LANES]


def _attn_prompt(q, k, v, group, batch, seq):
    dil = DILATIONS[group]
    assert WINDOWS[group] // dil == ATTN_BLOCK
    span = ATTN_BLOCK * dil
    tile = max(ROW_TILE, span)
    tiles = seq // tile
    cur = pl.BlockSpec((GROUP_SLABS, tile, LANES), lambda b, t: (group, b * tiles + t, 0))
    prev = pl.BlockSpec(
        (GROUP_SLABS, span, LANES),
        lambda b, t: (group, jnp.maximum((b * tiles + t) * (tile // span) - 1, 0), 0))
    out = pl.BlockSpec((GROUP_SLABS, tile, LANES), lambda b, t: (0, b * tiles + t, 0))
    shape = jax.ShapeDtypeStruct((GROUP_SLABS, batch * seq, LANES), F32)
    return pl.pallas_call(
        functools.partial(_attn_prompt_body, dil=dil, blocks=tile // span),
        grid=(batch, tiles),
        in_specs=[cur, cur, prev, cur, prev],
        out_specs=[out, out],
        out_shape=[shape, shape],
        compiler_params=_params("parallel", "arbitrary"),
        name=f"attn_prompt_g{group}",
    )(q, k, k, v, v)


def _attn_sample_body(q_ref, knt_ref, vnt_ref, c_ref, o_ref, lse_ref, *, dil, buf, seqs):
    width = buf + LANES
    t = lax.broadcasted_iota(jnp.int32, (SUBLANES, width), 0)
    c = lax.broadcasted_iota(jnp.int32, (SUBLANES, width), 1)
    tq = jnp.where(t < DEC_SEQ, t, t - DEC_SEQ)
    if dil == 1:
        vis = ((c < buf) & (c >= tq)) | ((c >= buf) & (c - buf <= tq))
    else:
        vis = ((c < buf) & (c % dil == tq)) | (c - buf == tq)

    def one(bi):
        for h in range(HEADS_PER_GROUP):
            kall = jnp.concatenate([c_ref[bi, 0, h].astype(BF16), knt_ref[bi, h]], axis=1)
            vall = jnp.concatenate([c_ref[bi, 1, h].astype(BF16), vnt_ref[bi, h]], axis=1)
            s = jnp.where(vis, _dot(q_ref[bi, h], kall), -jnp.inf)
            m = jnp.max(s, axis=1, keepdims=True)
            p = jnp.exp(s - m)
            l = jnp.sum(p, axis=1, keepdims=True)
            o_ref[bi, h] = _dot_nt(vall, (p * (1.0 / l)).astype(BF16))
            lse_ref[bi, h] = jnp.broadcast_to(m + jnp.log(l), (SUBLANES, LANES))

    def pair(i, carry):
        one(2 * i)
        one(2 * i + 1)
        return carry

    if seqs == 2:
        pair(0, 0)
    else:
        lax.fori_loop(0, seqs // 2, pair, 0)


def _attn_sample(q_rows, k_rows, v_rows, cache, group):
    nb, buf = cache.shape[0], cache.shape[1]
    dil = DILATIONS[group]
    assert buf == ATTN_BLOCK * dil and WINDOWS[group] == buf and (dil == 1 or dil >= DEC_SEQ)
    seqs = max(2, 2 * PAST_LEN // buf)
    heads = (HEADS_PER_GROUP, HEAD_DIM)
    per_head = lambda a: a.reshape(DEC_SEQ, nb, *heads)
    qh = jnp.pad(per_head(q_rows).transpose(1, 2, 0, 3),
                 ((0, 0), (0, 0), (0, SUBLANES - DEC_SEQ), (0, 0))).astype(BF16)
    fresh = lambda a: jnp.pad(per_head(a).transpose(1, 2, 3, 0),
                              ((0, 0), (0, 0), (0, 0), (0, LANES - DEC_SEQ))).astype(BF16)
    head_spec = lambda rows, cols: pl.BlockSpec((seqs, HEADS_PER_GROUP, rows, cols), lambda i: (i, 0, 0, 0))
    o_t, lse = pl.pallas_call(
        functools.partial(_attn_sample_body, dil=dil, buf=buf, seqs=seqs),
        grid=(nb // seqs,),
        in_specs=[head_spec(SUBLANES, HEAD_DIM), head_spec(HEAD_DIM, LANES), head_spec(HEAD_DIM, LANES),
                  pl.BlockSpec((seqs, 2, HEADS_PER_GROUP, HEAD_DIM, buf), lambda i: (i, 0, 0, 0, 0))],
        out_specs=[head_spec(HEAD_DIM, SUBLANES), head_spec(SUBLANES, LANES)],
        out_shape=[jax.ShapeDtypeStruct((nb, HEADS_PER_GROUP, HEAD_DIM, SUBLANES), F32),
                   jax.ShapeDtypeStruct((nb, HEADS_PER_GROUP, SUBLANES, LANES), F32)],
        compiler_params=_params("parallel"),
        name=f"attn_sample_g{group}",
    )(qh, fresh(k_rows), fresh(v_rows), cache.transpose(0, 2, 3, 4, 1))
    o = o_t[..., :DEC_SEQ].transpose(3, 0, 1, 2).reshape(DEC_SEQ * nb, GROUP_DIM)
    lse = jnp.repeat(lse[:, :, :DEC_SEQ, 0].transpose(2, 0, 1), HEAD_DIM, axis=-1)
    return o, lse.reshape(DEC_SEQ * nb, GROUP_DIM)


def _to_slabs(rows):
    return rows.reshape(rows.shape[0], GROUP_SLABS, LANES).transpose(1, 0, 2)


def _from_slabs(slabs, group):
    part = slabs[group * GROUP_SLABS:(group + 1) * GROUP_SLABS]
    return part.transpose(1, 0, 2).reshape(part.shape[1], GROUP_DIM)


def _merge_body(x_ref, o0_ref, o1_ref, o2_ref, l0_ref, l1_ref, l2_ref, w_ref, out_ref):
    rows = lambda ref: jnp.concatenate([ref[s] for s in range(GROUP_SLABS)], axis=1)
    l0, l1, l2 = rows(l0_ref), rows(l1_ref), rows(l2_ref)
    m = jnp.maximum(jnp.maximum(l0, l1), l2)
    e0, e1, e2 = jnp.exp(l0 - m), jnp.exp(l1 - m), jnp.exp(l2 - m)
    inv = 1.0 / (e0 + e1 + e2)
    y = x_ref[...]
    for g, (o_ref, e) in enumerate(((o0_ref, e0), (o1_ref, e1), (o2_ref, e2))):
        y = y + _dot((rows(o_ref) * (e * inv)).astype(BF16),
                     w_ref[g * GROUP_DIM:(g + 1) * GROUP_DIM, :])
    out_ref[...] = y


def _merge_out_proj(x, outs, lses, w_out, first_tile, tiles):
    n = x.shape[0]
    part = pl.BlockSpec((GROUP_SLABS, ROW_TILE, LANES), lambda i: (0, i, 0))
    return pl.pallas_call(
        _merge_body,
        grid=(tiles,),
        in_specs=[_row_spec(D_MODEL, first_tile)] + [part] * 6 + [_resident((D_ATTN, D_MODEL))],
        out_specs=_row_spec(D_MODEL, first_tile),
        out_shape=jax.ShapeDtypeStruct((n, D_MODEL), F32),
        input_output_aliases={0: 0},
        compiler_params=_params("parallel"),
        name="attn_merge_out_proj",
    )(x, *outs, *lses, w_out)


def _attn_layer(x, n_prompt, batch, caches, gmix, w_qkv, q_norm_g, k_norm_g, w_out):
    n = x.shape[0]
    seq = n_prompt // batch
    nb = (n - n_prompt) // DEC_SEQ
    q, k, v = _qkv(x, n_prompt, seq, gmix, w_qkv, q_norm_g, k_norm_g)
    qs, ks, vs = q[:, n_prompt:], k[:, n_prompt:], v[:, n_prompt:]
    outs_p, lses_p, outs_s, lses_s, kv_p, kv_s = [], [], [], [], [], []
    for g in range(N_ATTN_GROUPS):
        o, lse = _attn_prompt(q, k, v, g, batch, seq)
        outs_p.append(o)
        lses_p.append(lse)
        k_rows, v_rows = _from_slabs(ks, g), _from_slabs(vs, g)
        o, lse = _attn_sample(_from_slabs(qs, g), k_rows, v_rows, caches[g], g)
        outs_s.append(_to_slabs(o))
        lses_s.append(_to_slabs(lse))
        keep = min(WINDOWS[g], seq)
        heads = (HEADS_PER_GROUP, HEAD_DIM)
        tail = lambda a: _from_slabs(jnp.concatenate(
            [a[g * GROUP_SLABS:(g + 1) * GROUP_SLABS, (b + 1) * seq - keep:(b + 1) * seq]
             for b in range(batch)], axis=1), 0)
        kv_p.append(jnp.stack([tail(k), tail(v)], axis=1).reshape(batch, keep, 2, *heads))
        kv_s.append(jnp.stack([k_rows, v_rows], axis=1).reshape(DEC_SEQ, nb, 2, *heads)
                    .transpose(1, 0, 2, 3, 4))
    wout = w_out.astype(BF16)
    x = _merge_out_proj(x, outs_p, lses_p, wout, 0, n_prompt // ROW_TILE)
    x = _merge_out_proj(x, outs_s, lses_s, wout, n_prompt // ROW_TILE, (n - n_prompt) // ROW_TILE)
    return x, kv_p, kv_s


def _ffn_body(x_ref, g_ref, wg_ref, wu_ref, wd_ref, o_ref):
    x = x_ref[...]
    h = _rms(x, g_ref[...]).astype(BF16)
    y = x
    for c in range(D_FF // FF_CHUNK):
        cols = slice(c * FF_CHUNK, (c + 1) * FF_CHUNK)
        act = _silu(_dot(h, wg_ref[:, cols])) * _dot(h, wu_ref[:, cols])
        y = y + _dot(act.astype(BF16), wd_ref[cols, :])
    o_ref[...] = y


def _dense_ffn(x, g, w_gu, w_down):
    n = x.shape[0]
    wgu = w_gu.astype(BF16)
    half = pl.Buffered(1)
    return pl.pallas_call(
        _ffn_body,
        grid=(n // ROW_TILE,),
        in_specs=[_row_spec(D_MODEL), _resident((1, D_MODEL)),
                  pl.BlockSpec((D_MODEL, D_FF), lambda i: (0, 0), pipeline_mode=half),
                  pl.BlockSpec((D_MODEL, D_FF), lambda i: (0, 1), pipeline_mode=half),
                  _resident((D_FF, D_MODEL))],
        out_specs=_row_spec(D_MODEL),
        out_shape=jax.ShapeDtypeStruct((n, D_MODEL), F32),
        input_output_aliases={0: 0},
        compiler_params=_params("parallel"),
        name="dense_ffn",
    )(x, g.reshape(1, D_MODEL), wgu, wgu, w_down.astype(BF16))


def _router_body(x_ref, g_ref, r_ref, h_ref, idx_ref, gate_ref):
    h = _rms(x_ref[...], g_ref[...])
    h_ref[...] = h
    logits = jnp.dot(h, r_ref[...], precision=lax.Precision.HIGHEST, preferred_element_type=F32)
    lt = logits.T[:N_EXPERTS, :]
    e = lax.broadcasted_iota(jnp.int32, lt.shape, 0)
    m1 = jnp.max(lt, axis=0, keepdims=True)
    e1 = jnp.min(jnp.where(lt == m1, e, N_EXPERTS), axis=0, keepdims=True)
    rest = jnp.where(e == e1, -jnp.inf, lt)
    m2 = jnp.max(rest, axis=0, keepdims=True)
    e2 = jnp.min(jnp.where(rest == m2, e, N_EXPERTS), axis=0, keepdims=True)
    a = jnp.exp(m2 - m1)
    inv = 1.0 / (1.0 + a)
    idx_ref[...] = jnp.concatenate([e1, e2], axis=0)
    gate_ref[...] = jnp.concatenate([inv, a * inv], axis=0)


def _router(x, g, router):
    n = x.shape[0]
    r = jnp.pad(router, ((0, 0), (0, LANES - N_EXPERTS)))
    pair = pl.BlockSpec((2, ROW_TILE), lambda i: (0, i))
    return pl.pallas_call(
        _router_body,
        grid=(n // ROW_TILE,),
        in_specs=[_row_spec(D_MODEL), _resident((1, D_MODEL)), _resident((D_MODEL, LANES))],
        out_specs=[_row_spec(D_MODEL), pair, pair],
        out_shape=[jax.ShapeDtypeStruct((n, D_MODEL), F32),
                   jax.ShapeDtypeStruct((2, n), jnp.int32),
                   jax.ShapeDtypeStruct((2, n), F32)],
        compiler_params=_params("parallel"),
        name="moe_router",
    )(x, g.reshape(1, D_MODEL), r)


def _moe_body(te_ref, nsub_ref, src_ref, h_hbm, wg_ref, wu_ref, wd_ref, gs_ref, y_ref,
              acc_ref, xbuf, xb_ref, wgb, wub, wdb, sem):
    i, f = pl.program_id(0), pl.program_id(1)
    last_tile = pl.num_programs(0) - 1
    slot = i % 2
    nxt = jnp.minimum(i + 1, last_tile)
    active = nsub_ref[i] > 0
    del te_ref

    def sub_rows(j):
        return pl.ds(j * MOE_SUB, MOE_SUB)

    def start_row(tile, slot, r):
        tok = src_ref[tile * MOE_TILE + r]
        pltpu.make_async_copy(h_hbm.at[pl.ds(tok, 1)], xbuf.at[slot, pl.ds(r, 1)],
                              sem.at[slot, r // MOE_SUB]).start()

    def start_chunk(tile, slot, chunk):
        for k in range(MOE_FETCH):
            start_row(tile, slot, chunk * MOE_FETCH + k)

    def wait_tile(slot):
        for j in range(MOE_SUBS):
            blk = xbuf.at[slot, sub_rows(j)]
            pltpu.make_async_copy(blk, blk, sem.at[slot, j]).wait()

    def expert_block(j):
        xb = xb_ref[sub_rows(j), :]
        act = _silu(_dot(xb, wgb[...])) * _dot(xb, wub[...])
        acc_ref[sub_rows(j), :] += _dot(act.astype(BF16), wdb[...])

    def take_rows(j):
        xb_ref[sub_rows(j), :] = xbuf[slot, sub_rows(j), :].astype(BF16)
        acc_ref[sub_rows(j), :] = jnp.zeros((MOE_SUB, D_MODEL), F32)

    @pl.when((i == 0) & (f == 0))
    def _():
        def row(r, carry):
            start_row(0, 0, r)
            return carry
        lax.fori_loop(0, MOE_TILE, row, 0, unroll=8)

    @pl.when(f == 0)
    def _():
        @pl.when((i == 0) | (nsub_ref[jnp.maximum(i - 1, 0)] > 0))
        def _():
            wait_tile(slot)

        @pl.when(active)
        def _():
            take_rows(0)
            start_chunk(nxt, 1 - slot, 0)

        for j in range(1, MOE_SUBS):
            @pl.when(j < nsub_ref[i])
            def _():
                take_rows(j)

    @pl.when(active)
    def _():
        wgb[...] = wg_ref[0, 0].astype(BF16)
        wub[...] = wu_ref[0, 0].astype(BF16)
        wdb[...] = wd_ref[0, 0].astype(BF16)
        expert_block(0)
        start_chunk(nxt, 1 - slot, f + 1)
        for j in range(1, MOE_SUBS):
            @pl.when(j < nsub_ref[i])
            def _():
                expert_block(j)

    @pl.when(f == MOE_FF_STEPS - 1)
    def _():
        for j in range(MOE_SUBS):
            @pl.when(j < nsub_ref[i])
            def _():
                y_ref[sub_rows(j), :] = acc_ref[sub_rows(j), :] * gs_ref[sub_rows(j), :]

            @pl.when(j >= nsub_ref[i])
            def _():
                y_ref[sub_rows(j), :] = jnp.zeros((MOE_SUB, D_MODEL), F32)

        @pl.when(active & (i == last_tile))
        def _():
            wait_tile(1 - slot)


def _moe_experts(h, src, gs, tile_expert, tile_subs, w_gu, w_down, layer):
    rows = src.shape[0]
    tiles = rows // MOE_TILE
    last = MOE_FF_STEPS - 1
    fstep = lambda i, f, ns: jnp.where(ns[i] > 0, f, last)
    grid_spec = pltpu.PrefetchScalarGridSpec(
        num_scalar_prefetch=3,
        grid=(tiles, MOE_FF_STEPS),
        in_specs=[
            pl.BlockSpec(memory_space=pl.ANY),
            pl.BlockSpec((1, 1, D_MODEL, MOE_FF_CHUNK),
                         lambda i, f, te, ns, sr: (layer, te[i], 0, fstep(i, f, ns))),
            pl.BlockSpec((1, 1, D_MODEL, MOE_FF_CHUNK),
                         lambda i, f, te, ns, sr: (layer, te[i], 0, MOE_FF_STEPS + fstep(i, f, ns))),
            pl.BlockSpec((1, 1, MOE_FF_CHUNK, D_MODEL),
                         lambda i, f, te, ns, sr: (layer, te[i], fstep(i, f, ns), 0)),
            pl.BlockSpec((MOE_TILE, 1), lambda i, f, te, ns, sr: (i, 0)),
        ],
        out_specs=pl.BlockSpec((MOE_TILE, D_MODEL), lambda i, f, te, ns, sr: (i, 0)),
        scratch_shapes=[
            pltpu.VMEM((MOE_TILE, D_MODEL), F32),
            pltpu.VMEM((2, MOE_TILE, D_MODEL), F32),
            pltpu.VMEM((MOE_TILE, D_MODEL), BF16),
            pltpu.VMEM((D_MODEL, MOE_FF_CHUNK), BF16),
            pltpu.VMEM((D_MODEL, MOE_FF_CHUNK), BF16),
            pltpu.VMEM((MOE_FF_CHUNK, D_MODEL), BF16),
            pltpu.SemaphoreType.DMA((2, MOE_SUBS)),
        ],
    )
    return pl.pallas_call(
        _moe_body,
        grid_spec=grid_spec,
        out_shape=jax.ShapeDtypeStruct((rows, D_MODEL), F32),
        compiler_params=_params("arbitrary", "arbitrary"),
        name="moe_experts",
    )(tile_expert, tile_subs, src, h, w_gu, w_gu, w_down, gs)


def _route(idx, gates, n):
    assign = 2 * n
    tiles = -(-(assign + N_EXPERTS * (MOE_TILE - 1)) // MOE_TILE)
    experts = jnp.arange(N_EXPERTS, dtype=jnp.int32)
    ea = idx.reshape(assign)
    ga = gates.reshape(assign)
    onehot = (ea[:, None] == experts[None, :]).astype(jnp.int32)
    csum = jnp.cumsum(onehot, axis=0)
    counts = csum[-1]
    etiles = (counts + MOE_TILE - 1) // MOE_TILE
    tile_end = jnp.cumsum(etiles)
    tile_start = tile_end - etiles
    cnt_start = jnp.cumsum(counts) - counts
    pos = jnp.sum(onehot * ((tile_start * MOE_TILE)[None, :] + csum - 1), axis=1)
    token = jnp.arange(assign, dtype=jnp.int32) % n
    _, sorted_token, sorted_gate = lax.sort((ea, token, ga), num_keys=1, is_stable=True)
    ti = jnp.arange(tiles, dtype=jnp.int32)
    active = ti < tile_end[-1]
    te_raw = jnp.sum((ti[:, None] >= tile_end[None, :]).astype(jnp.int32), axis=1)
    last_expert = jnp.max(jnp.where(etiles > 0, experts, 0))
    te = jnp.where(active, jnp.minimum(te_raw, N_EXPERTS - 1), last_expert)
    tile_onehot = (te[:, None] == experts[None, :]).astype(jnp.int32)
    pick = lambda per_expert: jnp.sum(tile_onehot * per_expert[None, :], axis=1)
    tile_in_expert = ti - pick(tile_start)
    tv = jnp.where(active, jnp.clip(pick(counts) - tile_in_expert * MOE_TILE, 0, MOE_TILE), 0)
    first = pick(cnt_start) + tile_in_expert * MOE_TILE
    r = jnp.arange(MOE_TILE, dtype=jnp.int32)
    valid = (r[None, :] < tv[:, None]).reshape(-1)
    sidx = jnp.clip(first[:, None] + r[None, :], 0, assign - 1).reshape(-1)
    src = jnp.where(valid, jnp.take(sorted_token, sidx, mode="clip"), 0)
    gs = jnp.where(valid, jnp.take(sorted_gate, sidx, mode="clip"), 0.0)
    return src, gs.reshape(-1, 1), te, (tv + MOE_SUB - 1) // MOE_SUB, pos


def _moe_layer(x, g, router, w_gu, w_down, layer, split=None):
    n = x.shape[0]
    h, idx, gates = _router(x, g, router)
    src, gs, te, subs, pos = _route(idx, gates, n)
    y = _moe_experts(h, src, gs, te, subs, w_gu, w_down, layer)
    y1 = jnp.take(y, pos[:n], axis=0, mode="clip")
    y2 = jnp.take(y, pos[n:], axis=0, mode="clip")
    if split is None:
        return x + (y1 + y2)
    return x[:split] + (y1[:split] + y2[:split]), x[split:] + (y1[split:] + y2[split:])


def kernel(x_prompt, x_sample, state_conv, cache_kv_w128, cache_kv_w512, cache_kv_w2048, norm_mix_g, norm_ffn_g, a_w_in, a_ln_g, a_ln_b, a_w_s, a_b_s, a_w_out, b_w_in, b_conv_w, b_w_out, c_w_qkv, c_q_norm_g, c_k_norm_g, c_w_out, f_w_gu, f_w_down, m_router, m_w_gu, m_w_down):
    batch, seq, _ = x_prompt.shape
    nb = x_sample.shape[0]
    n_prompt = batch * seq
    caches = (cache_kv_w128, cache_kv_w512, cache_kv_w2048)
    x = jnp.concatenate([x_prompt.reshape(n_prompt, D_MODEL),
                         x_sample.transpose(1, 0, 2).reshape(DEC_SEQ * nb, D_MODEL)], axis=0)
    depth = norm_mix_g.shape[0]
    sgu_v, conv_p, conv_s = [], [], []
    kv_p = [[] for _ in range(N_ATTN_GROUPS)]
    kv_s = [[] for _ in range(N_ATTN_GROUPS)]
    for i in range(depth):
        kind, j = i % 3, i // 3
        if kind == 0:
            x, v_rows = _sgu_layer(x, n_prompt, norm_mix_g[i], a_w_in[j], a_ln_g[j], a_ln_b[j],
                                   a_w_s[j], a_b_s[j], a_w_out[j])
            sgu_v.append(v_rows.reshape(DEC_SEQ, nb, D_GATE).transpose(1, 0, 2))
        elif kind == 1:
            x, tail_p, tail_s = _conv_layer(x, n_prompt, batch, state_conv[j], norm_mix_g[i],
                                            b_w_in[j], b_conv_w[j], b_w_out[j])
            conv_p.append(tail_p)
            conv_s.append(tail_s)
        else:
            x, kvp, kvs = _attn_layer(x, n_prompt, batch, [c[j] for c in caches], norm_mix_g[i],
                                      c_w_qkv[j], c_q_norm_g[j], c_k_norm_g[j], c_w_out[j])
            for g in range(N_ATTN_GROUPS):
                kv_p[g].append(kvp[g])
                kv_s[g].append(kvs[g])
        e = i // 2
        if i % 2 == 0:
            x = _dense_ffn(x, norm_ffn_g[i], f_w_gu[e], f_w_down[e])
        else:
            x = _moe_layer(x, norm_ffn_g[i], m_router[e], m_w_gu, m_w_down, e,
                           split=n_prompt if i == depth - 1 else None)
    xp, xs = x if isinstance(x, tuple) else (x[:n_prompt], x[n_prompt:])
    y_prompt = xp.reshape(batch, seq, D_MODEL)
    y_sample = xs.reshape(DEC_SEQ, nb, D_MODEL).transpose(1, 0, 2)
    stack = lambda xs: jnp.stack(xs, axis=0)
    return (y_prompt, y_sample, stack(conv_p), stack(conv_s),
            stack(kv_p[0]), stack(kv_s[0]), stack(kv_p[1]), stack(kv_s[1]),
            stack(kv_p[2]), stack(kv_s[2]), stack(sgu_v))
```

```python
import functools

import numpy as np
import jax
import jax.numpy as jnp
from jax import lax
from jax.experimental import pallas as pl
from jax.experimental.pallas import tpu as pltpu

F32 = jnp.float32
BF16 = jnp.bfloat16

D_MODEL = 1024
EPS = 1e-6
CHUNK = 128
D_GATE = 2 * D_MODEL
SGU_GROUPS = 8
SGU_GDIM = D_GATE // SGU_GROUPS
D_CONV = D_MODEL
HEAD_DIM = 64
HEADS_PER_GROUP = 4
GROUP_DIM = HEADS_PER_GROUP * HEAD_DIM
WINDOWS = (128, 512, 2048)
DILATIONS = (1, 4, 16)
N_ATTN_GROUPS = 3
D_ATTN = N_ATTN_GROUPS * GROUP_DIM
ROPE_DIM = HEAD_DIM // 4
ROPE_THETA = 500000.0
ATTN_BLOCK = 128
D_FF = 2816
N_EXPERTS = 8
D_FF_EXPERT = 3584
DEC_SEQ = 4
PAST_LEN = 2048

SUBLANES = 8
LANES = 128
VMEM_LIMIT = 56 * 1024 * 1024

ROW_TILE = 512
FF_CHUNK = D_FF // 2
MOE_TILE = 1024
MOE_SUB = 256
MOE_SUBS = MOE_TILE // MOE_SUB
MOE_FF_CHUNK = 512
MOE_FF_STEPS = D_FF_EXPERT // MOE_FF_CHUNK
assert D_MODEL == SUBLANES * LANES


def _dot(a, b):
    return jnp.dot(a, b, preferred_element_type=F32)


def _dot_nt(a, b):
    return lax.dot_general(a, b, (((1,), (1,)), ((), ())), preferred_element_type=F32)


def _rms(x, g):
    return x * lax.rsqrt(jnp.mean(x * x, axis=-1, keepdims=True) + EPS) * g


def _gelu_tanh(x):
    c = np.float32(np.sqrt(2 / np.pi))
    return x * (0.5 * (1.0 + jnp.tanh(c * (x + np.float32(0.044715) * (x * x * x)))))


def _silu(x):
    return x * (1.0 / (1.0 + jnp.exp(-x)))


def _params(*sem):
    return pltpu.CompilerParams(dimension_semantics=sem, vmem_limit_bytes=VMEM_LIMIT)


def _resident(shape):
    zeros = (0,) * len(shape)
    return pl.BlockSpec(shape, lambda *_: zeros, pipeline_mode=pl.Buffered(1))


def _row_spec(width, first_tile=0):
    return pl.BlockSpec((ROW_TILE, width), lambda i: (i + first_tile, 0))


def _sgu_front(x, gmix, winv_ref, lng, lnb):
    h = _rms(x, gmix).astype(BF16)
    v = _gelu_tanh(_dot(h, winv_ref[...]))
    mu = jnp.mean(v, axis=-1, keepdims=True)
    vc = v - mu
    var = jnp.mean(vc * vc, axis=-1, keepdims=True)
    return h, vc * lax.rsqrt(var + EPS) * lng + lnb


def _sgu_back(x, h, winu_ref, s_ref, wout_ref, o_ref):
    u = _gelu_tanh(_dot(h, winu_ref[...]))
    o_ref[...] = x + _dot((u * s_ref[...]).astype(BF16), wout_ref[...])


def _sgu_prompt_body(x_ref, gmix_ref, winu_ref, winv_ref, lng_ref, lnb_ref, ws_ref, bs_ref,
                     wout_ref, o_ref, s_ref):
    x = x_ref[...]
    h, vn = _sgu_front(x, gmix_ref[...], winv_ref, lng_ref[...], lnb_ref[...])
    vb = vn.astype(BF16)
    row = lax.broadcasted_iota(jnp.int32, (CHUNK, CHUNK), 0)
    col = lax.broadcasted_iota(jnp.int32, (CHUNK, CHUNK), 1)
    for g in range(SGU_GROUPS):
        wm = jnp.where(row >= col, ws_ref[g], 0.0).astype(BF16)
        bias = jnp.concatenate([bs_ref[g]] * (SGU_GDIM // LANES), axis=1)
        cols = slice(g * SGU_GDIM, (g + 1) * SGU_GDIM)
        for c in range(ROW_TILE // CHUNK):
            rows = slice(c * CHUNK, (c + 1) * CHUNK)
            s_ref[rows, cols] = _dot(wm, vb[rows, cols]) + bias
    _sgu_back(x, h, winu_ref, s_ref, wout_ref, o_ref)


def _sgu_sample_body(w4_ref, b4_ref, x_ref, gmix_ref, winu_ref, winv_ref, lng_ref, lnb_ref,
                     wout_ref, *rest):
    o_ref, v_ref, s_ref = rest[-3:]
    x = x_ref[...]
    nb = x.shape[0] // DEC_SEQ
    h, vn = _sgu_front(x, gmix_ref[...], winv_ref, lng_ref[...], lnb_ref[...])
    v_ref[...] = vn
    for g in range(SGU_GROUPS):
        cols = slice(g * SGU_GDIM, (g + 1) * SGU_GDIM)
        for t in range(DEC_SEQ):
            acc = w4_ref[g, t * DEC_SEQ] * vn[0:nb, cols]
            for j in range(1, t + 1):
                acc = acc + w4_ref[g, t * DEC_SEQ + j] * vn[j * nb:(j + 1) * nb, cols]
            s_ref[t * nb:(t + 1) * nb, cols] = acc + b4_ref[g, t]
    _sgu_back(x, h, winu_ref, s_ref, wout_ref, o_ref)


def _sgu_layer(x, n_prompt, gmix, w_in, ln_g, ln_b, w_s, b_s, w_out, x_sample=None):
    n = x.shape[0] if x_sample is None else x.shape[0] + x_sample.shape[0]
    n_sample = n - n_prompt
    assert n_sample == ROW_TILE and n_prompt % ROW_TILE == 0
    winu = w_in[:, :D_GATE].astype(BF16)
    winv = w_in[:, D_GATE:].astype(BF16)
    wout = w_out.astype(BF16)
    gmix = gmix.reshape(1, D_MODEL)
    lng = ln_g.reshape(1, D_GATE)
    lnb = ln_b.reshape(1, D_GATE)
    bs = jnp.broadcast_to(b_s[:, :, None], (SGU_GROUPS, CHUNK, LANES))
    weights = [_resident((1, D_MODEL)), _resident((D_MODEL, D_GATE)), _resident((D_MODEL, D_GATE)),
               _resident((1, D_GATE)), _resident((1, D_GATE))]
    x = pl.pallas_call(
        _sgu_prompt_body,
        grid=(n_prompt // ROW_TILE,),
        in_specs=[_row_spec(D_MODEL)] + weights + [
            _resident((SGU_GROUPS, CHUNK, CHUNK)), _resident((SGU_GROUPS, CHUNK, LANES)),
            _resident((D_GATE, D_MODEL))],
        out_specs=_row_spec(D_MODEL),
        out_shape=jax.ShapeDtypeStruct((n, D_MODEL), F32),
        scratch_shapes=[pltpu.VMEM((ROW_TILE, D_GATE), F32)],
        input_output_aliases={0: 0} if x_sample is None else {},
        compiler_params=_params("parallel"),
        name="sgu_prompt",
    )(x, gmix, winu, winv, lng, lnb, w_s, bs, wout)
    w4 = w_s[:, :DEC_SEQ, :DEC_SEQ].reshape(SGU_GROUPS, DEC_SEQ * DEC_SEQ)
    b4 = b_s[:, :DEC_SEQ]
    smem = pl.BlockSpec(memory_space=pltpu.SMEM)
    first = n_prompt // ROW_TILE
    args = [w4, b4, x if x_sample is None else x_sample, gmix, winu, winv, lng, lnb, wout]
    in_specs = ([smem, smem, _row_spec(D_MODEL, first if x_sample is None else 0)] + weights
                + [_resident((D_GATE, D_MODEL))])
    if x_sample is not None:
        args.append(x)
        in_specs.append(pl.BlockSpec(memory_space=pl.ANY))
    x, v_rows = pl.pallas_call(
        _sgu_sample_body,
        grid=(1,),
        in_specs=in_specs,
        out_specs=[_row_spec(D_MODEL, first), pl.BlockSpec((ROW_TILE, D_GATE), lambda i: (0, 0))],
        out_shape=[jax.ShapeDtypeStruct((n, D_MODEL), F32),
                   jax.ShapeDtypeStruct((n_sample, D_GATE), F32)],
        scratch_shapes=[pltpu.VMEM((ROW_TILE, D_GATE), F32)],
        input_output_aliases={(2 if x_sample is None else len(args) - 1): 0},
        compiler_params=_params("arbitrary"),
        name="sgu_sample",
    )(*args)
    return x, v_rows


def _conv_front(x, gmix, win_ref):
    h = _rms(x, gmix).astype(BF16)
    gate_b = _dot(h, win_ref[:, 0:D_CONV])
    gate_c = _dot(h, win_ref[:, D_CONV:2 * D_CONV])
    xi = _dot(h, win_ref[:, 2 * D_CONV:3 * D_CONV])
    return gate_b, gate_c * xi


def _conv_prompt_body(x_ref, gmix_ref, win_ref, cw_ref, wout_ref, o_ref, tail_ref, zbuf, *,
                      tiles_per_seq):
    @pl.when(pl.program_id(0) % tiles_per_seq == 0)
    def _():
        zbuf[0:SUBLANES, :] = jnp.zeros((SUBLANES, D_CONV), F32)

    x = x_ref[...]
    gate_b, z = _conv_front(x, gmix_ref[...], win_ref)
    zbuf[SUBLANES:SUBLANES + ROW_TILE, :] = z
    conv = (cw_ref[2:3, :] * z
            + cw_ref[1:2, :] * zbuf[SUBLANES - 1:SUBLANES - 1 + ROW_TILE, :]
            + cw_ref[0:1, :] * zbuf[SUBLANES - 2:SUBLANES - 2 + ROW_TILE, :])
    o_ref[...] = x + _dot((gate_b * conv).astype(BF16), wout_ref[...])
    tail = zbuf[ROW_TILE + SUBLANES - 2:ROW_TILE + SUBLANES, :]
    tail_ref[0] = tail
    zbuf[SUBLANES - 2:SUBLANES, :] = tail


def _conv_sample_body(x_ref, gmix_ref, win_ref, cw_ref, st_ref, wout_ref, o_ref, tail_ref):
    x = x_ref[...]
    nb = x.shape[0] // DEC_SEQ
    gate_b, z = _conv_front(x, gmix_ref[...], win_ref)
    z1 = jnp.concatenate([st_ref[1], z[:(DEC_SEQ - 1) * nb]], axis=0)
    z2 = jnp.concatenate([st_ref[0], st_ref[1], z[:(DEC_SEQ - 2) * nb]], axis=0)
    conv = cw_ref[2:3, :] * z + cw_ref[1:2, :] * z1 + cw_ref[0:1, :] * z2
    o_ref[...] = x + _dot((gate_b * conv).astype(BF16), wout_ref[...])
    tail_ref[...] = z[(DEC_SEQ - 2) * nb:]


def _conv_layer(x, n_prompt, batch, state, gmix, w_in, conv_w, w_out):
    n = x.shape[0]
    n_sample = n - n_prompt
    nb = n_sample // DEC_SEQ
    assert n_sample == ROW_TILE
    tiles_per_seq = n_prompt // batch // ROW_TILE
    win = w_in.astype(BF16)
    wout = w_out.astype(BF16)
    gmix = gmix.reshape(1, D_MODEL)
    weights = [_resident((1, D_MODEL)), _resident((D_MODEL, 3 * D_CONV)), _resident((3, D_CONV))]
    x, tail_p = pl.pallas_call(
        functools.partial(_conv_prompt_body, tiles_per_seq=tiles_per_seq),
        grid=(n_prompt // ROW_TILE,),
        in_specs=[_row_spec(D_MODEL)] + weights + [_resident((D_CONV, D_MODEL))],
        out_specs=[_row_spec(D_MODEL),
                   pl.BlockSpec((1, 2, D_CONV), lambda i: (i // tiles_per_seq, 0, 0))],
        out_shape=[jax.ShapeDtypeStruct((n, D_MODEL), F32),
                   jax.ShapeDtypeStruct((batch, 2, D_CONV), F32)],
        scratch_shapes=[pltpu.VMEM((ROW_TILE + SUBLANES, D_CONV), F32)],
        input_output_aliases={0: 0},
        compiler_params=_params("arbitrary"),
        name="conv_prompt",
    )(x, gmix, win, conv_w, wout)
    first = n_prompt // ROW_TILE
    st = state.transpose(1, 0, 2)
    x, tail_s = pl.pallas_call(
        _conv_sample_body,
        grid=(1,),
        in_specs=[_row_spec(D_MODEL, first)] + weights + [
            _resident((2, nb, D_CONV)), _resident((D_CONV, D_MODEL))],
        out_specs=[_row_spec(D_MODEL, first), pl.BlockSpec((2 * nb, D_CONV), lambda i: (0, 0))],
        out_shape=[jax.ShapeDtypeStruct((n, D_MODEL), F32),
                   jax.ShapeDtypeStruct((2 * nb, D_CONV), F32)],
        input_output_aliases={0: 0},
        compiler_params=_params("arbitrary"),
        name="conv_sample",
    )(x, gmix, win, conv_w, st, wout)
    return x, tail_p, tail_s.reshape(2, nb, D_CONV).transpose(1, 0, 2)


def _rope_tables(seq, n_sample_rows):
    nb = n_sample_rows // DEC_SEQ
    pos = jnp.concatenate([jnp.arange(seq, dtype=jnp.int32),
                           PAST_LEN + jnp.arange(n_sample_rows, dtype=jnp.int32) // nb])
    inv_freq = jnp.power(ROPE_THETA, -jnp.arange(0, ROPE_DIM, 2, dtype=F32) / ROPE_DIM)
    d = np.arange(LANES) % HEAD_DIM
    half = ROPE_DIM // 2
    ang = pos.astype(F32)[:, None] * inv_freq[d % half][None, :]
    cos, sin = jnp.cos(ang), jnp.sin(ang)
    c = jnp.where(d[None, :] < ROPE_DIM, cos, 1.0)
    s1 = jnp.where(d[None, :] < half, -sin, 0.0)
    s2 = jnp.where((d[None, :] >= half) & (d[None, :] < ROPE_DIM), sin, 0.0)
    return c, s1, s2


def _qkv_body(x_ref, gmix_ref, w_ref, gq_ref, gk_ref, c_ref, s1_ref, s2_ref, bm_ref,
              q_ref, k_ref, v_ref):
    h = _rms(x_ref[...], gmix_ref[...]).astype(BF16)
    c, s1, s2 = c_ref[...], s1_ref[...], s2_ref[...]
    bm = bm_ref[...]

    def head_norm_rope(t, gain, out_ref, scale):
        for a in range(D_ATTN // GROUP_DIM):
            ts = t[:, a * GROUP_DIM:(a + 1) * GROUP_DIM]
            sq = ts * ts
            hi = sq.astype(BF16)
            lo = (sq - hi.astype(F32)).astype(BF16)
            ms = _dot(hi, bm) + _dot(lo, bm)
            tn = ts * lax.rsqrt(ms + EPS) * gain[:, a * GROUP_DIM:(a + 1) * GROUP_DIM]
            for b in range(GROUP_DIM // LANES):
                th = tn[:, b * LANES:(b + 1) * LANES]
                rot = (th * c + pltpu.roll(th, LANES - ROPE_DIM // 2, 1) * s1
                       + pltpu.roll(th, ROPE_DIM // 2, 1) * s2)
                out_ref[a * (GROUP_DIM // LANES) + b] = rot if scale is None else rot * scale

    head_norm_rope(_dot(h, w_ref[:, 0:D_ATTN]), gq_ref[...], q_ref, np.float32(HEAD_DIM ** -0.5))
    head_norm_rope(_dot(h, w_ref[:, D_ATTN:2 * D_ATTN]), gk_ref[...], k_ref, None)
    v = _dot(h, w_ref[:, 2 * D_ATTN:3 * D_ATTN])
    for s in range(D_ATTN // LANES):
        v_ref[s] = v[:, s * LANES:(s + 1) * LANES]


def _qkv(x, n_prompt, seq, gmix, w_qkv, q_norm_g, k_norm_g):
    n = x.shape[0]
    n_sample = n - n_prompt
    tiles_per_seq = seq // ROW_TILE
    prompt_tiles = n_prompt // ROW_TILE
    c, s1, s2 = _rope_tables(seq, n_sample)
    heads = D_ATTN // HEAD_DIM
    slabs = D_ATTN // LANES
    gq = jnp.tile(q_norm_g, heads).reshape(1, D_ATTN)
    gk = jnp.tile(k_norm_g, heads).reshape(1, D_ATTN)
    lane_head = np.arange(GROUP_DIM) // HEAD_DIM
    bm = jnp.asarray((lane_head[:, None] == lane_head[None, :]) / HEAD_DIM, BF16)
    table = pl.BlockSpec(
        (ROW_TILE, LANES),
        lambda i: (jnp.where(i < prompt_tiles, i % tiles_per_seq, tiles_per_seq), 0))
    return pl.pallas_call(
        _qkv_body,
        grid=(n // ROW_TILE,),
        in_specs=[_row_spec(D_MODEL), _resident((1, D_MODEL)), _resident((D_MODEL, 3 * D_ATTN)),
                  _resident((1, D_ATTN)), _resident((1, D_ATTN)), table, table, table,
                  _resident((GROUP_DIM, GROUP_DIM))],
        out_specs=[pl.BlockSpec((slabs, ROW_TILE, LANES), lambda i: (0, i, 0))] * 3,
        out_shape=[jax.ShapeDtypeStruct((slabs, n, LANES), F32)] * 3,
        compiler_params=_params("parallel"),
        name="qkv",
    )(x, gmix.reshape(1, D_MODEL), w_qkv.astype(BF16), gq, gk, c, s1, s2, bm)


GROUP_SLABS = GROUP_DIM // LANES


def _attn_prompt_body(q_ref, kc_ref, kp_ref, vc_ref, vp_ref, o_ref, l_ref, *, dil, blocks):
    first_key = jnp.where(pl.program_id(1) == 0, ATTN_BLOCK, 0)
    lane = lax.broadcasted_iota(jnp.int32, (1, GROUP_DIM), 1)
    head_masks = [lane // HEAD_DIM == hd for hd in range(HEADS_PER_GROUP)]
    qi = lax.broadcasted_iota(jnp.int32, (ATTN_BLOCK, 2 * ATTN_BLOCK), 0)
    kj = lax.broadcasted_iota(jnp.int32, (ATTN_BLOCK, 2 * ATTN_BLOCK), 1)
    dist = qi + ATTN_BLOCK - kj
    band = (dist >= 0) & (dist <= ATTN_BLOCK)
    band_first = band & (kj >= first_key)

    def stream_rows(r, j):
        if dil == 1:
            return pl.ds(j * ATTN_BLOCK, ATTN_BLOCK)
        return pl.ds(r + j * ATTN_BLOCK * dil, ATTN_BLOCK, stride=dil)

    def load(ref, rows):
        return jnp.concatenate([ref[s, rows, :] for s in range(GROUP_SLABS)], axis=1)

    for r in range(dil):
        for j in range(blocks):
            rows = stream_rows(r, j)
            q = load(q_ref, rows).astype(BF16)
            if j == 0:
                k_prev, v_prev, mask = load(kp_ref, stream_rows(r, 0)), load(vp_ref, stream_rows(r, 0)), band_first
            else:
                prev = stream_rows(r, j - 1)
                k_prev, v_prev, mask = load(kc_ref, prev), load(vc_ref, prev), band
            kk = jnp.concatenate([k_prev, load(kc_ref, rows)], axis=0).astype(BF16)
            vv = jnp.concatenate([v_prev, load(vc_ref, rows)], axis=0).astype(BF16)
            o_acc = jnp.zeros((ATTN_BLOCK, GROUP_DIM), F32)
            l_acc = jnp.zeros((ATTN_BLOCK, GROUP_DIM), F32)
            for hm in head_masks:
                s = _dot_nt(jnp.where(hm, q, jnp.zeros_like(q)), kk)
                s = jnp.where(mask, s, -jnp.inf)
                m = jnp.max(s, axis=-1, keepdims=True)
                p = jnp.exp(s - m)
                l = jnp.sum(p, axis=-1, keepdims=True)
                pv = _dot(p.astype(BF16), vv)
                o_acc = o_acc + jnp.where(hm, pv * (1.0 / l), 0.0)
                l_acc = l_acc + jnp.where(hm, m + jnp.log(l), 0.0)
            for s in range(GROUP_SLABS):
                o_ref[s, rows, :] = o_acc[:, s * LANES:(s + 1) * LANES]
                l_ref[s, rows, :] = l_acc[:, s * LANES:(s + 1) * LANES]


def _attn_prompt(q, k, v, group, batch, seq):
    dil = DILATIONS[group]
    assert WINDOWS[group] // dil == ATTN_BLOCK
    span = ATTN_BLOCK * dil
    tile = max(ROW_TILE, span)
    tiles = seq // tile
    cur = pl.BlockSpec((GROUP_SLABS, tile, LANES), lambda b, t: (group, b * tiles + t, 0))
    prev = pl.BlockSpec(
        (GROUP_SLABS, span, LANES),
        lambda b, t: (group, jnp.maximum((b * tiles + t) * (tile // span) - 1, 0), 0))
    out = pl.BlockSpec((GROUP_SLABS, tile, LANES), lambda b, t: (0, b * tiles + t, 0))
    shape = jax.ShapeDtypeStruct((GROUP_SLABS, batch * seq, LANES), F32)
    return pl.pallas_call(
        functools.partial(_attn_prompt_body, dil=dil, blocks=tile // span),
        grid=(batch, tiles),
        in_specs=[cur, cur, prev, cur, prev],
        out_specs=[out, out],
        out_shape=[shape, shape],
        compiler_params=_params("parallel", "arbitrary"),
        name=f"attn_prompt_g{group}",
    )(q, k, k, v, v)


HEADS_PER_SLAB = LANES // HEAD_DIM


def _attn_sample_body(q_ref, k_ref, v_ref, c_ref, o_ref, lse_ref, *, dil, buf, seqs):
    rows = HEADS_PER_SLAB * SUBLANES
    lane = lax.broadcasted_iota(jnp.int32, (1, LANES), 1)
    head_masks = [lane // HEAD_DIM == hh for hh in range(HEADS_PER_SLAB)]

    def visibility(width):
        t = lax.broadcasted_iota(jnp.int32, (rows, width), 0) % SUBLANES
        c = lax.broadcasted_iota(jnp.int32, (rows, width), 1)
        return jnp.where(t < DEC_SEQ, t, t - DEC_SEQ), c

    tq, c = visibility(buf)
    vis_cache = (c >= tq) if dil == 1 else (c % dil == tq)
    tq, c = visibility(SUBLANES)
    vis_fresh = (c < DEC_SEQ) & ((c <= tq) if dil == 1 else (c == tq))

    def one(bi):
        for s in range(GROUP_SLABS):
            heads = slice(s * HEADS_PER_SLAB, (s + 1) * HEADS_PER_SLAB)
            q, k_new, v_new = q_ref[s, bi], k_ref[s, bi].astype(BF16), v_ref[s, bi]
            q_heads = jnp.concatenate([jnp.where(hm, q, 0.0) for hm in head_masks], axis=0).astype(BF16)
            k_cache = c_ref[bi, 0, heads].reshape(LANES, buf).astype(BF16)
            v_cache = c_ref[bi, 1, heads].reshape(LANES, buf).astype(BF16)
            s_cache = jnp.where(vis_cache, _dot(q_heads, k_cache), -jnp.inf)
            s_fresh = jnp.where(vis_fresh, _dot_nt(q_heads, k_new), -jnp.inf)
            m = jnp.maximum(jnp.max(s_cache, axis=1, keepdims=True), jnp.max(s_fresh, axis=1, keepdims=True))
            p_cache = jnp.exp(s_cache - m)
            p_fresh = jnp.exp(s_fresh - m)
            l = jnp.sum(p_cache, axis=1, keepdims=True) + jnp.sum(p_fresh, axis=1, keepdims=True)
            inv = 1.0 / l
            o = _dot_nt((p_cache * inv).astype(BF16), v_cache)
            w_fresh = p_fresh * inv
            for t2 in range(DEC_SEQ):
                o = o + w_fresh[:, t2:t2 + 1] * v_new[t2:t2 + 1, :]
            lse = jnp.broadcast_to(m + jnp.log(l), (rows, LANES))
            o_slab = jnp.zeros((SUBLANES, LANES), F32)
            lse_slab = jnp.zeros((SUBLANES, LANES), F32)
            for hh, hm in enumerate(head_masks):
                part = slice(hh * SUBLANES, (hh + 1) * SUBLANES)
                o_slab = o_slab + jnp.where(hm, o[part], 0.0)
                lse_slab = lse_slab + jnp.where(hm, lse[part], 0.0)
            o_ref[s, bi] = o_slab
            lse_ref[s, bi] = lse_slab

    def pair(i, carry):
        one(2 * i)
        one(2 * i + 1)
        return carry

    if seqs == 2:
        pair(0, 0)
    else:
        lax.fori_loop(0, seqs // 2, pair, 0)


def _sample_rows(slabs, n_prompt):
    part = slabs[:, n_prompt:]
    nb = part.shape[1] // DEC_SEQ
    part = part.reshape(part.shape[0], DEC_SEQ, nb, LANES).transpose(0, 2, 1, 3)
    return jnp.pad(part, ((0, 0), (0, 0), (0, SUBLANES - DEC_SEQ), (0, 0)))


def _attn_sample(q, k, v, cache, group):
    nb, buf = cache.shape[0], cache.shape[1]
    dil = DILATIONS[group]
    assert buf == ATTN_BLOCK * dil and WINDOWS[group] == buf and (dil == 1 or dil >= DEC_SEQ)
    seqs = max(2, 2 * PAST_LEN // buf)
    fresh = pl.BlockSpec((GROUP_SLABS, seqs, SUBLANES, LANES), lambda i: (group, i, 0, 0))
    out = pl.BlockSpec((GROUP_SLABS, seqs, SUBLANES, LANES), lambda i: (0, i, 0, 0))
    shape = jax.ShapeDtypeStruct((GROUP_SLABS, nb, SUBLANES, LANES), F32)
    o, lse = pl.pallas_call(
        functools.partial(_attn_sample_body, dil=dil, buf=buf, seqs=seqs),
        grid=(nb // seqs,),
        in_specs=[fresh, fresh, fresh,
                  pl.BlockSpec((seqs, 2, HEADS_PER_GROUP, HEAD_DIM, buf), lambda i: (i, 0, 0, 0, 0))],
        out_specs=[out, out],
        out_shape=[shape, shape],
        compiler_params=_params("parallel"),
        name=f"attn_sample_g{group}",
    )(q, k, v, cache.transpose(0, 2, 3, 4, 1))
    rows = lambda a: a[:, :, :DEC_SEQ].transpose(0, 2, 1, 3).reshape(GROUP_SLABS, DEC_SEQ * nb, LANES)
    return rows(o), rows(lse)


def _from_slabs(slabs, group):
    part = slabs[group * GROUP_SLABS:(group + 1) * GROUP_SLABS]
    return part.transpose(1, 0, 2).reshape(part.shape[1], GROUP_DIM)


def _merge_body(x_ref, o0_ref, o1_ref, o2_ref, l0_ref, l1_ref, l2_ref, w_ref, out_ref):
    rows = lambda ref: jnp.concatenate([ref[s] for s in range(GROUP_SLABS)], axis=1)
    l0, l1, l2 = rows(l0_ref), rows(l1_ref), rows(l2_ref)
    m = jnp.maximum(jnp.maximum(l0, l1), l2)
    e0, e1, e2 = jnp.exp(l0 - m), jnp.exp(l1 - m), jnp.exp(l2 - m)
    inv = 1.0 / (e0 + e1 + e2)
    y = x_ref[...]
    for g, (o_ref, e) in enumerate(((o0_ref, e0), (o1_ref, e1), (o2_ref, e2))):
        y = y + _dot((rows(o_ref) * (e * inv)).astype(BF16),
                     w_ref[g * GROUP_DIM:(g + 1) * GROUP_DIM, :])
    out_ref[...] = y


def _merge_out_proj(x, outs, lses, w_out, first_tile, tiles):
    n = x.shape[0]
    part = pl.BlockSpec((GROUP_SLABS, ROW_TILE, LANES), lambda i: (0, i, 0))
    return pl.pallas_call(
        _merge_body,
        grid=(tiles,),
        in_specs=[_row_spec(D_MODEL, first_tile)] + [part] * 6 + [_resident((D_ATTN, D_MODEL))],
        out_specs=_row_spec(D_MODEL, first_tile),
        out_shape=jax.ShapeDtypeStruct((n, D_MODEL), F32),
        input_output_aliases={0: 0},
        compiler_params=_params("parallel"),
        name="attn_merge_out_proj",
    )(x, *outs, *lses, w_out)


def _attn_layer(x, n_prompt, batch, caches, gmix, w_qkv, q_norm_g, k_norm_g, w_out):
    n = x.shape[0]
    seq = n_prompt // batch
    nb = (n - n_prompt) // DEC_SEQ
    q, k, v = _qkv(x, n_prompt, seq, gmix, w_qkv, q_norm_g, k_norm_g)
    ks, vs = k[:, n_prompt:], v[:, n_prompt:]
    q_new, k_new, v_new = (_sample_rows(a, n_prompt) for a in (q, k, v))
    outs_p, lses_p, outs_s, lses_s, kv_p, kv_s = [], [], [], [], [], []
    for g in range(N_ATTN_GROUPS):
        o, lse = _attn_prompt(q, k, v, g, batch, seq)
        outs_p.append(o)
        lses_p.append(lse)
        o, lse = _attn_sample(q_new, k_new, v_new, caches[g], g)
        outs_s.append(o)
        lses_s.append(lse)
        k_rows, v_rows = _from_slabs(ks, g), _from_slabs(vs, g)
        keep = min(WINDOWS[g], seq)
        heads = (HEADS_PER_GROUP, HEAD_DIM)
        tail = lambda a: _from_slabs(jnp.concatenate(
            [a[g * GROUP_SLABS:(g + 1) * GROUP_SLABS, (b + 1) * seq - keep:(b + 1) * seq]
             for b in range(batch)], axis=1), 0)
        kv_p.append(jnp.stack([tail(k), tail(v)], axis=1).reshape(batch, keep, 2, *heads))
        kv_s.append(jnp.stack([k_rows, v_rows], axis=1).reshape(DEC_SEQ, nb, 2, *heads)
                    .transpose(1, 0, 2, 3, 4))
    wout = w_out.astype(BF16)
    x = _merge_out_proj(x, outs_p, lses_p, wout, 0, n_prompt // ROW_TILE)
    x = _merge_out_proj(x, outs_s, lses_s, wout, n_prompt // ROW_TILE, (n - n_prompt) // ROW_TILE)
    return x, kv_p, kv_s


def _ffn_body(x_ref, g_ref, wg_ref, wu_ref, wd_ref, o_ref):
    x = x_ref[...]
    h = _rms(x, g_ref[...]).astype(BF16)
    y = x
    for c in range(D_FF // FF_CHUNK):
        cols = slice(c * FF_CHUNK, (c + 1) * FF_CHUNK)
        act = _silu(_dot(h, wg_ref[:, cols])) * _dot(h, wu_ref[:, cols])
        y = y + _dot(act.astype(BF16), wd_ref[cols, :])
    o_ref[...] = y


def _dense_ffn(x, g, w_gu, w_down):
    n = x.shape[0]
    wgu = w_gu.astype(BF16)
    half = pl.Buffered(1)
    return pl.pallas_call(
        _ffn_body,
        grid=(n // ROW_TILE,),
        in_specs=[_row_spec(D_MODEL), _resident((1, D_MODEL)),
                  pl.BlockSpec((D_MODEL, D_FF), lambda i: (0, 0), pipeline_mode=half),
                  pl.BlockSpec((D_MODEL, D_FF), lambda i: (0, 1), pipeline_mode=half),
                  _resident((D_FF, D_MODEL))],
        out_specs=_row_spec(D_MODEL),
        out_shape=jax.ShapeDtypeStruct((n, D_MODEL), F32),
        input_output_aliases={0: 0},
        compiler_params=_params("parallel"),
        name="dense_ffn",
    )(x, g.reshape(1, D_MODEL), wgu, wgu, w_down.astype(BF16))


def _router_body(x_ref, g_ref, r_ref, h_ref, idx_ref, gate_ref):
    h = _rms(x_ref[...], g_ref[...])
    for s in range(D_MODEL // LANES):
        h_ref[pl.ds(s, ROW_TILE, stride=SUBLANES), :] = h[:, s * LANES:(s + 1) * LANES]
    logits = jnp.dot(h, r_ref[...], precision=lax.Precision.HIGHEST, preferred_element_type=F32)
    lt = logits.T[:N_EXPERTS, :]
    e = lax.broadcasted_iota(jnp.int32, lt.shape, 0)
    m1 = jnp.max(lt, axis=0, keepdims=True)
    e1 = jnp.min(jnp.where(lt == m1, e, N_EXPERTS), axis=0, keepdims=True)
    rest = jnp.where(e == e1, -jnp.inf, lt)
    m2 = jnp.max(rest, axis=0, keepdims=True)
    e2 = jnp.min(jnp.where(rest == m2, e, N_EXPERTS), axis=0, keepdims=True)
    a = jnp.exp(m2 - m1)
    inv = 1.0 / (1.0 + a)
    idx_ref[...] = jnp.concatenate([e1, e2], axis=0)
    gate_ref[...] = jnp.concatenate([inv, a * inv], axis=0)


def _router(x, g, router):
    n = x.shape[0]
    r = jnp.pad(router, ((0, 0), (0, LANES - N_EXPERTS)))
    pair = pl.BlockSpec((2, ROW_TILE), lambda i: (0, i))
    return pl.pallas_call(
        _router_body,
        grid=(n // ROW_TILE,),
        in_specs=[_row_spec(D_MODEL), _resident((1, D_MODEL)), _resident((D_MODEL, LANES))],
        out_specs=[pl.BlockSpec((ROW_TILE * SUBLANES, LANES), lambda i: (i, 0)), pair, pair],
        out_shape=[jax.ShapeDtypeStruct((n * SUBLANES, LANES), F32),
                   jax.ShapeDtypeStruct((2, n), jnp.int32),
                   jax.ShapeDtypeStruct((2, n), F32)],
        compiler_params=_params("parallel"),
        name="moe_router",
    )(x, g.reshape(1, D_MODEL), r)


def _moe_body(te_ref, nsub_ref, src_ref, h_hbm, wg_ref, wu_ref, wd_ref, gs_ref, y_ref,
              acc_ref, xbuf, xb_ref, wgb, wub, wdb, sem):
    i, f = pl.program_id(0), pl.program_id(1)
    slot = i % 2
    del te_ref

    def sub_rows(j):
        return pl.ds(j * MOE_SUB, MOE_SUB)

    def fetch(tile, slot):
        for j in range(MOE_SUBS):
            @pl.when(j < nsub_ref[tile])
            def _():
                def row(r, carry):
                    tok = pl.multiple_of(src_ref[tile * MOE_TILE + j * MOE_SUB + r], SUBLANES)
                    pltpu.make_async_copy(
                        h_hbm.at[pl.ds(tok, SUBLANES)],
                        xbuf.at[slot, pl.ds((j * MOE_SUB + r) * SUBLANES, SUBLANES)],
                        sem.at[slot, j]).start()
                    return carry
                lax.fori_loop(0, MOE_SUB, row, 0, unroll=8)

    @pl.when((i == 0) & (f == 0))
    def _():
        fetch(0, 0)

    @pl.when(f == 0)
    def _():
        for j in range(MOE_SUBS):
            @pl.when(j < nsub_ref[i])
            def _():
                blk = xbuf.at[slot, pl.ds(j * MOE_SUB * SUBLANES, MOE_SUB * SUBLANES)]
                pltpu.make_async_copy(blk, blk, sem.at[slot, j]).wait()
                for s in range(D_MODEL // LANES):
                    piece = xbuf[slot, pl.ds(j * MOE_SUB * SUBLANES + s, MOE_SUB, stride=SUBLANES), :]
                    xb_ref[sub_rows(j), s * LANES:(s + 1) * LANES] = piece.astype(BF16)
                acc_ref[sub_rows(j), :] = jnp.zeros((MOE_SUB, D_MODEL), F32)

        @pl.when(i + 1 < pl.num_programs(0))
        def _():
            fetch(i + 1, 1 - slot)

    @pl.when(nsub_ref[i] > 0)
    def _():
        wgb[...] = wg_ref[0, 0].astype(BF16)
        wub[...] = wu_ref[0, 0].astype(BF16)
        wdb[...] = wd_ref[0, 0].astype(BF16)
        for j in range(MOE_SUBS):
            @pl.when(j < nsub_ref[i])
            def _():
                xb = xb_ref[sub_rows(j), :]
                act = _silu(_dot(xb, wgb[...])) * _dot(xb, wub[...])
                acc_ref[sub_rows(j), :] += _dot(act.astype(BF16), wdb[...])

    @pl.when(f == MOE_FF_STEPS - 1)
    def _():
        for j in range(MOE_SUBS):
            @pl.when(j < nsub_ref[i])
            def _():
                y_ref[sub_rows(j), :] = acc_ref[sub_rows(j), :] * gs_ref[sub_rows(j), :]

            @pl.when(j >= nsub_ref[i])
            def _():
                y_ref[sub_rows(j), :] = jnp.zeros((MOE_SUB, D_MODEL), F32)


def _moe_experts(h, src, gs, tile_expert, tile_subs, w_gu, w_down, layer):
    rows = src.shape[0]
    tiles = rows // MOE_TILE
    last = MOE_FF_STEPS - 1
    fstep = lambda i, f, ns: jnp.where(ns[i] > 0, f, last)
    grid_spec = pltpu.PrefetchScalarGridSpec(
        num_scalar_prefetch=3,
        grid=(tiles, MOE_FF_STEPS),
        in_specs=[
            pl.BlockSpec(memory_space=pl.ANY),
            pl.BlockSpec((1, 1, D_MODEL, MOE_FF_CHUNK),
                         lambda i, f, te, ns, sr: (layer, te[i], 0, fstep(i, f, ns))),
            pl.BlockSpec((1, 1, D_MODEL, MOE_FF_CHUNK),
                         lambda i, f, te, ns, sr: (layer, te[i], 0, MOE_FF_STEPS + fstep(i, f, ns))),
            pl.BlockSpec((1, 1, MOE_FF_CHUNK, D_MODEL),
                         lambda i, f, te, ns, sr: (layer, te[i], fstep(i, f, ns), 0)),
            pl.BlockSpec((MOE_TILE, 1), lambda i, f, te, ns, sr: (i, 0)),
        ],
        out_specs=pl.BlockSpec((MOE_TILE, D_MODEL), lambda i, f, te, ns, sr: (i, 0)),
        scratch_shapes=[
            pltpu.VMEM((MOE_TILE, D_MODEL), F32),
            pltpu.VMEM((2, MOE_TILE * SUBLANES, LANES), F32),
            pltpu.VMEM((MOE_TILE, D_MODEL), BF16),
            pltpu.VMEM((D_MODEL, MOE_FF_CHUNK), BF16),
            pltpu.VMEM((D_MODEL, MOE_FF_CHUNK), BF16),
            pltpu.VMEM((MOE_FF_CHUNK, D_MODEL), BF16),
            pltpu.SemaphoreType.DMA((2, MOE_SUBS)),
        ],
    )
    return pl.pallas_call(
        _moe_body,
        grid_spec=grid_spec,
        out_shape=jax.ShapeDtypeStruct((rows, D_MODEL), F32),
        compiler_params=_params("arbitrary", "arbitrary"),
        name="moe_experts",
    )(tile_expert, tile_subs, src, h, w_gu, w_gu, w_down, gs)


def _route(idx, gates, n):
    assign = 2 * n
    tiles = -(-(assign + N_EXPERTS * (MOE_TILE - 1)) // MOE_TILE)
    experts = jnp.arange(N_EXPERTS, dtype=jnp.int32)
    ea = idx.reshape(assign)
    ga = gates.reshape(assign)
    onehot = (ea[:, None] == experts[None, :]).astype(jnp.int32)
    csum = jnp.cumsum(onehot, axis=0)
    counts = csum[-1]
    etiles = (counts + MOE_TILE - 1) // MOE_TILE
    tile_end = jnp.cumsum(etiles)
    tile_start = tile_end - etiles
    cnt_start = jnp.cumsum(counts) - counts
    pos = jnp.sum(onehot * ((tile_start * MOE_TILE)[None, :] + csum - 1), axis=1)
    token = jnp.arange(assign, dtype=jnp.int32) % n
    _, sorted_token, sorted_gate = lax.sort((ea, token, ga), num_keys=1, is_stable=True)
    ti = jnp.arange(tiles, dtype=jnp.int32)
    active = ti < tile_end[-1]
    te_raw = jnp.sum((ti[:, None] >= tile_end[None, :]).astype(jnp.int32), axis=1)
    last_expert = jnp.max(jnp.where(etiles > 0, experts, 0))
    te = jnp.where(active, jnp.minimum(te_raw, N_EXPERTS - 1), last_expert)
    tile_onehot = (te[:, None] == experts[None, :]).astype(jnp.int32)
    pick = lambda per_expert: jnp.sum(tile_onehot * per_expert[None, :], axis=1)
    tile_in_expert = ti - pick(tile_start)
    tv = jnp.where(active, jnp.clip(pick(counts) - tile_in_expert * MOE_TILE, 0, MOE_TILE), 0)
    first = pick(cnt_start) + tile_in_expert * MOE_TILE
    r = jnp.arange(MOE_TILE, dtype=jnp.int32)
    valid = (r[None, :] < tv[:, None]).reshape(-1)
    sidx = jnp.clip(first[:, None] + r[None, :], 0, assign - 1).reshape(-1)
    src = jnp.where(valid, jnp.take(sorted_token, sidx, mode="clip"), 0) * SUBLANES
    gs = jnp.where(valid, jnp.take(sorted_gate, sidx, mode="clip"), 0.0)
    return src, gs.reshape(-1, 1), te, (tv + MOE_SUB - 1) // MOE_SUB, pos


def _moe_layer(x, g, router, w_gu, w_down, layer, split=None):
    n = x.shape[0]
    h, idx, gates = _router(x, g, router)
    src, gs, te, subs, pos = _route(idx, gates, n)
    y = _moe_experts(h, src, gs, te, subs, w_gu, w_down, layer)
    y1 = jnp.take(y, pos[:n], axis=0, mode="clip")
    y2 = jnp.take(y, pos[n:], axis=0, mode="clip")
    if split is None:
        return x + (y1 + y2)
    return x[:split] + (y1[:split] + y2[:split]), x[split:] + (y1[split:] + y2[split:])


def kernel(x_prompt, x_sample, state_conv, cache_kv_w128, cache_kv_w512, cache_kv_w2048, norm_mix_g, norm_ffn_g, a_w_in, a_ln_g, a_ln_b, a_w_s, a_b_s, a_w_out, b_w_in, b_conv_w, b_w_out, c_w_qkv, c_q_norm_g, c_k_norm_g, c_w_out, f_w_gu, f_w_down, m_router, m_w_gu, m_w_down):
    batch, seq, _ = x_prompt.shape
    nb = x_sample.shape[0]
    n_prompt = batch * seq
    caches = (cache_kv_w128, cache_kv_w512, cache_kv_w2048)
    x = x_prompt.reshape(n_prompt, D_MODEL)
    x_sample_rows = x_sample.transpose(1, 0, 2).reshape(DEC_SEQ * nb, D_MODEL)
    depth = norm_mix_g.shape[0]
    sgu_v, conv_p, conv_s = [], [], []
    kv_p = [[] for _ in range(N_ATTN_GROUPS)]
    kv_s = [[] for _ in range(N_ATTN_GROUPS)]
    for i in range(depth):
        kind, j = i % 3, i // 3
        if kind == 0:
            x, v_rows = _sgu_layer(x, n_prompt, norm_mix_g[i], a_w_in[j], a_ln_g[j], a_ln_b[j],
                                   a_w_s[j], a_b_s[j], a_w_out[j],
                                   x_sample=x_sample_rows if i == 0 else None)
            sgu_v.append(v_rows.reshape(DEC_SEQ, nb, D_GATE).transpose(1, 0, 2))
        elif kind == 1:
            x, tail_p, tail_s = _conv_layer(x, n_prompt, batch, state_conv[j], norm_mix_g[i],
                                            b_w_in[j], b_conv_w[j], b_w_out[j])
            conv_p.append(tail_p)
            conv_s.append(tail_s)
        else:
            x, kvp, kvs = _attn_layer(x, n_prompt, batch, [c[j] for c in caches], norm_mix_g[i],
                                      c_w_qkv[j], c_q_norm_g[j], c_k_norm_g[j], c_w_out[j])
            for g in range(N_ATTN_GROUPS):
                kv_p[g].append(kvp[g])
                kv_s[g].append(kvs[g])
        e = i // 2
        if i % 2 == 0:
            x = _dense_ffn(x, norm_ffn_g[i], f_w_gu[e], f_w_down[e])
        else:
            x = _moe_layer(x, norm_ffn_g[i], m_router[e], m_w_gu, m_w_down, e,
                           split=n_prompt if i == depth - 1 else None)
    xp, xs = x if isinstance(x, tuple) else (x[:n_prompt], x[n_prompt:])
    y_prompt = xp.reshape(batch, seq, D_MODEL)
    y_sample = xs.reshape(DEC_SEQ, nb, D_MODEL).transpose(1, 0, 2)
    stack = lambda xs: jnp.stack(xs, axis=0)
    return (y_prompt, y_sample, stack(conv_p), stack(conv_s),
            stack(kv_p[0]), stack(kv_s[0]), stack(kv_p[1]), stack(kv_s[1]),
            stack(kv_p[2]), stack(kv_s[2]), stack(sgu_v))
```

```python
import functools

import numpy as np
import jax
import jax.numpy as jnp
from jax import lax
from jax.experimental import pallas as pl
from jax.experimental.pallas import tpu as pltpu

F32 = jnp.float32
BF16 = jnp.bfloat16

D_MODEL = 1024
EPS = 1e-6
CHUNK = 128
D_GATE = 2 * D_MODEL
SGU_GROUPS = 8
SGU_GDIM = D_GATE // SGU_GROUPS
D_CONV = D_MODEL
HEAD_DIM = 64
HEADS_PER_GROUP = 4
GROUP_DIM = HEADS_PER_GROUP * HEAD_DIM
WINDOWS = (128, 512, 2048)
DILATIONS = (1, 4, 16)
N_ATTN_GROUPS = 3
D_ATTN = N_ATTN_GROUPS * GROUP_DIM
ROPE_DIM = HEAD_DIM // 4
ROPE_THETA = 500000.0
ATTN_BLOCK = 128
D_FF = 2816
N_EXPERTS = 8
D_FF_EXPERT = 3584
DEC_SEQ = 4
PAST_LEN = 2048

SUBLANES = 8
LANES = 128
VMEM_LIMIT = 56 * 1024 * 1024

ROW_TILE = 512
FF_CHUNK = D_FF // 2
MOE_TILE = 1024
MOE_SUB = 256
MOE_SUBS = MOE_TILE // MOE_SUB
MOE_FF_CHUNK = 512
MOE_FF_STEPS = D_FF_EXPERT // MOE_FF_CHUNK
assert D_MODEL == SUBLANES * LANES


def _dot(a, b):
    return jnp.dot(a, b, preferred_element_type=F32)


def _dot_nt(a, b):
    return lax.dot_general(a, b, (((1,), (1,)), ((), ())), preferred_element_type=F32)


def _rms(x, g):
    return x * lax.rsqrt(jnp.mean(x * x, axis=-1, keepdims=True) + EPS) * g


def _gelu_tanh(x):
    c = np.float32(np.sqrt(2 / np.pi))
    return x * (0.5 * (1.0 + jnp.tanh(c * (x + np.float32(0.044715) * (x * x * x)))))


def _silu(x):
    return x * (1.0 / (1.0 + jnp.exp(-x)))


def _params(*sem):
    return pltpu.CompilerParams(dimension_semantics=sem, vmem_limit_bytes=VMEM_LIMIT)


def _resident(shape):
    zeros = (0,) * len(shape)
    return pl.BlockSpec(shape, lambda *_: zeros, pipeline_mode=pl.Buffered(1))


def _row_spec(width, first_tile=0):
    return pl.BlockSpec((ROW_TILE, width), lambda i: (i + first_tile, 0))


def _sgu_front(x, gmix, winv_ref, lng, lnb):
    h = _rms(x, gmix).astype(BF16)
    v = _gelu_tanh(_dot(h, winv_ref[...]))
    mu = jnp.mean(v, axis=-1, keepdims=True)
    vc = v - mu
    var = jnp.mean(vc * vc, axis=-1, keepdims=True)
    return h, vc * lax.rsqrt(var + EPS) * lng + lnb


def _sgu_back(x, h, winu_ref, s_ref, wout_ref, o_ref):
    u = _gelu_tanh(_dot(h, winu_ref[...]))
    o_ref[...] = x + _dot((u * s_ref[...]).astype(BF16), wout_ref[...])


def _sgu_prompt_body(x_ref, gmix_ref, winu_ref, winv_ref, lng_ref, lnb_ref, ws_ref, bs_ref,
                     wout_ref, o_ref, s_ref):
    x = x_ref[...]
    h, vn = _sgu_front(x, gmix_ref[...], winv_ref, lng_ref[...], lnb_ref[...])
    vb = vn.astype(BF16)
    row = lax.broadcasted_iota(jnp.int32, (CHUNK, CHUNK), 0)
    col = lax.broadcasted_iota(jnp.int32, (CHUNK, CHUNK), 1)
    for g in range(SGU_GROUPS):
        wm = jnp.where(row >= col, ws_ref[g], 0.0).astype(BF16)
        bias = jnp.concatenate([bs_ref[g]] * (SGU_GDIM // LANES), axis=1)
        cols = slice(g * SGU_GDIM, (g + 1) * SGU_GDIM)
        for c in range(ROW_TILE // CHUNK):
            rows = slice(c * CHUNK, (c + 1) * CHUNK)
            s_ref[rows, cols] = _dot(wm, vb[rows, cols]) + bias
    _sgu_back(x, h, winu_ref, s_ref, wout_ref, o_ref)


def _sgu_prompt_staging_body(x_ref, xs_ref, *rest):
    o_ref = rest[-2]
    last = pl.program_id(0) == pl.num_programs(0) - 1

    @pl.when(jnp.logical_not(last))
    def _():
        _sgu_prompt_body(x_ref, *rest)

    @pl.when(last)
    def _():
        o_ref[...] = xs_ref[...]


def _sgu_sample_body(w4_ref, b4_ref, x_ref, gmix_ref, winu_ref, winv_ref, lng_ref, lnb_ref,
                     wout_ref, o_ref, v_ref, s_ref):
    x = x_ref[...]
    nb = x.shape[0] // DEC_SEQ
    h, vn = _sgu_front(x, gmix_ref[...], winv_ref, lng_ref[...], lnb_ref[...])
    v_ref[...] = vn
    for g in range(SGU_GROUPS):
        cols = slice(g * SGU_GDIM, (g + 1) * SGU_GDIM)
        for t in range(DEC_SEQ):
            acc = w4_ref[g, t * DEC_SEQ] * vn[0:nb, cols]
            for j in range(1, t + 1):
                acc = acc + w4_ref[g, t * DEC_SEQ + j] * vn[j * nb:(j + 1) * nb, cols]
            s_ref[t * nb:(t + 1) * nb, cols] = acc + b4_ref[g, t]
    _sgu_back(x, h, winu_ref, s_ref, wout_ref, o_ref)


def _sgu_layer(x, n_prompt, gmix, w_in, ln_g, ln_b, w_s, b_s, w_out, x_sample=None):
    n = x.shape[0] if x_sample is None else x.shape[0] + x_sample.shape[0]
    n_sample = n - n_prompt
    assert n_sample == ROW_TILE and n_prompt % ROW_TILE == 0
    winu = w_in[:, :D_GATE].astype(BF16)
    winv = w_in[:, D_GATE:].astype(BF16)
    wout = w_out.astype(BF16)
    gmix = gmix.reshape(1, D_MODEL)
    lng = ln_g.reshape(1, D_GATE)
    lnb = ln_b.reshape(1, D_GATE)
    bs = jnp.broadcast_to(b_s[:, :, None], (SGU_GROUPS, CHUNK, LANES))
    weights = [_resident((1, D_MODEL)), _resident((D_MODEL, D_GATE)), _resident((D_MODEL, D_GATE)),
               _resident((1, D_GATE)), _resident((1, D_GATE))]
    prompt_tiles = n_prompt // ROW_TILE
    prompt_specs = weights + [
        _resident((SGU_GROUPS, CHUNK, CHUNK)), _resident((SGU_GROUPS, CHUNK, LANES)),
        _resident((D_GATE, D_MODEL))]
    prompt_args = (gmix, winu, winv, lng, lnb, w_s, bs, wout)
    if x_sample is None:
        body, grid, aliases = _sgu_prompt_body, prompt_tiles, {0: 0}
        prompt_specs = [_row_spec(D_MODEL)] + prompt_specs
        prompt_args = (x,) + prompt_args
    else:
        body, grid, aliases = _sgu_prompt_staging_body, prompt_tiles + 1, {}
        prompt_specs = [
            pl.BlockSpec((ROW_TILE, D_MODEL), lambda i: (jnp.minimum(i, prompt_tiles - 1), 0)),
            _resident((ROW_TILE, D_MODEL))] + prompt_specs
        prompt_args = (x, x_sample) + prompt_args
    x = pl.pallas_call(
        body,
        grid=(grid,),
        in_specs=prompt_specs,
        out_specs=_row_spec(D_MODEL),
        out_shape=jax.ShapeDtypeStruct((n, D_MODEL), F32),
        scratch_shapes=[pltpu.VMEM((ROW_TILE, D_GATE), F32)],
        input_output_aliases=aliases,
        compiler_params=_params("parallel"),
        name="sgu_prompt",
    )(*prompt_args)
    w4 = w_s[:, :DEC_SEQ, :DEC_SEQ].reshape(SGU_GROUPS, DEC_SEQ * DEC_SEQ)
    b4 = b_s[:, :DEC_SEQ]
    smem = pl.BlockSpec(memory_space=pltpu.SMEM)
    first = n_prompt // ROW_TILE
    x, v_rows = pl.pallas_call(
        _sgu_sample_body,
        grid=(1,),
        in_specs=[smem, smem, _row_spec(D_MODEL, first)] + weights + [_resident((D_GATE, D_MODEL))],
        out_specs=[_row_spec(D_MODEL, first), pl.BlockSpec((ROW_TILE, D_GATE), lambda i: (0, 0))],
        out_shape=[jax.ShapeDtypeStruct((n, D_MODEL), F32),
                   jax.ShapeDtypeStruct((n_sample, D_GATE), F32)],
        scratch_shapes=[pltpu.VMEM((ROW_TILE, D_GATE), F32)],
        input_output_aliases={2: 0},
        compiler_params=_params("arbitrary"),
        name="sgu_sample",
    )(w4, b4, x, gmix, winu, winv, lng, lnb, wout)
    return x, v_rows


def _conv_front(x, gmix, win_ref):
    h = _rms(x, gmix).astype(BF16)
    gate_b = _dot(h, win_ref[:, 0:D_CONV])
    gate_c = _dot(h, win_ref[:, D_CONV:2 * D_CONV])
    xi = _dot(h, win_ref[:, 2 * D_CONV:3 * D_CONV])
    return gate_b, gate_c * xi


def _conv_prompt_body(x_ref, gmix_ref, win_ref, cw_ref, wout_ref, o_ref, tail_ref, zbuf, *,
                      tiles_per_seq):
    @pl.when(pl.program_id(0) % tiles_per_seq == 0)
    def _():
        zbuf[0:SUBLANES, :] = jnp.zeros((SUBLANES, D_CONV), F32)

    x = x_ref[...]
    gate_b, z = _conv_front(x, gmix_ref[...], win_ref)
    zbuf[SUBLANES:SUBLANES + ROW_TILE, :] = z
    conv = (cw_ref[2:3, :] * z
            + cw_ref[1:2, :] * zbuf[SUBLANES - 1:SUBLANES - 1 + ROW_TILE, :]
            + cw_ref[0:1, :] * zbuf[SUBLANES - 2:SUBLANES - 2 + ROW_TILE, :])
    o_ref[...] = x + _dot((gate_b * conv).astype(BF16), wout_ref[...])
    tail = zbuf[ROW_TILE + SUBLANES - 2:ROW_TILE + SUBLANES, :]
    tail_ref[0] = tail
    zbuf[SUBLANES - 2:SUBLANES, :] = tail


def _conv_sample_body(x_ref, gmix_ref, win_ref, cw_ref, st_ref, wout_ref, o_ref, tail_ref):
    x = x_ref[...]
    nb = x.shape[0] // DEC_SEQ
    gate_b, z = _conv_front(x, gmix_ref[...], win_ref)
    z1 = jnp.concatenate([st_ref[1], z[:(DEC_SEQ - 1) * nb]], axis=0)
    z2 = jnp.concatenate([st_ref[0], st_ref[1], z[:(DEC_SEQ - 2) * nb]], axis=0)
    conv = cw_ref[2:3, :] * z + cw_ref[1:2, :] * z1 + cw_ref[0:1, :] * z2
    o_ref[...] = x + _dot((gate_b * conv).astype(BF16), wout_ref[...])
    tail_ref[...] = z[(DEC_SEQ - 2) * nb:]


def _conv_layer(x, n_prompt, batch, state, gmix, w_in, conv_w, w_out):
    n = x.shape[0]
    n_sample = n - n_prompt
    nb = n_sample // DEC_SEQ
    assert n_sample == ROW_TILE
    tiles_per_seq = n_prompt // batch // ROW_TILE
    win = w_in.astype(BF16)
    wout = w_out.astype(BF16)
    gmix = gmix.reshape(1, D_MODEL)
    weights = [_resident((1, D_MODEL)), _resident((D_MODEL, 3 * D_CONV)), _resident((3, D_CONV))]
    x, tail_p = pl.pallas_call(
        functools.partial(_conv_prompt_body, tiles_per_seq=tiles_per_seq),
        grid=(n_prompt // ROW_TILE,),
        in_specs=[_row_spec(D_MODEL)] + weights + [_resident((D_CONV, D_MODEL))],
        out_specs=[_row_spec(D_MODEL),
                   pl.BlockSpec((1, 2, D_CONV), lambda i: (i // tiles_per_seq, 0, 0))],
        out_shape=[jax.ShapeDtypeStruct((n, D_MODEL), F32),
                   jax.ShapeDtypeStruct((batch, 2, D_CONV), F32)],
        scratch_shapes=[pltpu.VMEM((ROW_TILE + SUBLANES, D_CONV), F32)],
        input_output_aliases={0: 0},
        compiler_params=_params("arbitrary"),
        name="conv_prompt",
    )(x, gmix, win, conv_w, wout)
    first = n_prompt // ROW_TILE
    st = state.transpose(1, 0, 2)
    x, tail_s = pl.pallas_call(
        _conv_sample_body,
        grid=(1,),
        in_specs=[_row_spec(D_MODEL, first)] + weights + [
            _resident((2, nb, D_CONV)), _resident((D_CONV, D_MODEL))],
        out_specs=[_row_spec(D_MODEL, first), pl.BlockSpec((2 * nb, D_CONV), lambda i: (0, 0))],
        out_shape=[jax.ShapeDtypeStruct((n, D_MODEL), F32),
                   jax.ShapeDtypeStruct((2 * nb, D_CONV), F32)],
        input_output_aliases={0: 0},
        compiler_params=_params("arbitrary"),
        name="conv_sample",
    )(x, gmix, win, conv_w, st, wout)
    return x, tail_p, tail_s.reshape(2, nb, D_CONV).transpose(1, 0, 2)


def _rope_tables(seq, n_sample_rows):
    nb = n_sample_rows // DEC_SEQ
    pos = jnp.concatenate([jnp.arange(seq, dtype=jnp.int32),
                           PAST_LEN + jnp.arange(n_sample_rows, dtype=jnp.int32) // nb])
    inv_freq = jnp.power(ROPE_THETA, -jnp.arange(0, ROPE_DIM, 2, dtype=F32) / ROPE_DIM)
    d = np.arange(LANES) % HEAD_DIM
    half = ROPE_DIM // 2
    ang = pos.astype(F32)[:, None] * inv_freq[d % half][None, :]
    cos, sin = jnp.cos(ang), jnp.sin(ang)
    c = jnp.where(d[None, :] < ROPE_DIM, cos, 1.0)
    s1 = jnp.where(d[None, :] < half, -sin, 0.0)
    s2 = jnp.where((d[None, :] >= half) & (d[None, :] < ROPE_DIM), sin, 0.0)
    return c, s1, s2


def _qkv_body(x_ref, gmix_ref, w_ref, gq_ref, gk_ref, c_ref, s1_ref, s2_ref, bm_ref,
              q_ref, k_ref, v_ref):
    h = _rms(x_ref[...], gmix_ref[...]).astype(BF16)
    c, s1, s2 = c_ref[...], s1_ref[...], s2_ref[...]
    bm = bm_ref[...]

    def head_norm_rope(t, gain, out_ref, scale):
        for a in range(D_ATTN // GROUP_DIM):
            ts = t[:, a * GROUP_DIM:(a + 1) * GROUP_DIM]
            sq = ts * ts
            hi = sq.astype(BF16)
            lo = (sq - hi.astype(F32)).astype(BF16)
            ms = _dot(hi, bm) + _dot(lo, bm)
            tn = ts * lax.rsqrt(ms + EPS) * gain[:, a * GROUP_DIM:(a + 1) * GROUP_DIM]
            for b in range(GROUP_DIM // LANES):
                th = tn[:, b * LANES:(b + 1) * LANES]
                rot = (th * c + pltpu.roll(th, LANES - ROPE_DIM // 2, 1) * s1
                       + pltpu.roll(th, ROPE_DIM // 2, 1) * s2)
                out_ref[a * (GROUP_DIM // LANES) + b] = rot if scale is None else rot * scale

    head_norm_rope(_dot(h, w_ref[:, 0:D_ATTN]), gq_ref[...], q_ref, np.float32(HEAD_DIM ** -0.5))
    head_norm_rope(_dot(h, w_ref[:, D_ATTN:2 * D_ATTN]), gk_ref[...], k_ref, None)
    v = _dot(h, w_ref[:, 2 * D_ATTN:3 * D_ATTN])
    for s in range(D_ATTN // LANES):
        v_ref[s] = v[:, s * LANES:(s + 1) * LANES]


def _qkv(x, n_prompt, seq, gmix, w_qkv, q_norm_g, k_norm_g):
    n = x.shape[0]
    n_sample = n - n_prompt
    tiles_per_seq = seq // ROW_TILE
    prompt_tiles = n_prompt // ROW_TILE
    c, s1, s2 = _rope_tables(seq, n_sample)
    heads = D_ATTN // HEAD_DIM
    slabs = D_ATTN // LANES
    gq = jnp.tile(q_norm_g, heads).reshape(1, D_ATTN)
    gk = jnp.tile(k_norm_g, heads).reshape(1, D_ATTN)
    lane_head = np.arange(GROUP_DIM) // HEAD_DIM
    bm = jnp.asarray((lane_head[:, None] == lane_head[None, :]) / HEAD_DIM, BF16)
    table = pl.BlockSpec(
        (ROW_TILE, LANES),
        lambda i: (jnp.where(i < prompt_tiles, i % tiles_per_seq, tiles_per_seq), 0))
    return pl.pallas_call(
        _qkv_body,
        grid=(n // ROW_TILE,),
        in_specs=[_row_spec(D_MODEL), _resident((1, D_MODEL)), _resident((D_MODEL, 3 * D_ATTN)),
                  _resident((1, D_ATTN)), _resident((1, D_ATTN)), table, table, table,
                  _resident((GROUP_DIM, GROUP_DIM))],
        out_specs=[pl.BlockSpec((slabs, ROW_TILE, LANES), lambda i: (0, i, 0))] * 3,
        out_shape=[jax.ShapeDtypeStruct((slabs, n, LANES), F32)] * 3,
        compiler_params=_params("parallel"),
        name="qkv",
    )(x, gmix.reshape(1, D_MODEL), w_qkv.astype(BF16), gq, gk, c, s1, s2, bm)


GROUP_SLABS = GROUP_DIM // LANES


def _attn_prompt_body(q_ref, kc_ref, kp_ref, vc_ref, vp_ref, o_ref, l_ref, *, dil, blocks):
    first_key = jnp.where(pl.program_id(1) == 0, ATTN_BLOCK, 0)
    lane = lax.broadcasted_iota(jnp.int32, (1, GROUP_DIM), 1)
    head_masks = [lane // HEAD_DIM == hd for hd in range(HEADS_PER_GROUP)]
    qi = lax.broadcasted_iota(jnp.int32, (ATTN_BLOCK, 2 * ATTN_BLOCK), 0)
    kj = lax.broadcasted_iota(jnp.int32, (ATTN_BLOCK, 2 * ATTN_BLOCK), 1)
    dist = qi + ATTN_BLOCK - kj
    band = (dist >= 0) & (dist <= ATTN_BLOCK)
    band_first = band & (kj >= first_key)

    def stream_rows(r, j):
        if dil == 1:
            return pl.ds(j * ATTN_BLOCK, ATTN_BLOCK)
        return pl.ds(r + j * ATTN_BLOCK * dil, ATTN_BLOCK, stride=dil)

    def load(ref, rows):
        return jnp.concatenate([ref[s, rows, :] for s in range(GROUP_SLABS)], axis=1)

    for r in range(dil):
        for j in range(blocks):
            rows = stream_rows(r, j)
            q = load(q_ref, rows).astype(BF16)
            if j == 0:
                k_prev, v_prev, mask = load(kp_ref, stream_rows(r, 0)), load(vp_ref, stream_rows(r, 0)), band_first
            else:
                prev = stream_rows(r, j - 1)
                k_prev, v_prev, mask = load(kc_ref, prev), load(vc_ref, prev), band
            kk = jnp.concatenate([k_prev, load(kc_ref, rows)], axis=0).astype(BF16)
            vv = jnp.concatenate([v_prev, load(vc_ref, rows)], axis=0).astype(BF16)
            q_heads = jnp.concatenate([jnp.where(hm, q, jnp.zeros_like(q)) for hm in head_masks], axis=0)
            s = jnp.where(jnp.concatenate([mask] * HEADS_PER_GROUP, axis=0), _dot_nt(q_heads, kk), -jnp.inf)
            m = jnp.max(s, axis=-1, keepdims=True)
            p = jnp.exp(s - m)
            l = jnp.sum(p, axis=-1, keepdims=True)
            pv = _dot(p.astype(BF16), vv) * (1.0 / l)
            lse = m + jnp.log(l)
            o_acc = jnp.zeros((ATTN_BLOCK, GROUP_DIM), F32)
            l_acc = jnp.zeros((ATTN_BLOCK, GROUP_DIM), F32)
            for hd, hm in enumerate(head_masks):
                part = slice(hd * ATTN_BLOCK, (hd + 1) * ATTN_BLOCK)
                o_acc = o_acc + jnp.where(hm, pv[part], 0.0)
                l_acc = l_acc + jnp.where(hm, lse[part], 0.0)
            for s in range(GROUP_SLABS):
                o_ref[s, rows, :] = o_acc[:, s * LANES:(s + 1) * LANES]
                l_ref[s, rows, :] = l_acc[:, s * LANES:(s + 1) * LANES]


def _attn_prompt(q, k, v, group, batch, seq):
    dil = DILATIONS[group]
    assert WINDOWS[group] // dil == ATTN_BLOCK
    span = ATTN_BLOCK * dil
    tile = max(ROW_TILE, span)
    tiles = seq // tile
    cur = pl.BlockSpec((GROUP_SLABS, tile, LANES), lambda b, t: (group, b * tiles + t, 0))
    prev = pl.BlockSpec(
        (GROUP_SLABS, span, LANES),
        lambda b, t: (group, jnp.maximum((b * tiles + t) * (tile // span) - 1, 0), 0))
    out = pl.BlockSpec((GROUP_SLABS, tile, LANES), lambda b, t: (0, b * tiles + t, 0))
    shape = jax.ShapeDtypeStruct((GROUP_SLABS, batch * seq, LANES), F32)
    return pl.pallas_call(
        functools.partial(_attn_prompt_body, dil=dil, blocks=tile // span),
        grid=(batch, tiles),
        in_specs=[cur, cur, prev, cur, prev],
        out_specs=[out, out],
        out_shape=[shape, shape],
        compiler_params=_params("parallel", "arbitrary"),
        name=f"attn_prompt_g{group}",
    )(q, k, k, v, v)


HEADS_PER_SLAB = LANES // HEAD_DIM


def _attn_sample_body(q_ref, k_ref, v_ref, c_ref, o_ref, lse_ref, *, dil, buf, seqs):
    rows = HEADS_PER_SLAB * SUBLANES
    lane = lax.broadcasted_iota(jnp.int32, (1, LANES), 1)
    head_masks = [lane // HEAD_DIM == hh for hh in range(HEADS_PER_SLAB)]

    def visibility(width):
        t = lax.broadcasted_iota(jnp.int32, (rows, width), 0) % SUBLANES
        c = lax.broadcasted_iota(jnp.int32, (rows, width), 1)
        return jnp.where(t < DEC_SEQ, t, t - DEC_SEQ), c

    tq, c = visibility(buf)
    vis_cache = (c >= tq) if dil == 1 else (c % dil == tq)
    tq, c = visibility(SUBLANES)
    vis_fresh = (c < DEC_SEQ) & ((c <= tq) if dil == 1 else (c == tq))

    def one(bi):
        for s in range(GROUP_SLABS):
            heads = slice(s * HEADS_PER_SLAB, (s + 1) * HEADS_PER_SLAB)
            q, k_new, v_new = q_ref[s, bi], k_ref[s, bi].astype(BF16), v_ref[s, bi]
            q_heads = jnp.concatenate([jnp.where(hm, q, 0.0) for hm in head_masks], axis=0).astype(BF16)
            k_cache = c_ref[bi, 0, heads].reshape(LANES, buf).astype(BF16)
            v_cache = c_ref[bi, 1, heads].reshape(LANES, buf).astype(BF16)
            s_cache = jnp.where(vis_cache, _dot(q_heads, k_cache), -jnp.inf)
            s_fresh = jnp.where(vis_fresh, _dot_nt(q_heads, k_new), -jnp.inf)
            m = jnp.maximum(jnp.max(s_cache, axis=1, keepdims=True), jnp.max(s_fresh, axis=1, keepdims=True))
            p_cache = jnp.exp(s_cache - m)
            p_fresh = jnp.exp(s_fresh - m)
            l = jnp.sum(p_cache, axis=1, keepdims=True) + jnp.sum(p_fresh, axis=1, keepdims=True)
            inv = 1.0 / l
            o = _dot_nt((p_cache * inv).astype(BF16), v_cache)
            w_fresh = p_fresh * inv
            for t2 in range(DEC_SEQ):
                o = o + w_fresh[:, t2:t2 + 1] * v_new[t2:t2 + 1, :]
            lse = jnp.broadcast_to(m + jnp.log(l), (rows, LANES))
            o_slab = jnp.zeros((SUBLANES, LANES), F32)
            lse_slab = jnp.zeros((SUBLANES, LANES), F32)
            for hh, hm in enumerate(head_masks):
                part = slice(hh * SUBLANES, (hh + 1) * SUBLANES)
                o_slab = o_slab + jnp.where(hm, o[part], 0.0)
                lse_slab = lse_slab + jnp.where(hm, lse[part], 0.0)
            o_ref[s, bi] = o_slab
            lse_ref[s, bi] = lse_slab

    def pair(i, carry):
        one(2 * i)
        one(2 * i + 1)
        return carry

    if seqs == 2:
        pair(0, 0)
    else:
        lax.fori_loop(0, seqs // 2, pair, 0)


def _sample_rows(slabs, n_prompt):
    part = slabs[:, n_prompt:]
    nb = part.shape[1] // DEC_SEQ
    part = part.reshape(part.shape[0], DEC_SEQ, nb, LANES).transpose(0, 2, 1, 3)
    return jnp.pad(part, ((0, 0), (0, 0), (0, SUBLANES - DEC_SEQ), (0, 0)))


def _attn_sample(q, k, v, cache, group):
    nb, buf = cache.shape[0], cache.shape[1]
    dil = DILATIONS[group]
    assert buf == ATTN_BLOCK * dil and WINDOWS[group] == buf and (dil == 1 or dil >= DEC_SEQ)
    seqs = max(2, 2 * PAST_LEN // buf)
    fresh = pl.BlockSpec((GROUP_SLABS, seqs, SUBLANES, LANES), lambda i: (group, i, 0, 0))
    out = pl.BlockSpec((GROUP_SLABS, seqs, SUBLANES, LANES), lambda i: (0, i, 0, 0))
    shape = jax.ShapeDtypeStruct((GROUP_SLABS, nb, SUBLANES, LANES), F32)
    o, lse = pl.pallas_call(
        functools.partial(_attn_sample_body, dil=dil, buf=buf, seqs=seqs),
        grid=(nb // seqs,),
        in_specs=[fresh, fresh, fresh,
                  pl.BlockSpec((seqs, 2, HEADS_PER_GROUP, HEAD_DIM, buf), lambda i: (i, 0, 0, 0, 0))],
        out_specs=[out, out],
        out_shape=[shape, shape],
        compiler_params=_params("parallel"),
        name=f"attn_sample_g{group}",
    )(q, k, v, cache.transpose(0, 2, 3, 4, 1))
    rows = lambda a: a[:, :, :DEC_SEQ].transpose(0, 2, 1, 3).reshape(GROUP_SLABS, DEC_SEQ * nb, LANES)
    return rows(o), rows(lse)


def _from_slabs(slabs, group):
    part = slabs[group * GROUP_SLABS:(group + 1) * GROUP_SLABS]
    return part.transpose(1, 0, 2).reshape(part.shape[1], GROUP_DIM)


def _merge_body(x_ref, o0_ref, o1_ref, o2_ref, l0_ref, l1_ref, l2_ref, w_ref, out_ref):
    rows = lambda ref: jnp.concatenate([ref[s] for s in range(GROUP_SLABS)], axis=1)
    l0, l1, l2 = rows(l0_ref), rows(l1_ref), rows(l2_ref)
    m = jnp.maximum(jnp.maximum(l0, l1), l2)
    e0, e1, e2 = jnp.exp(l0 - m), jnp.exp(l1 - m), jnp.exp(l2 - m)
    inv = 1.0 / (e0 + e1 + e2)
    y = x_ref[...]
    for g, (o_ref, e) in enumerate(((o0_ref, e0), (o1_ref, e1), (o2_ref, e2))):
        y = y + _dot((rows(o_ref) * (e * inv)).astype(BF16),
                     w_ref[g * GROUP_DIM:(g + 1) * GROUP_DIM, :])
    out_ref[...] = y


def _merge_out_proj(x, outs, lses, w_out, first_tile, tiles):
    n = x.shape[0]
    part = pl.BlockSpec((GROUP_SLABS, ROW_TILE, LANES), lambda i: (0, i, 0))
    return pl.pallas_call(
        _merge_body,
        grid=(tiles,),
        in_specs=[_row_spec(D_MODEL, first_tile)] + [part] * 6 + [_resident((D_ATTN, D_MODEL))],
        out_specs=_row_spec(D_MODEL, first_tile),
        out_shape=jax.ShapeDtypeStruct((n, D_MODEL), F32),
        input_output_aliases={0: 0},
        compiler_params=_params("parallel"),
        name="attn_merge_out_proj",
    )(x, *outs, *lses, w_out)


def _attn_layer(x, n_prompt, batch, caches, gmix, w_qkv, q_norm_g, k_norm_g, w_out):
    n = x.shape[0]
    seq = n_prompt // batch
    nb = (n - n_prompt) // DEC_SEQ
    q, k, v = _qkv(x, n_prompt, seq, gmix, w_qkv, q_norm_g, k_norm_g)
    ks, vs = k[:, n_prompt:], v[:, n_prompt:]
    q_new, k_new, v_new = (_sample_rows(a, n_prompt) for a in (q, k, v))
    outs_p, lses_p, outs_s, lses_s, kv_p, kv_s = [], [], [], [], [], []
    for g in range(N_ATTN_GROUPS):
        o, lse = _attn_prompt(q, k, v, g, batch, seq)
        outs_p.append(o)
        lses_p.append(lse)
        o, lse = _attn_sample(q_new, k_new, v_new, caches[g], g)
        outs_s.append(o)
        lses_s.append(lse)
        k_rows, v_rows = _from_slabs(ks, g), _from_slabs(vs, g)
        keep = min(WINDOWS[g], seq)
        heads = (HEADS_PER_GROUP, HEAD_DIM)
        tail = lambda a: _from_slabs(jnp.concatenate(
            [a[g * GROUP_SLABS:(g + 1) * GROUP_SLABS, (b + 1) * seq - keep:(b + 1) * seq]
             for b in range(batch)], axis=1), 0)
        kv_p.append(jnp.stack([tail(k), tail(v)], axis=1).reshape(batch, keep, 2, *heads))
        kv_s.append(jnp.stack([k_rows, v_rows], axis=1).reshape(DEC_SEQ, nb, 2, *heads)
                    .transpose(1, 0, 2, 3, 4))
    wout = w_out.astype(BF16)
    x = _merge_out_proj(x, outs_p, lses_p, wout, 0, n_prompt // ROW_TILE)
    x = _merge_out_proj(x, outs_s, lses_s, wout, n_prompt // ROW_TILE, (n - n_prompt) // ROW_TILE)
    return x, kv_p, kv_s


def _ffn_body(x_ref, g_ref, wg_ref, wu_ref, wd_ref, o_ref):
    x = x_ref[...]
    h = _rms(x, g_ref[...]).astype(BF16)
    y = x
    for c in range(D_FF // FF_CHUNK):
        cols = slice(c * FF_CHUNK, (c + 1) * FF_CHUNK)
        act = _silu(_dot(h, wg_ref[:, cols])) * _dot(h, wu_ref[:, cols])
        y = y + _dot(act.astype(BF16), wd_ref[cols, :])
    o_ref[...] = y


def _dense_ffn(x, g, w_gu, w_down):
    n = x.shape[0]
    wgu = w_gu.astype(BF16)
    half = pl.Buffered(1)
    return pl.pallas_call(
        _ffn_body,
        grid=(n // ROW_TILE,),
        in_specs=[_row_spec(D_MODEL), _resident((1, D_MODEL)),
                  pl.BlockSpec((D_MODEL, D_FF), lambda i: (0, 0), pipeline_mode=half),
                  pl.BlockSpec((D_MODEL, D_FF), lambda i: (0, 1), pipeline_mode=half),
                  _resident((D_FF, D_MODEL))],
        out_specs=_row_spec(D_MODEL),
        out_shape=jax.ShapeDtypeStruct((n, D_MODEL), F32),
        input_output_aliases={0: 0},
        compiler_params=_params("parallel"),
        name="dense_ffn",
    )(x, g.reshape(1, D_MODEL), wgu, wgu, w_down.astype(BF16))


def _router_body(x_ref, g_ref, r_ref, h_ref, idx_ref, gate_ref):
    h = _rms(x_ref[...], g_ref[...])
    for s in range(D_MODEL // LANES):
        h_ref[pl.ds(s, ROW_TILE, stride=SUBLANES), :] = h[:, s * LANES:(s + 1) * LANES]
    logits = jnp.dot(h, r_ref[...], precision=lax.Precision.HIGHEST, preferred_element_type=F32)
    lt = logits.T[:N_EXPERTS, :]
    e = lax.broadcasted_iota(jnp.int32, lt.shape, 0)
    m1 = jnp.max(lt, axis=0, keepdims=True)
    e1 = jnp.min(jnp.where(lt == m1, e, N_EXPERTS), axis=0, keepdims=True)
    rest = jnp.where(e == e1, -jnp.inf, lt)
    m2 = jnp.max(rest, axis=0, keepdims=True)
    e2 = jnp.min(jnp.where(rest == m2, e, N_EXPERTS), axis=0, keepdims=True)
    a = jnp.exp(m2 - m1)
    inv = 1.0 / (1.0 + a)
    idx_ref[...] = jnp.concatenate([e1, e2], axis=0)
    gate_ref[...] = jnp.concatenate([inv, a * inv], axis=0)


def _router(x, g, router):
    n = x.shape[0]
    r = jnp.pad(router, ((0, 0), (0, LANES - N_EXPERTS)))
    pair = pl.BlockSpec((2, ROW_TILE), lambda i: (0, i))
    return pl.pallas_call(
        _router_body,
        grid=(n // ROW_TILE,),
        in_specs=[_row_spec(D_MODEL), _resident((1, D_MODEL)), _resident((D_MODEL, LANES))],
        out_specs=[pl.BlockSpec((ROW_TILE * SUBLANES, LANES), lambda i: (i, 0)), pair, pair],
        out_shape=[jax.ShapeDtypeStruct((n * SUBLANES, LANES), F32),
                   jax.ShapeDtypeStruct((2, n), jnp.int32),
                   jax.ShapeDtypeStruct((2, n), F32)],
        compiler_params=_params("parallel"),
        name="moe_router",
    )(x, g.reshape(1, D_MODEL), r)


def _moe_body(te_ref, nsub_ref, quota_ref, nnext_ref, src_ref, h_hbm, wg_ref, wu_ref, wd_ref, gs_ref,
              y_ref, acc_ref, xbuf, xb_ref, wgb, wub, wdb, sem, issued):
    i, f = pl.program_id(0), pl.program_id(1)
    slot = i % 2
    del te_ref

    def sub_rows(j):
        return pl.ds(j * MOE_SUB, MOE_SUB)

    def fetch_rows(tile, slot, lo, hi):
        def row(r, carry):
            tok = pl.multiple_of(src_ref[tile * MOE_TILE + r], SUBLANES)
            pltpu.make_async_copy(h_hbm.at[pl.ds(tok, SUBLANES)],
                                  xbuf.at[slot, pl.ds(r * SUBLANES, SUBLANES)],
                                  sem.at[slot, r // MOE_SUB]).start()
            return carry
        lax.fori_loop(lo, hi, row, 0)

    @pl.when((i == 0) & (f == 0))
    def _():
        fetch_rows(0, 0, 0, nsub_ref[0] * MOE_SUB)

    @pl.when(f == 0)
    def _():
        issued[0] = 0
        for j in range(MOE_SUBS):
            @pl.when(j < nsub_ref[i])
            def _():
                blk = xbuf.at[slot, pl.ds(j * MOE_SUB * SUBLANES, MOE_SUB * SUBLANES)]
                pltpu.make_async_copy(blk, blk, sem.at[slot, j]).wait()
                for s in range(D_MODEL // LANES):
                    piece = xbuf[slot, pl.ds(j * MOE_SUB * SUBLANES + s, MOE_SUB, stride=SUBLANES), :]
                    xb_ref[sub_rows(j), s * LANES:(s + 1) * LANES] = piece.astype(BF16)
                acc_ref[sub_rows(j), :] = jnp.zeros((MOE_SUB, D_MODEL), F32)

    @pl.when(nsub_ref[i] > 0)
    def _():
        wgb[...] = wg_ref[0, 0].astype(BF16)
        wub[...] = wu_ref[0, 0].astype(BF16)
        wdb[...] = wd_ref[0, 0].astype(BF16)
        for j in range(MOE_SUBS):
            @pl.when(j < nsub_ref[i])
            def _():
                xb = xb_ref[sub_rows(j), :]
                act = _silu(_dot(xb, wgb[...])) * _dot(xb, wub[...])
                acc_ref[sub_rows(j), :] += _dot(act.astype(BF16), wdb[...])
                lo = issued[0]
                hi = jnp.minimum(lo + quota_ref[i], nnext_ref[i])
                fetch_rows(i + 1, 1 - slot, lo, hi)
                issued[0] = hi

    @pl.when(f == MOE_FF_STEPS - 1)
    def _():
        for j in range(MOE_SUBS):
            @pl.when(j < nsub_ref[i])
            def _():
                y_ref[sub_rows(j), :] = acc_ref[sub_rows(j), :] * gs_ref[sub_rows(j), :]

            @pl.when(j >= nsub_ref[i])
            def _():
                y_ref[sub_rows(j), :] = jnp.zeros((MOE_SUB, D_MODEL), F32)


def _moe_experts(h, src, gs, tile_expert, tile_subs, w_gu, w_down, layer):
    rows = src.shape[0]
    tiles = rows // MOE_TILE
    last = MOE_FF_STEPS - 1
    fstep = lambda i, f, ns: jnp.where(ns[i] > 0, f, last)
    next_rows = jnp.concatenate([tile_subs[1:], jnp.zeros((1,), jnp.int32)]) * MOE_SUB
    blocks = jnp.maximum(tile_subs * MOE_FF_STEPS, 1)
    quota = jnp.where(tile_subs > 0, (next_rows + blocks - 1) // blocks, 0)
    grid_spec = pltpu.PrefetchScalarGridSpec(
        num_scalar_prefetch=5,
        grid=(tiles, MOE_FF_STEPS),
        in_specs=[
            pl.BlockSpec(memory_space=pl.ANY),
            pl.BlockSpec((1, 1, D_MODEL, MOE_FF_CHUNK),
                         lambda i, f, te, ns, *_: (layer, te[i], 0, fstep(i, f, ns))),
            pl.BlockSpec((1, 1, D_MODEL, MOE_FF_CHUNK),
                         lambda i, f, te, ns, *_: (layer, te[i], 0, MOE_FF_STEPS + fstep(i, f, ns))),
            pl.BlockSpec((1, 1, MOE_FF_CHUNK, D_MODEL),
                         lambda i, f, te, ns, *_: (layer, te[i], fstep(i, f, ns), 0)),
            pl.BlockSpec((MOE_TILE, 1), lambda i, f, *_: (i, 0)),
        ],
        out_specs=pl.BlockSpec((MOE_TILE, D_MODEL), lambda i, f, *_: (i, 0)),
        scratch_shapes=[
            pltpu.VMEM((MOE_TILE, D_MODEL), F32),
            pltpu.VMEM((2, MOE_TILE * SUBLANES, LANES), F32),
            pltpu.VMEM((MOE_TILE, D_MODEL), BF16),
            pltpu.VMEM((D_MODEL, MOE_FF_CHUNK), BF16),
            pltpu.VMEM((D_MODEL, MOE_FF_CHUNK), BF16),
            pltpu.VMEM((MOE_FF_CHUNK, D_MODEL), BF16),
            pltpu.SemaphoreType.DMA((2, MOE_SUBS)),
            pltpu.SMEM((1,), jnp.int32),
        ],
    )
    return pl.pallas_call(
        _moe_body,
        grid_spec=grid_spec,
        out_shape=jax.ShapeDtypeStruct((rows, D_MODEL), F32),
        compiler_params=_params("arbitrary", "arbitrary"),
        name="moe_experts",
    )(tile_expert, tile_subs, quota, next_rows, src, h, w_gu, w_gu, w_down, gs)


def _route(idx, gates, n):
    assign = 2 * n
    tiles = -(-(assign + N_EXPERTS * (MOE_TILE - 1)) // MOE_TILE)
    experts = jnp.arange(N_EXPERTS, dtype=jnp.int32)
    ea = idx.reshape(assign)
    ga = gates.reshape(assign)
    onehot = (ea[:, None] == experts[None, :]).astype(jnp.int32)
    csum = jnp.cumsum(onehot, axis=0)
    counts = csum[-1]
    etiles = (counts + MOE_TILE - 1) // MOE_TILE
    tile_end = jnp.cumsum(etiles)
    tile_start = tile_end - etiles
    cnt_start = jnp.cumsum(counts) - counts
    pos = jnp.sum(onehot * ((tile_start * MOE_TILE)[None, :] + csum - 1), axis=1)
    token = jnp.arange(assign, dtype=jnp.int32) % n
    _, sorted_token, sorted_gate = lax.sort((ea, token, ga), num_keys=1, is_stable=True)
    ti = jnp.arange(tiles, dtype=jnp.int32)
    active = ti < tile_end[-1]
    te_raw = jnp.sum((ti[:, None] >= tile_end[None, :]).astype(jnp.int32), axis=1)
    last_expert = jnp.max(jnp.where(etiles > 0, experts, 0))
    te = jnp.where(active, jnp.minimum(te_raw, N_EXPERTS - 1), last_expert)
    tile_onehot = (te[:, None] == experts[None, :]).astype(jnp.int32)
    pick = lambda per_expert: jnp.sum(tile_onehot * per_expert[None, :], axis=1)
    tile_in_expert = ti - pick(tile_start)
    tv = jnp.where(active, jnp.clip(pick(counts) - tile_in_expert * MOE_TILE, 0, MOE_TILE), 0)
    first = pick(cnt_start) + tile_in_expert * MOE_TILE
    r = jnp.arange(MOE_TILE, dtype=jnp.int32)
    valid = (r[None, :] < tv[:, None]).reshape(-1)
    sidx = jnp.clip(first[:, None] + r[None, :], 0, assign - 1).reshape(-1)
    src = jnp.where(valid, jnp.take(sorted_token, sidx, mode="clip"), 0) * SUBLANES
    gs = jnp.where(valid, jnp.take(sorted_gate, sidx, mode="clip"), 0.0)
    return src, gs.reshape(-1, 1), te, (tv + MOE_SUB - 1) // MOE_SUB, pos


def _moe_layer(x, g, router, w_gu, w_down, layer, split=None):
    n = x.shape[0]
    h, idx, gates = _router(x, g, router)
    src, gs, te, subs, pos = _route(idx, gates, n)
    y = _moe_experts(h, src, gs, te, subs, w_gu, w_down, layer)
    y1 = jnp.take(y, pos[:n], axis=0, mode="clip")
    y2 = jnp.take(y, pos[n:], axis=0, mode="clip")
    if split is None:
        return x + (y1 + y2)
    return x[:split] + (y1[:split] + y2[:split]), x[split:] + (y1[split:] + y2[split:])


def kernel(x_prompt, x_sample, state_conv, cache_kv_w128, cache_kv_w512, cache_kv_w2048, norm_mix_g, norm_ffn_g, a_w_in, a_ln_g, a_ln_b, a_w_s, a_b_s, a_w_out, b_w_in, b_conv_w, b_w_out, c_w_qkv, c_q_norm_g, c_k_norm_g, c_w_out, f_w_gu, f_w_down, m_router, m_w_gu, m_w_down):
    batch, seq, _ = x_prompt.shape
    nb = x_sample.shape[0]
    n_prompt = batch * seq
    caches = (cache_kv_w128, cache_kv_w512, cache_kv_w2048)
    x = x_prompt.reshape(n_prompt, D_MODEL)
    x_sample_rows = x_sample.transpose(1, 0, 2).reshape(DEC_SEQ * nb, D_MODEL)
    depth = norm_mix_g.shape[0]
    sgu_v, conv_p, conv_s = [], [], []
    kv_p = [[] for _ in range(N_ATTN_GROUPS)]
    kv_s = [[] for _ in range(N_ATTN_GROUPS)]
    for i in range(depth):
        kind, j = i % 3, i // 3
        if kind == 0:
            x, v_rows = _sgu_layer(x, n_prompt, norm_mix_g[i], a_w_in[j], a_ln_g[j], a_ln_b[j],
                                   a_w_s[j], a_b_s[j], a_w_out[j],
                                   x_sample=x_sample_rows if i == 0 else None)
            sgu_v.append(v_rows.reshape(DEC_SEQ, nb, D_GATE).transpose(1, 0, 2))
        elif kind == 1:
            x, tail_p, tail_s = _conv_layer(x, n_prompt, batch, state_conv[j], norm_mix_g[i],
                                            b_w_in[j], b_conv_w[j], b_w_out[j])
            conv_p.append(tail_p)
            conv_s.append(tail_s)
        else:
            x, kvp, kvs = _attn_layer(x, n_prompt, batch, [c[j] for c in caches], norm_mix_g[i],
                                      c_w_qkv[j], c_q_norm_g[j], c_k_norm_g[j], c_w_out[j])
            for g in range(N_ATTN_GROUPS):
                kv_p[g].append(kvp[g])
                kv_s[g].append(kvs[g])
        e = i // 2
        if i % 2 == 0:
            x = _dense_ffn(x, norm_ffn_g[i], f_w_gu[e], f_w_down[e])
        else:
            x = _moe_layer(x, norm_ffn_g[i], m_router[e], m_w_gu, m_w_down, e,
                           split=n_prompt if i == depth - 1 else None)
    xp, xs = x if isinstance(x, tuple) else (x[:n_prompt], x[n_prompt:])
    y_prompt = xp.reshape(batch, seq, D_MODEL)
    y_sample = xs.reshape(DEC_SEQ, nb, D_MODEL).transpose(1, 0, 2)
    stack = lambda xs: jnp.stack(xs, axis=0)
    return (y_prompt, y_sample, stack(conv_p), stack(conv_s),
            stack(kv_p[0]), stack(kv_s[0]), stack(kv_p[1]), stack(kv_s[1]),
            stack(kv_p[2]), stack(kv_s[2]), stack(sgu_v))
```

```python
import functools

import numpy as np
import jax
import jax.numpy as jnp
from jax import lax
from jax.experimental import pallas as pl
from jax.experimental.pallas import tpu as pltpu

F32 = jnp.float32
BF16 = jnp.bfloat16

D_MODEL = 1024
EPS = 1e-6
CHUNK = 128
D_GATE = 2 * D_MODEL
SGU_GROUPS = 8
SGU_GDIM = D_GATE // SGU_GROUPS
D_CONV = D_MODEL
HEAD_DIM = 64
HEADS_PER_GROUP = 4
GROUP_DIM = HEADS_PER_GROUP * HEAD_DIM
WINDOWS = (128, 512, 2048)
DILATIONS = (1, 4, 16)
N_ATTN_GROUPS = 3
D_ATTN = N_ATTN_GROUPS * GROUP_DIM
ROPE_DIM = HEAD_DIM // 4
ROPE_THETA = 500000.0
ATTN_BLOCK = 128
D_FF = 2816
N_EXPERTS = 8
D_FF_EXPERT = 3584
DEC_SEQ = 4
PAST_LEN = 2048

SUBLANES = 8
LANES = 128
VMEM_LIMIT = 56 * 1024 * 1024

ROW_TILE = 512
FF_CHUNK = D_FF // 2
MOE_TILE = 1024
MOE_SUB = 256
MOE_SUBS = MOE_TILE // MOE_SUB
MOE_FF_CHUNK = 512
MOE_FF_STEPS = D_FF_EXPERT // MOE_FF_CHUNK
assert D_MODEL == SUBLANES * LANES


def _dot(a, b):
    return jnp.dot(a, b, preferred_element_type=F32)


def _dot_nt(a, b):
    return lax.dot_general(a, b, (((1,), (1,)), ((), ())), preferred_element_type=F32)


def _rms(x, g):
    return x * lax.rsqrt(jnp.mean(x * x, axis=-1, keepdims=True) + EPS) * g


def _gelu_tanh(x):
    c = np.float32(np.sqrt(2 / np.pi))
    return x * (0.5 * (1.0 + jnp.tanh(c * (x + np.float32(0.044715) * (x * x * x)))))


def _silu(x):
    return x * (1.0 / (1.0 + jnp.exp(-x)))


def _params(*sem):
    return pltpu.CompilerParams(dimension_semantics=sem, vmem_limit_bytes=VMEM_LIMIT)


def _resident(shape):
    zeros = (0,) * len(shape)
    return pl.BlockSpec(shape, lambda *_: zeros, pipeline_mode=pl.Buffered(1))


def _row_spec(width, first_tile=0):
    return pl.BlockSpec((ROW_TILE, width), lambda i: (i + first_tile, 0))


def _sgu_front(x, gmix, winv_ref, lng, lnb):
    h = _rms(x, gmix).astype(BF16)
    v = _gelu_tanh(_dot(h, winv_ref[...]))
    mu = jnp.mean(v, axis=-1, keepdims=True)
    vc = v - mu
    var = jnp.mean(vc * vc, axis=-1, keepdims=True)
    return h, vc * lax.rsqrt(var + EPS) * lng + lnb


def _sgu_back(x, h, winu_ref, s_ref, wout_ref, o_ref):
    u = _gelu_tanh(_dot(h, winu_ref[...]))
    o_ref[...] = x + _dot((u * s_ref[...]).astype(BF16), wout_ref[...])


def _sgu_prompt_body(x_ref, gmix_ref, winu_ref, winv_ref, lng_ref, lnb_ref, ws_ref, bs_ref,
                     wout_ref, o_ref, s_ref):
    x = x_ref[...]
    h, vn = _sgu_front(x, gmix_ref[...], winv_ref, lng_ref[...], lnb_ref[...])
    vb = vn.astype(BF16)
    row = lax.broadcasted_iota(jnp.int32, (CHUNK, CHUNK), 0)
    col = lax.broadcasted_iota(jnp.int32, (CHUNK, CHUNK), 1)
    for g in range(SGU_GROUPS):
        wm = jnp.where(row >= col, ws_ref[g], 0.0).astype(BF16)
        bias = jnp.concatenate([bs_ref[g]] * (SGU_GDIM // LANES), axis=1)
        cols = slice(g * SGU_GDIM, (g + 1) * SGU_GDIM)
        for c in range(ROW_TILE // CHUNK):
            rows = slice(c * CHUNK, (c + 1) * CHUNK)
            s_ref[rows, cols] = _dot(wm, vb[rows, cols]) + bias
    _sgu_back(x, h, winu_ref, s_ref, wout_ref, o_ref)


def _sgu_prompt_staging_body(x_ref, xs_ref, *rest):
    o_ref = rest[-2]
    last = pl.program_id(0) == pl.num_programs(0) - 1

    @pl.when(jnp.logical_not(last))
    def _():
        _sgu_prompt_body(x_ref, *rest)

    @pl.when(last)
    def _():
        o_ref[...] = xs_ref[...]


def _sgu_sample_body(w4_ref, b4_ref, x_ref, gmix_ref, winu_ref, winv_ref, lng_ref, lnb_ref,
                     wout_ref, o_ref, v_ref, s_ref):
    x = x_ref[...]
    nb = x.shape[0] // DEC_SEQ
    h, vn = _sgu_front(x, gmix_ref[...], winv_ref, lng_ref[...], lnb_ref[...])
    v_ref[...] = vn
    for g in range(SGU_GROUPS):
        cols = slice(g * SGU_GDIM, (g + 1) * SGU_GDIM)
        for t in range(DEC_SEQ):
            acc = w4_ref[g, t * DEC_SEQ] * vn[0:nb, cols]
            for j in range(1, t + 1):
                acc = acc + w4_ref[g, t * DEC_SEQ + j] * vn[j * nb:(j + 1) * nb, cols]
            s_ref[t * nb:(t + 1) * nb, cols] = acc + b4_ref[g, t]
    _sgu_back(x, h, winu_ref, s_ref, wout_ref, o_ref)


def _sgu_layer(x, n_prompt, gmix, w_in, ln_g, ln_b, w_s, b_s, w_out, x_sample=None):
    n = x.shape[0] if x_sample is None else x.shape[0] + x_sample.shape[0]
    n_sample = n - n_prompt
    assert n_sample == ROW_TILE and n_prompt % ROW_TILE == 0
    winu = w_in[:, :D_GATE].astype(BF16)
    winv = w_in[:, D_GATE:].astype(BF16)
    wout = w_out.astype(BF16)
    gmix = gmix.reshape(1, D_MODEL)
    lng = ln_g.reshape(1, D_GATE)
    lnb = ln_b.reshape(1, D_GATE)
    bs = jnp.broadcast_to(b_s[:, :, None], (SGU_GROUPS, CHUNK, LANES))
    weights = [_resident((1, D_MODEL)), _resident((D_MODEL, D_GATE)), _resident((D_MODEL, D_GATE)),
               _resident((1, D_GATE)), _resident((1, D_GATE))]
    prompt_tiles = n_prompt // ROW_TILE
    prompt_specs = weights + [
        _resident((SGU_GROUPS, CHUNK, CHUNK)), _resident((SGU_GROUPS, CHUNK, LANES)),
        _resident((D_GATE, D_MODEL))]
    prompt_args = (gmix, winu, winv, lng, lnb, w_s, bs, wout)
    if x_sample is None:
        body, grid, aliases = _sgu_prompt_body, prompt_tiles, {0: 0}
        prompt_specs = [_row_spec(D_MODEL)] + prompt_specs
        prompt_args = (x,) + prompt_args
    else:
        body, grid, aliases = _sgu_prompt_staging_body, prompt_tiles + 1, {}
        prompt_specs = [
            pl.BlockSpec((ROW_TILE, D_MODEL), lambda i: (jnp.minimum(i, prompt_tiles - 1), 0)),
            _resident((ROW_TILE, D_MODEL))] + prompt_specs
        prompt_args = (x, x_sample) + prompt_args
    x = pl.pallas_call(
        body,
        grid=(grid,),
        in_specs=prompt_specs,
        out_specs=_row_spec(D_MODEL),
        out_shape=jax.ShapeDtypeStruct((n, D_MODEL), F32),
        scratch_shapes=[pltpu.VMEM((ROW_TILE, D_GATE), F32)],
        input_output_aliases=aliases,
        compiler_params=_params("parallel"),
        name="sgu_prompt",
    )(*prompt_args)
    w4 = w_s[:, :DEC_SEQ, :DEC_SEQ].reshape(SGU_GROUPS, DEC_SEQ * DEC_SEQ)
    b4 = b_s[:, :DEC_SEQ]
    smem = pl.BlockSpec(memory_space=pltpu.SMEM)
    first = n_prompt // ROW_TILE
    x, v_rows = pl.pallas_call(
        _sgu_sample_body,
        grid=(1,),
        in_specs=[smem, smem, _row_spec(D_MODEL, first)] + weights + [_resident((D_GATE, D_MODEL))],
        out_specs=[_row_spec(D_MODEL, first), pl.BlockSpec((ROW_TILE, D_GATE), lambda i: (0, 0))],
        out_shape=[jax.ShapeDtypeStruct((n, D_MODEL), F32),
                   jax.ShapeDtypeStruct((n_sample, D_GATE), F32)],
        scratch_shapes=[pltpu.VMEM((ROW_TILE, D_GATE), F32)],
        input_output_aliases={2: 0},
        compiler_params=_params("arbitrary"),
        name="sgu_sample",
    )(w4, b4, x, gmix, winu, winv, lng, lnb, wout)
    return x, v_rows


def _conv_front(x, gmix, win_ref):
    h = _rms(x, gmix).astype(BF16)
    gate_b = _dot(h, win_ref[:, 0:D_CONV])
    gate_c = _dot(h, win_ref[:, D_CONV:2 * D_CONV])
    xi = _dot(h, win_ref[:, 2 * D_CONV:3 * D_CONV])
    return gate_b, gate_c * xi


def _conv_prompt_body(x_ref, gmix_ref, win_ref, cw_ref, wout_ref, o_ref, tail_ref, zbuf, *,
                      tiles_per_seq):
    @pl.when(pl.program_id(0) % tiles_per_seq == 0)
    def _():
        zbuf[0:SUBLANES, :] = jnp.zeros((SUBLANES, D_CONV), F32)

    x = x_ref[...]
    gate_b, z = _conv_front(x, gmix_ref[...], win_ref)
    zbuf[SUBLANES:SUBLANES + ROW_TILE, :] = z
    conv = (cw_ref[2:3, :] * z
            + cw_ref[1:2, :] * zbuf[SUBLANES - 1:SUBLANES - 1 + ROW_TILE, :]
            + cw_ref[0:1, :] * zbuf[SUBLANES - 2:SUBLANES - 2 + ROW_TILE, :])
    o_ref[...] = x + _dot((gate_b * conv).astype(BF16), wout_ref[...])
    tail = zbuf[ROW_TILE + SUBLANES - 2:ROW_TILE + SUBLANES, :]
    tail_ref[0] = tail
    zbuf[SUBLANES - 2:SUBLANES, :] = tail


def _conv_sample_body(x_ref, gmix_ref, win_ref, cw_ref, st_ref, wout_ref, o_ref, tail_ref):
    x = x_ref[...]
    nb = x.shape[0] // DEC_SEQ
    gate_b, z = _conv_front(x, gmix_ref[...], win_ref)
    z1 = jnp.concatenate([st_ref[1], z[:(DEC_SEQ - 1) * nb]], axis=0)
    z2 = jnp.concatenate([st_ref[0], st_ref[1], z[:(DEC_SEQ - 2) * nb]], axis=0)
    conv = cw_ref[2:3, :] * z + cw_ref[1:2, :] * z1 + cw_ref[0:1, :] * z2
    o_ref[...] = x + _dot((gate_b * conv).astype(BF16), wout_ref[...])
    tail_ref[...] = z[(DEC_SEQ - 2) * nb:]


def _conv_layer(x, n_prompt, batch, state, gmix, w_in, conv_w, w_out):
    n = x.shape[0]
    n_sample = n - n_prompt
    nb = n_sample // DEC_SEQ
    assert n_sample == ROW_TILE
    tiles_per_seq = n_prompt // batch // ROW_TILE
    win = w_in.astype(BF16)
    wout = w_out.astype(BF16)
    gmix = gmix.reshape(1, D_MODEL)
    weights = [_resident((1, D_MODEL)), _resident((D_MODEL, 3 * D_CONV)), _resident((3, D_CONV))]
    x, tail_p = pl.pallas_call(
        functools.partial(_conv_prompt_body, tiles_per_seq=tiles_per_seq),
        grid=(n_prompt // ROW_TILE,),
        in_specs=[_row_spec(D_MODEL)] + weights + [_resident((D_CONV, D_MODEL))],
        out_specs=[_row_spec(D_MODEL),
                   pl.BlockSpec((1, 2, D_CONV), lambda i: (i // tiles_per_seq, 0, 0))],
        out_shape=[jax.ShapeDtypeStruct((n, D_MODEL), F32),
                   jax.ShapeDtypeStruct((batch, 2, D_CONV), F32)],
        scratch_shapes=[pltpu.VMEM((ROW_TILE + SUBLANES, D_CONV), F32)],
        input_output_aliases={0: 0},
        compiler_params=_params("arbitrary"),
        name="conv_prompt",
    )(x, gmix, win, conv_w, wout)
    first = n_prompt // ROW_TILE
    st = state.transpose(1, 0, 2)
    x, tail_s = pl.pallas_call(
        _conv_sample_body,
        grid=(1,),
        in_specs=[_row_spec(D_MODEL, first)] + weights + [
            _resident((2, nb, D_CONV)), _resident((D_CONV, D_MODEL))],
        out_specs=[_row_spec(D_MODEL, first), pl.BlockSpec((2 * nb, D_CONV), lambda i: (0, 0))],
        out_shape=[jax.ShapeDtypeStruct((n, D_MODEL), F32),
                   jax.ShapeDtypeStruct((2 * nb, D_CONV), F32)],
        input_output_aliases={0: 0},
        compiler_params=_params("arbitrary"),
        name="conv_sample",
    )(x, gmix, win, conv_w, st, wout)
    return x, tail_p, tail_s.reshape(2, nb, D_CONV).transpose(1, 0, 2)


def _rope_tables(seq, n_sample_rows):
    nb = n_sample_rows // DEC_SEQ
    pos = jnp.concatenate([jnp.arange(seq, dtype=jnp.int32),
                           PAST_LEN + jnp.arange(n_sample_rows, dtype=jnp.int32) // nb])
    inv_freq = jnp.power(ROPE_THETA, -jnp.arange(0, ROPE_DIM, 2, dtype=F32) / ROPE_DIM)
    d = np.arange(LANES) % HEAD_DIM
    half = ROPE_DIM // 2
    ang = pos.astype(F32)[:, None] * inv_freq[d % half][None, :]
    cos, sin = jnp.cos(ang), jnp.sin(ang)
    c = jnp.where(d[None, :] < ROPE_DIM, cos, 1.0)
    s1 = jnp.where(d[None, :] < half, -sin, 0.0)
    s2 = jnp.where((d[None, :] >= half) & (d[None, :] < ROPE_DIM), sin, 0.0)
    return c, s1, s2


def _qkv_body(x_ref, gmix_ref, w_ref, gq_ref, gk_ref, c_ref, s1_ref, s2_ref, bm_ref,
              q_ref, k_ref, v_ref):
    h = _rms(x_ref[...], gmix_ref[...]).astype(BF16)
    c, s1, s2 = c_ref[...], s1_ref[...], s2_ref[...]
    bm = bm_ref[...]

    def head_norm_rope(t, gain, out_ref, scale):
        for a in range(D_ATTN // GROUP_DIM):
            ts = t[:, a * GROUP_DIM:(a + 1) * GROUP_DIM]
            sq = ts * ts
            hi = sq.astype(BF16)
            lo = (sq - hi.astype(F32)).astype(BF16)
            ms = _dot(hi, bm) + _dot(lo, bm)
            tn = ts * lax.rsqrt(ms + EPS) * gain[:, a * GROUP_DIM:(a + 1) * GROUP_DIM]
            for b in range(GROUP_DIM // LANES):
                th = tn[:, b * LANES:(b + 1) * LANES]
                rot = (th * c + pltpu.roll(th, LANES - ROPE_DIM // 2, 1) * s1
                       + pltpu.roll(th, ROPE_DIM // 2, 1) * s2)
                out_ref[a * (GROUP_DIM // LANES) + b] = rot if scale is None else rot * scale

    head_norm_rope(_dot(h, w_ref[:, 0:D_ATTN]), gq_ref[...], q_ref, np.float32(HEAD_DIM ** -0.5))
    head_norm_rope(_dot(h, w_ref[:, D_ATTN:2 * D_ATTN]), gk_ref[...], k_ref, None)
    v = _dot(h, w_ref[:, 2 * D_ATTN:3 * D_ATTN])
    for s in range(D_ATTN // LANES):
        v_ref[s] = v[:, s * LANES:(s + 1) * LANES]


def _qkv(x, n_prompt, seq, gmix, w_qkv, q_norm_g, k_norm_g):
    n = x.shape[0]
    n_sample = n - n_prompt
    tiles_per_seq = seq // ROW_TILE
    prompt_tiles = n_prompt // ROW_TILE
    c, s1, s2 = _rope_tables(seq, n_sample)
    heads = D_ATTN // HEAD_DIM
    slabs = D_ATTN // LANES
    gq = jnp.tile(q_norm_g, heads).reshape(1, D_ATTN)
    gk = jnp.tile(k_norm_g, heads).reshape(1, D_ATTN)
    lane_head = np.arange(GROUP_DIM) // HEAD_DIM
    bm = jnp.asarray((lane_head[:, None] == lane_head[None, :]) / HEAD_DIM, BF16)
    table = pl.BlockSpec(
        (ROW_TILE, LANES),
        lambda i: (jnp.where(i < prompt_tiles, i % tiles_per_seq, tiles_per_seq), 0))
    return pl.pallas_call(
        _qkv_body,
        grid=(n // ROW_TILE,),
        in_specs=[_row_spec(D_MODEL), _resident((1, D_MODEL)), _resident((D_MODEL, 3 * D_ATTN)),
                  _resident((1, D_ATTN)), _resident((1, D_ATTN)), table, table, table,
                  _resident((GROUP_DIM, GROUP_DIM))],
        out_specs=[pl.BlockSpec((slabs, ROW_TILE, LANES), lambda i: (0, i, 0))] * 3,
        out_shape=[jax.ShapeDtypeStruct((slabs, n, LANES), F32)] * 3,
        compiler_params=_params("parallel"),
        name="qkv",
    )(x, gmix.reshape(1, D_MODEL), w_qkv.astype(BF16), gq, gk, c, s1, s2, bm)


GROUP_SLABS = GROUP_DIM // LANES


def _attn_prompt_body(q_ref, kc_ref, kp_ref, vc_ref, vp_ref, o_ref, l_ref, *, dil, blocks):
    first_key = jnp.where(pl.program_id(1) == 0, ATTN_BLOCK, 0)
    lane = lax.broadcasted_iota(jnp.int32, (1, GROUP_DIM), 1)
    head_masks = [lane // HEAD_DIM == hd for hd in range(HEADS_PER_GROUP)]
    qi = lax.broadcasted_iota(jnp.int32, (ATTN_BLOCK, 2 * ATTN_BLOCK), 0)
    kj = lax.broadcasted_iota(jnp.int32, (ATTN_BLOCK, 2 * ATTN_BLOCK), 1)
    dist = qi + ATTN_BLOCK - kj
    band = (dist >= 0) & (dist <= ATTN_BLOCK)
    band_first = band & (kj >= first_key)

    def stream_rows(r, j):
        if dil == 1:
            return pl.ds(j * ATTN_BLOCK, ATTN_BLOCK)
        return pl.ds(r + j * ATTN_BLOCK * dil, ATTN_BLOCK, stride=dil)

    def load(ref, rows):
        return jnp.concatenate([ref[s, rows, :] for s in range(GROUP_SLABS)], axis=1)

    for r in range(dil):
        for j in range(blocks):
            rows = stream_rows(r, j)
            q = load(q_ref, rows).astype(BF16)
            if j == 0:
                k_prev, v_prev, mask = load(kp_ref, stream_rows(r, 0)), load(vp_ref, stream_rows(r, 0)), band_first
            else:
                prev = stream_rows(r, j - 1)
                k_prev, v_prev, mask = load(kc_ref, prev), load(vc_ref, prev), band
            kk = jnp.concatenate([k_prev, load(kc_ref, rows)], axis=0).astype(BF16)
            vv = jnp.concatenate([v_prev, load(vc_ref, rows)], axis=0).astype(BF16)
            q_heads = jnp.concatenate([jnp.where(hm, q, jnp.zeros_like(q)) for hm in head_masks], axis=0)
            s = jnp.where(jnp.concatenate([mask] * HEADS_PER_GROUP, axis=0), _dot_nt(q_heads, kk), -jnp.inf)
            m = jnp.max(s, axis=-1, keepdims=True)
            p = jnp.exp(s - m)
            l = jnp.sum(p, axis=-1, keepdims=True)
            pv = _dot(p.astype(BF16), vv) * (1.0 / l)
            lse = m + jnp.log(l)
            o_acc = jnp.zeros((ATTN_BLOCK, GROUP_DIM), F32)
            l_acc = jnp.zeros((ATTN_BLOCK, GROUP_DIM), F32)
            for hd, hm in enumerate(head_masks):
                part = slice(hd * ATTN_BLOCK, (hd + 1) * ATTN_BLOCK)
                o_acc = o_acc + jnp.where(hm, pv[part], 0.0)
                l_acc = l_acc + jnp.where(hm, lse[part], 0.0)
            for s in range(GROUP_SLABS):
                o_ref[s, rows, :] = o_acc[:, s * LANES:(s + 1) * LANES]
                l_ref[s, rows, :] = l_acc[:, s * LANES:(s + 1) * LANES]


def _attn_prompt(q, k, v, group, batch, seq):
    dil = DILATIONS[group]
    assert WINDOWS[group] // dil == ATTN_BLOCK
    span = ATTN_BLOCK * dil
    tile = max(ROW_TILE, span)
    tiles = seq // tile
    cur = pl.BlockSpec((GROUP_SLABS, tile, LANES), lambda b, t: (group, b * tiles + t, 0))
    prev = pl.BlockSpec(
        (GROUP_SLABS, span, LANES),
        lambda b, t: (group, jnp.maximum((b * tiles + t) * (tile // span) - 1, 0), 0))
    out = pl.BlockSpec((GROUP_SLABS, tile, LANES), lambda b, t: (0, b * tiles + t, 0))
    shape = jax.ShapeDtypeStruct((GROUP_SLABS, batch * seq, LANES), F32)
    return pl.pallas_call(
        functools.partial(_attn_prompt_body, dil=dil, blocks=tile // span),
        grid=(batch, tiles),
        in_specs=[cur, cur, prev, cur, prev],
        out_specs=[out, out],
        out_shape=[shape, shape],
        compiler_params=_params("parallel", "arbitrary"),
        name=f"attn_prompt_g{group}",
    )(q, k, k, v, v)


HEADS_PER_SLAB = LANES // HEAD_DIM


def _attn_sample_body(q_ref, k_ref, v_ref, c_ref, o_ref, lse_ref, *, dil, buf, seqs):
    rows = HEADS_PER_SLAB * SUBLANES
    lane = lax.broadcasted_iota(jnp.int32, (1, LANES), 1)
    head_masks = [lane // HEAD_DIM == hh for hh in range(HEADS_PER_SLAB)]

    def visibility(width):
        t = lax.broadcasted_iota(jnp.int32, (rows, width), 0) % SUBLANES
        c = lax.broadcasted_iota(jnp.int32, (rows, width), 1)
        return jnp.where(t < DEC_SEQ, t, t - DEC_SEQ), c

    tq, c = visibility(buf)
    vis_cache = (c >= tq) if dil == 1 else (c % dil == tq)
    tq, c = visibility(SUBLANES)
    vis_fresh = (c < DEC_SEQ) & ((c <= tq) if dil == 1 else (c == tq))

    def one(bi):
        for s in range(GROUP_SLABS):
            heads = slice(s * HEADS_PER_SLAB, (s + 1) * HEADS_PER_SLAB)
            q, k_new, v_new = q_ref[s, bi], k_ref[s, bi].astype(BF16), v_ref[s, bi]
            q_heads = jnp.concatenate([jnp.where(hm, q, 0.0) for hm in head_masks], axis=0).astype(BF16)
            k_cache = c_ref[bi, 0, heads].reshape(LANES, buf).astype(BF16)
            v_cache = c_ref[bi, 1, heads].reshape(LANES, buf).astype(BF16)
            s_cache = jnp.where(vis_cache, _dot(q_heads, k_cache), -jnp.inf)
            s_fresh = jnp.where(vis_fresh, _dot_nt(q_heads, k_new), -jnp.inf)
            m = jnp.maximum(jnp.max(s_cache, axis=1, keepdims=True), jnp.max(s_fresh, axis=1, keepdims=True))
            p_cache = jnp.exp(s_cache - m)
            p_fresh = jnp.exp(s_fresh - m)
            l = jnp.sum(p_cache, axis=1, keepdims=True) + jnp.sum(p_fresh, axis=1, keepdims=True)
            inv = 1.0 / l
            o = _dot_nt((p_cache * inv).astype(BF16), v_cache)
            w_fresh = p_fresh * inv
            for t2 in range(DEC_SEQ):
                o = o + w_fresh[:, t2:t2 + 1] * v_new[t2:t2 + 1, :]
            lse = jnp.broadcast_to(m + jnp.log(l), (rows, LANES))
            o_slab = jnp.zeros((SUBLANES, LANES), F32)
            lse_slab = jnp.zeros((SUBLANES, LANES), F32)
            for hh, hm in enumerate(head_masks):
                part = slice(hh * SUBLANES, (hh + 1) * SUBLANES)
                o_slab = o_slab + jnp.where(hm, o[part], 0.0)
                lse_slab = lse_slab + jnp.where(hm, lse[part], 0.0)
            o_ref[s, bi] = o_slab
            lse_ref[s, bi] = lse_slab

    def pair(i, carry):
        one(2 * i)
        one(2 * i + 1)
        return carry

    if seqs == 2:
        pair(0, 0)
    else:
        lax.fori_loop(0, seqs // 2, pair, 0)


def _sample_rows(slabs, n_prompt):
    part = slabs[:, n_prompt:]
    nb = part.shape[1] // DEC_SEQ
    part = part.reshape(part.shape[0], DEC_SEQ, nb, LANES).transpose(0, 2, 1, 3)
    return jnp.pad(part, ((0, 0), (0, 0), (0, SUBLANES - DEC_SEQ), (0, 0)))


def _attn_sample(q, k, v, cache, group):
    nb, buf = cache.shape[0], cache.shape[1]
    dil = DILATIONS[group]
    assert buf == ATTN_BLOCK * dil and WINDOWS[group] == buf and (dil == 1 or dil >= DEC_SEQ)
    seqs = max(2, 2 * PAST_LEN // buf)
    fresh = pl.BlockSpec((GROUP_SLABS, seqs, SUBLANES, LANES), lambda i: (group, i, 0, 0))
    out = pl.BlockSpec((GROUP_SLABS, seqs, SUBLANES, LANES), lambda i: (0, i, 0, 0))
    shape = jax.ShapeDtypeStruct((GROUP_SLABS, nb, SUBLANES, LANES), F32)
    o, lse = pl.pallas_call(
        functools.partial(_attn_sample_body, dil=dil, buf=buf, seqs=seqs),
        grid=(nb // seqs,),
        in_specs=[fresh, fresh, fresh,
                  pl.BlockSpec((seqs, 2, HEADS_PER_GROUP, HEAD_DIM, buf), lambda i: (i, 0, 0, 0, 0))],
        out_specs=[out, out],
        out_shape=[shape, shape],
        compiler_params=_params("parallel"),
        name=f"attn_sample_g{group}",
    )(q, k, v, cache.transpose(0, 2, 3, 4, 1))
    rows = lambda a: a[:, :, :DEC_SEQ].transpose(0, 2, 1, 3).reshape(GROUP_SLABS, DEC_SEQ * nb, LANES)
    return rows(o), rows(lse)


def _from_slabs(slabs, group):
    part = slabs[group * GROUP_SLABS:(group + 1) * GROUP_SLABS]
    return part.transpose(1, 0, 2).reshape(part.shape[1], GROUP_DIM)


def _merge_body(x_ref, o0_ref, o1_ref, o2_ref, l0_ref, l1_ref, l2_ref, w_ref, out_ref):
    rows = lambda ref: jnp.concatenate([ref[s] for s in range(GROUP_SLABS)], axis=1)
    l0, l1, l2 = rows(l0_ref), rows(l1_ref), rows(l2_ref)
    m = jnp.maximum(jnp.maximum(l0, l1), l2)
    e0, e1, e2 = jnp.exp(l0 - m), jnp.exp(l1 - m), jnp.exp(l2 - m)
    inv = 1.0 / (e0 + e1 + e2)
    y = x_ref[...]
    for g, (o_ref, e) in enumerate(((o0_ref, e0), (o1_ref, e1), (o2_ref, e2))):
        y = y + _dot((rows(o_ref) * (e * inv)).astype(BF16),
                     w_ref[g * GROUP_DIM:(g + 1) * GROUP_DIM, :])
    out_ref[...] = y


def _merge_out_proj(x, outs, lses, w_out, first_tile, tiles):
    n = x.shape[0]
    part = pl.BlockSpec((GROUP_SLABS, ROW_TILE, LANES), lambda i: (0, i, 0))
    return pl.pallas_call(
        _merge_body,
        grid=(tiles,),
        in_specs=[_row_spec(D_MODEL, first_tile)] + [part] * 6 + [_resident((D_ATTN, D_MODEL))],
        out_specs=_row_spec(D_MODEL, first_tile),
        out_shape=jax.ShapeDtypeStruct((n, D_MODEL), F32),
        input_output_aliases={0: 0},
        compiler_params=_params("parallel"),
        name="attn_merge_out_proj",
    )(x, *outs, *lses, w_out)


def _attn_layer(x, n_prompt, batch, caches, gmix, w_qkv, q_norm_g, k_norm_g, w_out):
    n = x.shape[0]
    seq = n_prompt // batch
    nb = (n - n_prompt) // DEC_SEQ
    q, k, v = _qkv(x, n_prompt, seq, gmix, w_qkv, q_norm_g, k_norm_g)
    ks, vs = k[:, n_prompt:], v[:, n_prompt:]
    q_new, k_new, v_new = (_sample_rows(a, n_prompt) for a in (q, k, v))
    outs_p, lses_p, outs_s, lses_s, kv_p, kv_s = [], [], [], [], [], []
    for g in range(N_ATTN_GROUPS):
        o, lse = _attn_prompt(q, k, v, g, batch, seq)
        outs_p.append(o)
        lses_p.append(lse)
        o, lse = _attn_sample(q_new, k_new, v_new, caches[g], g)
        outs_s.append(o)
        lses_s.append(lse)
        k_rows, v_rows = _from_slabs(ks, g), _from_slabs(vs, g)
        keep = min(WINDOWS[g], seq)
        heads = (HEADS_PER_GROUP, HEAD_DIM)
        tail = lambda a: _from_slabs(jnp.concatenate(
            [a[g * GROUP_SLABS:(g + 1) * GROUP_SLABS, (b + 1) * seq - keep:(b + 1) * seq]
             for b in range(batch)], axis=1), 0)
        kv_p.append(jnp.stack([tail(k), tail(v)], axis=1).reshape(batch, keep, 2, *heads))
        kv_s.append(jnp.stack([k_rows, v_rows], axis=1).reshape(DEC_SEQ, nb, 2, *heads)
                    .transpose(1, 0, 2, 3, 4))
    wout = w_out.astype(BF16)
    x = _merge_out_proj(x, outs_p, lses_p, wout, 0, n_prompt // ROW_TILE)
    x = _merge_out_proj(x, outs_s, lses_s, wout, n_prompt // ROW_TILE, (n - n_prompt) // ROW_TILE)
    return x, kv_p, kv_s


def _ffn_body(x_ref, g_ref, wg_ref, wu_ref, wd_ref, o_ref):
    x = x_ref[...]
    h = _rms(x, g_ref[...]).astype(BF16)
    y = x
    for c in range(D_FF // FF_CHUNK):
        cols = slice(c * FF_CHUNK, (c + 1) * FF_CHUNK)
        act = _silu(_dot(h, wg_ref[:, cols])) * _dot(h, wu_ref[:, cols])
        y = y + _dot(act.astype(BF16), wd_ref[cols, :])
    o_ref[...] = y


def _dense_ffn(x, g, w_gu, w_down):
    n = x.shape[0]
    wgu = w_gu.astype(BF16)
    half = pl.Buffered(1)
    return pl.pallas_call(
        _ffn_body,
        grid=(n // ROW_TILE,),
        in_specs=[_row_spec(D_MODEL), _resident((1, D_MODEL)),
                  pl.BlockSpec((D_MODEL, D_FF), lambda i: (0, 0), pipeline_mode=half),
                  pl.BlockSpec((D_MODEL, D_FF), lambda i: (0, 1), pipeline_mode=half),
                  _resident((D_FF, D_MODEL))],
        out_specs=_row_spec(D_MODEL),
        out_shape=jax.ShapeDtypeStruct((n, D_MODEL), F32),
        input_output_aliases={0: 0},
        compiler_params=_params("parallel"),
        name="dense_ffn",
    )(x, g.reshape(1, D_MODEL), wgu, wgu, w_down.astype(BF16))


PACK_ROWS = D_MODEL // LANES


def _router_body(x_ref, g_ref, r_ref, h_ref, idx_ref, gate_ref):
    h = _rms(x_ref[...], g_ref[...])
    for s in range(PACK_ROWS):
        h_ref[pl.ds(s, ROW_TILE, stride=PACK_ROWS), :] = h[:, s * LANES:(s + 1) * LANES]
    logits = jnp.dot(h, r_ref[...], precision=lax.Precision.HIGHEST, preferred_element_type=F32)
    lt = logits.T[:N_EXPERTS, :]
    e = lax.broadcasted_iota(jnp.int32, lt.shape, 0)
    m1 = jnp.max(lt, axis=0, keepdims=True)
    e1 = jnp.min(jnp.where(lt == m1, e, N_EXPERTS), axis=0, keepdims=True)
    rest = jnp.where(e == e1, -jnp.inf, lt)
    m2 = jnp.max(rest, axis=0, keepdims=True)
    e2 = jnp.min(jnp.where(rest == m2, e, N_EXPERTS), axis=0, keepdims=True)
    a = jnp.exp(m2 - m1)
    inv = 1.0 / (1.0 + a)
    idx_ref[...] = jnp.concatenate([e1, e2], axis=0)
    gate_ref[...] = jnp.concatenate([inv, a * inv], axis=0)


def _router(x, g, router):
    n = x.shape[0]
    r = jnp.pad(router, ((0, 0), (0, LANES - N_EXPERTS)))
    pair = pl.BlockSpec((2, ROW_TILE), lambda i: (0, i))
    return pl.pallas_call(
        _router_body,
        grid=(n // ROW_TILE,),
        in_specs=[_row_spec(D_MODEL), _resident((1, D_MODEL)), _resident((D_MODEL, LANES))],
        out_specs=[pl.BlockSpec((ROW_TILE * PACK_ROWS, LANES), lambda i: (i, 0)), pair, pair],
        out_shape=[jax.ShapeDtypeStruct((n * PACK_ROWS, LANES), F32),
                   jax.ShapeDtypeStruct((2, n), jnp.int32),
                   jax.ShapeDtypeStruct((2, n), F32)],
        compiler_params=_params("parallel"),
        name="moe_router",
    )(x, g.reshape(1, D_MODEL), r)


def _moe_body(te_ref, nsub_ref, src_ref, h_hbm, wg_ref, wu_ref, wd_ref, gs_ref, y_ref,
              acc_ref, xbuf, xb_ref, wgb, wub, wdb, sem):
    i, f = pl.program_id(0), pl.program_id(1)
    slot = i % 2
    del te_ref

    def sub_rows(j):
        return pl.ds(j * MOE_SUB, MOE_SUB)

    def fetch(tile, slot):
        for j in range(MOE_SUBS):
            @pl.when(j < nsub_ref[tile])
            def _():
                def row(r, carry):
                    tok = pl.multiple_of(src_ref[tile * MOE_TILE + j * MOE_SUB + r], PACK_ROWS)
                    pltpu.make_async_copy(
                        h_hbm.at[pl.ds(tok, PACK_ROWS)],
                        xbuf.at[slot, pl.ds((j * MOE_SUB + r) * PACK_ROWS, PACK_ROWS)],
                        sem.at[slot, j]).start()
                    return carry
                lax.fori_loop(0, MOE_SUB, row, 0, unroll=8)

    @pl.when((i == 0) & (f == 0))
    def _():
        fetch(0, 0)

    @pl.when(f == 0)
    def _():
        for j in range(MOE_SUBS):
            @pl.when(j < nsub_ref[i])
            def _():
                blk = xbuf.at[slot, pl.ds(j * MOE_SUB * PACK_ROWS, MOE_SUB * PACK_ROWS)]
                pltpu.make_async_copy(blk, blk, sem.at[slot, j]).wait()
                for s in range(PACK_ROWS):
                    piece = xbuf[slot, pl.ds(j * MOE_SUB * PACK_ROWS + s, MOE_SUB, stride=PACK_ROWS), :]
                    xb_ref[sub_rows(j), s * LANES:(s + 1) * LANES] = piece.astype(BF16)
                acc_ref[sub_rows(j), :] = jnp.zeros((MOE_SUB, D_MODEL), F32)

        @pl.when(i + 1 < pl.num_programs(0))
        def _():
            fetch(i + 1, 1 - slot)

    @pl.when(nsub_ref[i] > 0)
    def _():
        wgb[...] = wg_ref[0, 0].astype(BF16)
        wub[...] = wu_ref[0, 0].astype(BF16)
        wdb[...] = wd_ref[0, 0].astype(BF16)
        for j in range(MOE_SUBS):
            @pl.when(j < nsub_ref[i])
            def _():
                xb = xb_ref[sub_rows(j), :]
                act = _silu(_dot(xb, wgb[...])) * _dot(xb, wub[...])
                acc_ref[sub_rows(j), :] += _dot(act.astype(BF16), wdb[...])

    @pl.when(f == MOE_FF_STEPS - 1)
    def _():
        for j in range(MOE_SUBS):
            @pl.when(j < nsub_ref[i])
            def _():
                y_ref[sub_rows(j), :] = acc_ref[sub_rows(j), :] * gs_ref[sub_rows(j), :]

            @pl.when(j >= nsub_ref[i])
            def _():
                y_ref[sub_rows(j), :] = jnp.zeros((MOE_SUB, D_MODEL), F32)


def _moe_experts(h, src, gs, tile_expert, tile_subs, w_gu, w_down, layer):
    rows = src.shape[0]
    tiles = rows // MOE_TILE
    last = MOE_FF_STEPS - 1
    fstep = lambda i, f, ns: jnp.where(ns[i] > 0, f, last)
    grid_spec = pltpu.PrefetchScalarGridSpec(
        num_scalar_prefetch=3,
        grid=(tiles, MOE_FF_STEPS),
        in_specs=[
            pl.BlockSpec(memory_space=pl.ANY),
            pl.BlockSpec((1, 1, D_MODEL, MOE_FF_CHUNK),
                         lambda i, f, te, ns, *_: (layer, te[i], 0, fstep(i, f, ns))),
            pl.BlockSpec((1, 1, D_MODEL, MOE_FF_CHUNK),
                         lambda i, f, te, ns, *_: (layer, te[i], 0, MOE_FF_STEPS + fstep(i, f, ns))),
            pl.BlockSpec((1, 1, MOE_FF_CHUNK, D_MODEL),
                         lambda i, f, te, ns, *_: (layer, te[i], fstep(i, f, ns), 0)),
            pl.BlockSpec((MOE_TILE, 1), lambda i, f, *_: (i, 0)),
        ],
        out_specs=pl.BlockSpec((MOE_TILE, D_MODEL), lambda i, f, *_: (i, 0)),
        scratch_shapes=[
            pltpu.VMEM((MOE_TILE, D_MODEL), F32),
            pltpu.VMEM((2, MOE_TILE * PACK_ROWS, LANES), F32),
            pltpu.VMEM((MOE_TILE, D_MODEL), BF16),
            pltpu.VMEM((D_MODEL, MOE_FF_CHUNK), BF16),
            pltpu.VMEM((D_MODEL, MOE_FF_CHUNK), BF16),
            pltpu.VMEM((MOE_FF_CHUNK, D_MODEL), BF16),
            pltpu.SemaphoreType.DMA((2, MOE_SUBS)),
        ],
    )
    return pl.pallas_call(
        _moe_body,
        grid_spec=grid_spec,
        out_shape=jax.ShapeDtypeStruct((rows, D_MODEL), F32),
        compiler_params=_params("arbitrary", "arbitrary"),
        name="moe_experts",
    )(tile_expert, tile_subs, src, h, w_gu, w_gu, w_down, gs)


def _route(idx, gates, n):
    assign = 2 * n
    tiles = -(-(assign + N_EXPERTS * (MOE_TILE - 1)) // MOE_TILE)
    experts = jnp.arange(N_EXPERTS, dtype=jnp.int32)
    ea = idx.reshape(assign)
    ga = gates.reshape(assign)
    onehot = (ea[:, None] == experts[None, :]).astype(jnp.int32)
    csum = jnp.cumsum(onehot, axis=0)
    counts = csum[-1]
    etiles = (counts + MOE_TILE - 1) // MOE_TILE
    tile_end = jnp.cumsum(etiles)
    tile_start = tile_end - etiles
    cnt_start = jnp.cumsum(counts) - counts
    pos = jnp.sum(onehot * ((tile_start * MOE_TILE)[None, :] + csum - 1), axis=1)
    token = jnp.arange(assign, dtype=jnp.int32) % n
    _, sorted_token, sorted_gate = lax.sort((ea, token, ga), num_keys=1, is_stable=True)
    ti = jnp.arange(tiles, dtype=jnp.int32)
    active = ti < tile_end[-1]
    te_raw = jnp.sum((ti[:, None] >= tile_end[None, :]).astype(jnp.int32), axis=1)
    last_expert = jnp.max(jnp.where(etiles > 0, experts, 0))
    te = jnp.where(active, jnp.minimum(te_raw, N_EXPERTS - 1), last_expert)
    tile_onehot = (te[:, None] == experts[None, :]).astype(jnp.int32)
    pick = lambda per_expert: jnp.sum(tile_onehot * per_expert[None, :], axis=1)
    tile_in_expert = ti - pick(tile_start)
    tv = jnp.where(active, jnp.clip(pick(counts) - tile_in_expert * MOE_TILE, 0, MOE_TILE), 0)
    first = pick(cnt_start) + tile_in_expert * MOE_TILE
    r = jnp.arange(MOE_TILE, dtype=jnp.int32)
    valid = (r[None, :] < tv[:, None]).reshape(-1)
    sidx = jnp.clip(first[:, None] + r[None, :], 0, assign - 1).reshape(-1)
    src = jnp.where(valid, jnp.take(sorted_token, sidx, mode="clip"), 0) * PACK_ROWS
    gs = jnp.where(valid, jnp.take(sorted_gate, sidx, mode="clip"), 0.0)
    return src, gs.reshape(-1, 1), te, (tv + MOE_SUB - 1) // MOE_SUB, pos


def _moe_layer(x, g, router, w_gu, w_down, layer, split=None):
    n = x.shape[0]
    h, idx, gates = _router(x, g, router)
    src, gs, te, subs, pos = _route(idx, gates, n)
    y = _moe_experts(h, src, gs, te, subs, w_gu, w_down, layer)
    y1 = jnp.take(y, pos[:n], axis=0, mode="clip")
    y2 = jnp.take(y, pos[n:], axis=0, mode="clip")
    if split is None:
        return x + (y1 + y2)
    return x[:split] + (y1[:split] + y2[:split]), x[split:] + (y1[split:] + y2[split:])


def kernel(x_prompt, x_sample, state_conv, cache_kv_w128, cache_kv_w512, cache_kv_w2048, norm_mix_g, norm_ffn_g, a_w_in, a_ln_g, a_ln_b, a_w_s, a_b_s, a_w_out, b_w_in, b_conv_w, b_w_out, c_w_qkv, c_q_norm_g, c_k_norm_g, c_w_out, f_w_gu, f_w_down, m_router, m_w_gu, m_w_down):
    batch, seq, _ = x_prompt.shape
    nb = x_sample.shape[0]
    n_prompt = batch * seq
    caches = (cache_kv_w128, cache_kv_w512, cache_kv_w2048)
    x = x_prompt.reshape(n_prompt, D_MODEL)
    x_sample_rows = x_sample.transpose(1, 0, 2).reshape(DEC_SEQ * nb, D_MODEL)
    depth = norm_mix_g.shape[0]
    sgu_v, conv_p, conv_s = [], [], []
    kv_p = [[] for _ in range(N_ATTN_GROUPS)]
    kv_s = [[] for _ in range(N_ATTN_GROUPS)]
    for i in range(depth):
        kind, j = i % 3, i // 3
        if kind == 0:
            x, v_rows = _sgu_layer(x, n_prompt, norm_mix_g[i], a_w_in[j], a_ln_g[j], a_ln_b[j],
                                   a_w_s[j], a_b_s[j], a_w_out[j],
                                   x_sample=x_sample_rows if i == 0 else None)
            sgu_v.append(v_rows.reshape(DEC_SEQ, nb, D_GATE).transpose(1, 0, 2))
        elif kind == 1:
            x, tail_p, tail_s = _conv_layer(x, n_prompt, batch, state_conv[j], norm_mix_g[i],
                                            b_w_in[j], b_conv_w[j], b_w_out[j])
            conv_p.append(tail_p)
            conv_s.append(tail_s)
        else:
            x, kvp, kvs = _attn_layer(x, n_prompt, batch, [c[j] for c in caches], norm_mix_g[i],
                                      c_w_qkv[j], c_q_norm_g[j], c_k_norm_g[j], c_w_out[j])
            for g in range(N_ATTN_GROUPS):
                kv_p[g].append(kvp[g])
                kv_s[g].append(kvs[g])
        e = i // 2
        if i % 2 == 0:
            x = _dense_ffn(x, norm_ffn_g[i], f_w_gu[e], f_w_down[e])
        else:
            x = _moe_layer(x, norm_ffn_g[i], m_router[e], m_w_gu, m_w_down, e,
                           split=n_prompt if i == depth - 1 else None)
    xp, xs = x if isinstance(x, tuple) else (x[:n_prompt], x[n_prompt:])
    y_prompt = xp.reshape(batch, seq, D_MODEL)
    y_sample = xs.reshape(DEC_SEQ, nb, D_MODEL).transpose(1, 0, 2)
    stack = lambda xs: jnp.stack(xs, axis=0)
    return (y_prompt, y_sample, stack(conv_p), stack(conv_s),
            stack(kv_p[0]), stack(kv_s[0]), stack(kv_p[1]), stack(kv_s[1]),
            stack(kv_p[2]), stack(kv_s[2]), stack(sgu_v))
```

```python
import functools

import numpy as np
import jax
import jax.numpy as jnp
from jax import lax
from jax.experimental import pallas as pl
from jax.experimental.pallas import tpu as pltpu

F32 = jnp.float32
BF16 = jnp.bfloat16

D_MODEL = 1024
EPS = 1e-6
CHUNK = 128
D_GATE = 2 * D_MODEL
SGU_GROUPS = 8
SGU_GDIM = D_GATE // SGU_GROUPS
D_CONV = D_MODEL
HEAD_DIM = 64
HEADS_PER_GROUP = 4
GROUP_DIM = HEADS_PER_GROUP * HEAD_DIM
WINDOWS = (128, 512, 2048)
DILATIONS = (1, 4, 16)
N_ATTN_GROUPS = 3
D_ATTN = N_ATTN_GROUPS * GROUP_DIM
ROPE_DIM = HEAD_DIM // 4
ROPE_THETA = 500000.0
ATTN_BLOCK = 128
D_FF = 2816
N_EXPERTS = 8
D_FF_EXPERT = 3584
DEC_SEQ = 4
PAST_LEN = 2048

SUBLANES = 8
LANES = 128
VMEM_LIMIT = 56 * 1024 * 1024

ROW_TILE = 512
FF_CHUNK = D_FF // 2
MOE_TILE = 1024
MOE_SUB = 256
MOE_SUBS = MOE_TILE // MOE_SUB
MOE_FF_CHUNK = 512
MOE_FF_STEPS = D_FF_EXPERT // MOE_FF_CHUNK
assert D_MODEL == SUBLANES * LANES


def _dot(a, b):
    return jnp.dot(a, b, preferred_element_type=F32)


def _dot_nt(a, b):
    return lax.dot_general(a, b, (((1,), (1,)), ((), ())), preferred_element_type=F32)


def _rms(x, g):
    return x * lax.rsqrt(jnp.mean(x * x, axis=-1, keepdims=True) + EPS) * g


def _gelu_tanh(x):
    c = np.float32(np.sqrt(2 / np.pi))
    return x * (0.5 * (1.0 + jnp.tanh(c * (x + np.float32(0.044715) * (x * x * x)))))


def _silu(x):
    return x * (1.0 / (1.0 + jnp.exp(-x)))


def _params(*sem):
    return pltpu.CompilerParams(dimension_semantics=sem, vmem_limit_bytes=VMEM_LIMIT)


def _resident(shape):
    zeros = (0,) * len(shape)
    return pl.BlockSpec(shape, lambda *_: zeros, pipeline_mode=pl.Buffered(1))


def _row_spec(width, first_tile=0):
    return pl.BlockSpec((ROW_TILE, width), lambda i: (i + first_tile, 0))


def _sgu_front(x, gmix, winv_ref, lng, lnb):
    h = _rms(x, gmix).astype(BF16)
    v = _gelu_tanh(_dot(h, winv_ref[...]))
    mu = jnp.mean(v, axis=-1, keepdims=True)
    vc = v - mu
    var = jnp.mean(vc * vc, axis=-1, keepdims=True)
    return h, vc * lax.rsqrt(var + EPS) * lng + lnb


def _sgu_back(x, h, winu_ref, s_ref, wout_ref, o_ref):
    u = _gelu_tanh(_dot(h, winu_ref[...]))
    o_ref[...] = x + _dot((u * s_ref[...]).astype(BF16), wout_ref[...])


def _sgu_prompt_body(x_ref, gmix_ref, winu_ref, winv_ref, lng_ref, lnb_ref, ws_ref, bs_ref,
                     wout_ref, o_ref, s_ref):
    x = x_ref[...]
    h, vn = _sgu_front(x, gmix_ref[...], winv_ref, lng_ref[...], lnb_ref[...])
    vb = vn.astype(BF16)
    row = lax.broadcasted_iota(jnp.int32, (CHUNK, CHUNK), 0)
    col = lax.broadcasted_iota(jnp.int32, (CHUNK, CHUNK), 1)
    for g in range(SGU_GROUPS):
        wm = jnp.where(row >= col, ws_ref[g], 0.0).astype(BF16)
        bias = jnp.concatenate([bs_ref[g]] * (SGU_GDIM // LANES), axis=1)
        cols = slice(g * SGU_GDIM, (g + 1) * SGU_GDIM)
        for c in range(ROW_TILE // CHUNK):
            rows = slice(c * CHUNK, (c + 1) * CHUNK)
            s_ref[rows, cols] = _dot(wm, vb[rows, cols]) + bias
    _sgu_back(x, h, winu_ref, s_ref, wout_ref, o_ref)


def _sgu_prompt_staging_body(x_ref, xs_ref, *rest):
    o_ref = rest[-2]
    last = pl.program_id(0) == pl.num_programs(0) - 1

    @pl.when(jnp.logical_not(last))
    def _():
        _sgu_prompt_body(x_ref, *rest)

    @pl.when(last)
    def _():
        o_ref[...] = xs_ref[...]


def _sgu_sample_body(w4_ref, b4_ref, x_ref, gmix_ref, winu_ref, winv_ref, lng_ref, lnb_ref,
                     wout_ref, o_ref, v_ref, s_ref):
    x = x_ref[...]
    nb = x.shape[0] // DEC_SEQ
    h, vn = _sgu_front(x, gmix_ref[...], winv_ref, lng_ref[...], lnb_ref[...])
    v_ref[...] = vn
    for g in range(SGU_GROUPS):
        cols = slice(g * SGU_GDIM, (g + 1) * SGU_GDIM)
        for t in range(DEC_SEQ):
            acc = w4_ref[g, t * DEC_SEQ] * vn[0:nb, cols]
            for j in range(1, t + 1):
                acc = acc + w4_ref[g, t * DEC_SEQ + j] * vn[j * nb:(j + 1) * nb, cols]
            s_ref[t * nb:(t + 1) * nb, cols] = acc + b4_ref[g, t]
    _sgu_back(x, h, winu_ref, s_ref, wout_ref, o_ref)


def _sgu_layer(x, n_prompt, gmix, w_in, ln_g, ln_b, w_s, b_s, w_out, x_sample=None):
    n = x.shape[0] if x_sample is None else x.shape[0] + x_sample.shape[0]
    n_sample = n - n_prompt
    assert n_sample == ROW_TILE and n_prompt % ROW_TILE == 0
    winu = w_in[:, :D_GATE].astype(BF16)
    winv = w_in[:, D_GATE:].astype(BF16)
    wout = w_out.astype(BF16)
    gmix = gmix.reshape(1, D_MODEL)
    lng = ln_g.reshape(1, D_GATE)
    lnb = ln_b.reshape(1, D_GATE)
    bs = jnp.broadcast_to(b_s[:, :, None], (SGU_GROUPS, CHUNK, LANES))
    weights = [_resident((1, D_MODEL)), _resident((D_MODEL, D_GATE)), _resident((D_MODEL, D_GATE)),
               _resident((1, D_GATE)), _resident((1, D_GATE))]
    prompt_tiles = n_prompt // ROW_TILE
    prompt_specs = weights + [
        _resident((SGU_GROUPS, CHUNK, CHUNK)), _resident((SGU_GROUPS, CHUNK, LANES)),
        _resident((D_GATE, D_MODEL))]
    prompt_args = (gmix, winu, winv, lng, lnb, w_s, bs, wout)
    if x_sample is None:
        body, grid, aliases = _sgu_prompt_body, prompt_tiles, {0: 0}
        prompt_specs = [_row_spec(D_MODEL)] + prompt_specs
        prompt_args = (x,) + prompt_args
    else:
        body, grid, aliases = _sgu_prompt_staging_body, prompt_tiles + 1, {}
        prompt_specs = [
            pl.BlockSpec((ROW_TILE, D_MODEL), lambda i: (jnp.minimum(i, prompt_tiles - 1), 0)),
            _resident((ROW_TILE, D_MODEL))] + prompt_specs
        prompt_args = (x, x_sample) + prompt_args
    x = pl.pallas_call(
        body,
        grid=(grid,),
        in_specs=prompt_specs,
        out_specs=_row_spec(D_MODEL),
        out_shape=jax.ShapeDtypeStruct((n, D_MODEL), F32),
        scratch_shapes=[pltpu.VMEM((ROW_TILE, D_GATE), F32)],
        input_output_aliases=aliases,
        compiler_params=_params("parallel"),
        name="sgu_prompt",
    )(*prompt_args)
    w4 = w_s[:, :DEC_SEQ, :DEC_SEQ].reshape(SGU_GROUPS, DEC_SEQ * DEC_SEQ)
    b4 = b_s[:, :DEC_SEQ]
    smem = pl.BlockSpec(memory_space=pltpu.SMEM)
    first = n_prompt // ROW_TILE
    x, v_rows = pl.pallas_call(
        _sgu_sample_body,
        grid=(1,),
        in_specs=[smem, smem, _row_spec(D_MODEL, first)] + weights + [_resident((D_GATE, D_MODEL))],
        out_specs=[_row_spec(D_MODEL, first), pl.BlockSpec((ROW_TILE, D_GATE), lambda i: (0, 0))],
        out_shape=[jax.ShapeDtypeStruct((n, D_MODEL), F32),
                   jax.ShapeDtypeStruct((n_sample, D_GATE), F32)],
        scratch_shapes=[pltpu.VMEM((ROW_TILE, D_GATE), F32)],
        input_output_aliases={2: 0},
        compiler_params=_params("arbitrary"),
        name="sgu_sample",
    )(w4, b4, x, gmix, winu, winv, lng, lnb, wout)
    return x, v_rows


def _conv_front(x, gmix, win_ref):
    h = _rms(x, gmix).astype(BF16)
    gate_b = _dot(h, win_ref[:, 0:D_CONV])
    gate_c = _dot(h, win_ref[:, D_CONV:2 * D_CONV])
    xi = _dot(h, win_ref[:, 2 * D_CONV:3 * D_CONV])
    return gate_b, gate_c * xi


def _conv_prompt_body(x_ref, gmix_ref, win_ref, cw_ref, wout_ref, o_ref, tail_ref, zbuf, *,
                      tiles_per_seq):
    @pl.when(pl.program_id(0) % tiles_per_seq == 0)
    def _():
        zbuf[0:SUBLANES, :] = jnp.zeros((SUBLANES, D_CONV), F32)

    x = x_ref[...]
    gate_b, z = _conv_front(x, gmix_ref[...], win_ref)
    zbuf[SUBLANES:SUBLANES + ROW_TILE, :] = z
    conv = (cw_ref[2:3, :] * z
            + cw_ref[1:2, :] * zbuf[SUBLANES - 1:SUBLANES - 1 + ROW_TILE, :]
            + cw_ref[0:1, :] * zbuf[SUBLANES - 2:SUBLANES - 2 + ROW_TILE, :])
    o_ref[...] = x + _dot((gate_b * conv).astype(BF16), wout_ref[...])
    tail = zbuf[ROW_TILE + SUBLANES - 2:ROW_TILE + SUBLANES, :]
    tail_ref[0] = tail
    zbuf[SUBLANES - 2:SUBLANES, :] = tail


def _conv_sample_body(x_ref, gmix_ref, win_ref, cw_ref, st_ref, wout_ref, o_ref, tail_ref):
    x = x_ref[...]
    nb = x.shape[0] // DEC_SEQ
    gate_b, z = _conv_front(x, gmix_ref[...], win_ref)
    z1 = jnp.concatenate([st_ref[1], z[:(DEC_SEQ - 1) * nb]], axis=0)
    z2 = jnp.concatenate([st_ref[0], st_ref[1], z[:(DEC_SEQ - 2) * nb]], axis=0)
    conv = cw_ref[2:3, :] * z + cw_ref[1:2, :] * z1 + cw_ref[0:1, :] * z2
    o_ref[...] = x + _dot((gate_b * conv).astype(BF16), wout_ref[...])
    tail_ref[...] = z[(DEC_SEQ - 2) * nb:]


def _conv_layer(x, n_prompt, batch, state, gmix, w_in, conv_w, w_out):
    n = x.shape[0]
    n_sample = n - n_prompt
    nb = n_sample // DEC_SEQ
    assert n_sample == ROW_TILE
    tiles_per_seq = n_prompt // batch // ROW_TILE
    win = w_in.astype(BF16)
    wout = w_out.astype(BF16)
    gmix = gmix.reshape(1, D_MODEL)
    weights = [_resident((1, D_MODEL)), _resident((D_MODEL, 3 * D_CONV)), _resident((3, D_CONV))]
    x, tail_p = pl.pallas_call(
        functools.partial(_conv_prompt_body, tiles_per_seq=tiles_per_seq),
        grid=(n_prompt // ROW_TILE,),
        in_specs=[_row_spec(D_MODEL)] + weights + [_resident((D_CONV, D_MODEL))],
        out_specs=[_row_spec(D_MODEL),
                   pl.BlockSpec((1, 2, D_CONV), lambda i: (i // tiles_per_seq, 0, 0))],
        out_shape=[jax.ShapeDtypeStruct((n, D_MODEL), F32),
                   jax.ShapeDtypeStruct((batch, 2, D_CONV), F32)],
        scratch_shapes=[pltpu.VMEM((ROW_TILE + SUBLANES, D_CONV), F32)],
        input_output_aliases={0: 0},
        compiler_params=_params("arbitrary"),
        name="conv_prompt",
    )(x, gmix, win, conv_w, wout)
    first = n_prompt // ROW_TILE
    st = state.transpose(1, 0, 2)
    x, tail_s = pl.pallas_call(
        _conv_sample_body,
        grid=(1,),
        in_specs=[_row_spec(D_MODEL, first)] + weights + [
            _resident((2, nb, D_CONV)), _resident((D_CONV, D_MODEL))],
        out_specs=[_row_spec(D_MODEL, first), pl.BlockSpec((2 * nb, D_CONV), lambda i: (0, 0))],
        out_shape=[jax.ShapeDtypeStruct((n, D_MODEL), F32),
                   jax.ShapeDtypeStruct((2 * nb, D_CONV), F32)],
        input_output_aliases={0: 0},
        compiler_params=_params("arbitrary"),
        name="conv_sample",
    )(x, gmix, win, conv_w, st, wout)
    return x, tail_p, tail_s.reshape(2, nb, D_CONV).transpose(1, 0, 2)


def _rope_tables(seq, n_sample_rows):
    nb = n_sample_rows // DEC_SEQ
    pos = jnp.concatenate([jnp.arange(seq, dtype=jnp.int32),
                           PAST_LEN + jnp.arange(n_sample_rows, dtype=jnp.int32) // nb])
    inv_freq = jnp.power(ROPE_THETA, -jnp.arange(0, ROPE_DIM, 2, dtype=F32) / ROPE_DIM)
    d = np.arange(LANES) % HEAD_DIM
    half = ROPE_DIM // 2
    ang = pos.astype(F32)[:, None] * inv_freq[d % half][None, :]
    cos, sin = jnp.cos(ang), jnp.sin(ang)
    c = jnp.where(d[None, :] < ROPE_DIM, cos, 1.0)
    s1 = jnp.where(d[None, :] < half, -sin, 0.0)
    s2 = jnp.where((d[None, :] >= half) & (d[None, :] < ROPE_DIM), sin, 0.0)
    return c, s1, s2


def _qkv_body(x_ref, gmix_ref, w_ref, gq_ref, gk_ref, c_ref, s1_ref, s2_ref, bm_ref,
              q_ref, k_ref, v_ref):
    h = _rms(x_ref[...], gmix_ref[...]).astype(BF16)
    c, s1, s2 = c_ref[...], s1_ref[...], s2_ref[...]
    bm = bm_ref[...]

    def head_norm_rope(t, gain, out_ref, scale):
        for a in range(D_ATTN // GROUP_DIM):
            ts = t[:, a * GROUP_DIM:(a + 1) * GROUP_DIM]
            sq = ts * ts
            hi = sq.astype(BF16)
            lo = (sq - hi.astype(F32)).astype(BF16)
            ms = _dot(hi, bm) + _dot(lo, bm)
            tn = ts * lax.rsqrt(ms + EPS) * gain[:, a * GROUP_DIM:(a + 1) * GROUP_DIM]
            for b in range(GROUP_DIM // LANES):
                th = tn[:, b * LANES:(b + 1) * LANES]
                rot = (th * c + pltpu.roll(th, LANES - ROPE_DIM // 2, 1) * s1
                       + pltpu.roll(th, ROPE_DIM // 2, 1) * s2)
                out_ref[a * (GROUP_DIM // LANES) + b] = rot if scale is None else rot * scale

    head_norm_rope(_dot(h, w_ref[:, 0:D_ATTN]), gq_ref[...], q_ref, np.float32(HEAD_DIM ** -0.5))
    head_norm_rope(_dot(h, w_ref[:, D_ATTN:2 * D_ATTN]), gk_ref[...], k_ref, None)
    v = _dot(h, w_ref[:, 2 * D_ATTN:3 * D_ATTN])
    for s in range(D_ATTN // LANES):
        v_ref[s] = v[:, s * LANES:(s + 1) * LANES]


def _qkv(x, n_prompt, seq, gmix, w_qkv, q_norm_g, k_norm_g):
    n = x.shape[0]
    n_sample = n - n_prompt
    tiles_per_seq = seq // ROW_TILE
    prompt_tiles = n_prompt // ROW_TILE
    c, s1, s2 = _rope_tables(seq, n_sample)
    heads = D_ATTN // HEAD_DIM
    slabs = D_ATTN // LANES
    gq = jnp.tile(q_norm_g, heads).reshape(1, D_ATTN)
    gk = jnp.tile(k_norm_g, heads).reshape(1, D_ATTN)
    lane_head = np.arange(GROUP_DIM) // HEAD_DIM
    bm = jnp.asarray((lane_head[:, None] == lane_head[None, :]) / HEAD_DIM, BF16)
    table = pl.BlockSpec(
        (ROW_TILE, LANES),
        lambda i: (jnp.where(i < prompt_tiles, i % tiles_per_seq, tiles_per_seq), 0))
    return pl.pallas_call(
        _qkv_body,
        grid=(n // ROW_TILE,),
        in_specs=[_row_spec(D_MODEL), _resident((1, D_MODEL)), _resident((D_MODEL, 3 * D_ATTN)),
                  _resident((1, D_ATTN)), _resident((1, D_ATTN)), table, table, table,
                  _resident((GROUP_DIM, GROUP_DIM))],
        out_specs=[pl.BlockSpec((slabs, ROW_TILE, LANES), lambda i: (0, i, 0))] * 3,
        out_shape=[jax.ShapeDtypeStruct((slabs, n, LANES), F32)] * 3,
        compiler_params=_params("parallel"),
        name="qkv",
    )(x, gmix.reshape(1, D_MODEL), w_qkv.astype(BF16), gq, gk, c, s1, s2, bm)


GROUP_SLABS = GROUP_DIM // LANES


def _attn_prompt_body(q_ref, kc_ref, kp_ref, vc_ref, vp_ref, o_ref, l_ref, *, dil, blocks):
    first_key = jnp.where(pl.program_id(1) == 0, ATTN_BLOCK, 0)
    lane = lax.broadcasted_iota(jnp.int32, (1, GROUP_DIM), 1)
    head_masks = [lane // HEAD_DIM == hd for hd in range(HEADS_PER_GROUP)]
    qi = lax.broadcasted_iota(jnp.int32, (ATTN_BLOCK, 2 * ATTN_BLOCK), 0)
    kj = lax.broadcasted_iota(jnp.int32, (ATTN_BLOCK, 2 * ATTN_BLOCK), 1)
    dist = qi + ATTN_BLOCK - kj
    band = (dist >= 0) & (dist <= ATTN_BLOCK)
    band_first = band & (kj >= first_key)

    def stream_rows(r, j):
        if dil == 1:
            return pl.ds(j * ATTN_BLOCK, ATTN_BLOCK)
        return pl.ds(r + j * ATTN_BLOCK * dil, ATTN_BLOCK, stride=dil)

    def load(ref, rows):
        return jnp.concatenate([ref[s, rows, :] for s in range(GROUP_SLABS)], axis=1)

    for r in range(dil):
        for j in range(blocks):
            rows = stream_rows(r, j)
            q = load(q_ref, rows).astype(BF16)
            if j == 0:
                k_prev, v_prev, mask = load(kp_ref, stream_rows(r, 0)), load(vp_ref, stream_rows(r, 0)), band_first
            else:
                prev = stream_rows(r, j - 1)
                k_prev, v_prev, mask = load(kc_ref, prev), load(vc_ref, prev), band
            kk = jnp.concatenate([k_prev, load(kc_ref, rows)], axis=0).astype(BF16)
            vv = jnp.concatenate([v_prev, load(vc_ref, rows)], axis=0).astype(BF16)
            q_heads = jnp.concatenate([jnp.where(hm, q, jnp.zeros_like(q)) for hm in head_masks], axis=0)
            s = jnp.where(jnp.concatenate([mask] * HEADS_PER_GROUP, axis=0), _dot_nt(q_heads, kk), -jnp.inf)
            m = jnp.max(s, axis=-1, keepdims=True)
            p = jnp.exp(s - m)
            l = jnp.sum(p, axis=-1, keepdims=True)
            pv = _dot(p.astype(BF16), vv) * (1.0 / l)
            lse = m + jnp.log(l)
            o_acc = jnp.zeros((ATTN_BLOCK, GROUP_DIM), F32)
            l_acc = jnp.zeros((ATTN_BLOCK, GROUP_DIM), F32)
            for hd, hm in enumerate(head_masks):
                part = slice(hd * ATTN_BLOCK, (hd + 1) * ATTN_BLOCK)
                o_acc = o_acc + jnp.where(hm, pv[part], 0.0)
                l_acc = l_acc + jnp.where(hm, lse[part], 0.0)
            for s in range(GROUP_SLABS):
                o_ref[s, rows, :] = o_acc[:, s * LANES:(s + 1) * LANES]
                l_ref[s, rows, :] = l_acc[:, s * LANES:(s + 1) * LANES]


def _attn_prompt(q, k, v, group, batch, seq):
    dil = DILATIONS[group]
    assert WINDOWS[group] // dil == ATTN_BLOCK
    span = ATTN_BLOCK * dil
    tile = max(ROW_TILE, span)
    tiles = seq // tile
    cur = pl.BlockSpec((GROUP_SLABS, tile, LANES), lambda b, t: (group, b * tiles + t, 0))
    prev = pl.BlockSpec(
        (GROUP_SLABS, span, LANES),
        lambda b, t: (group, jnp.maximum((b * tiles + t) * (tile // span) - 1, 0), 0))
    out = pl.BlockSpec((GROUP_SLABS, tile, LANES), lambda b, t: (0, b * tiles + t, 0))
    shape = jax.ShapeDtypeStruct((GROUP_SLABS, batch * seq, LANES), F32)
    return pl.pallas_call(
        functools.partial(_attn_prompt_body, dil=dil, blocks=tile // span),
        grid=(batch, tiles),
        in_specs=[cur, cur, prev, cur, prev],
        out_specs=[out, out],
        out_shape=[shape, shape],
        compiler_params=_params("parallel", "arbitrary"),
        name=f"attn_prompt_g{group}",
    )(q, k, k, v, v)


HEADS_PER_SLAB = LANES // HEAD_DIM


def _attn_sample_body(q_ref, k_ref, v_ref, c_ref, o_ref, lse_ref, *, dil, buf, seqs):
    rows = HEADS_PER_SLAB * SUBLANES
    lane = lax.broadcasted_iota(jnp.int32, (1, LANES), 1)
    head_masks = [lane // HEAD_DIM == hh for hh in range(HEADS_PER_SLAB)]

    def visibility(width):
        t = lax.broadcasted_iota(jnp.int32, (rows, width), 0) % SUBLANES
        c = lax.broadcasted_iota(jnp.int32, (rows, width), 1)
        return jnp.where(t < DEC_SEQ, t, t - DEC_SEQ), c

    tq, c = visibility(buf)
    vis_cache = (c >= tq) if dil == 1 else (c % dil == tq)
    tq, c = visibility(SUBLANES)
    vis_fresh = (c < DEC_SEQ) & ((c <= tq) if dil == 1 else (c == tq))

    def one(bi):
        for s in range(GROUP_SLABS):
            heads = slice(s * HEADS_PER_SLAB, (s + 1) * HEADS_PER_SLAB)
            q, k_new, v_new = q_ref[s, bi], k_ref[s, bi].astype(BF16), v_ref[s, bi]
            q_heads = jnp.concatenate([jnp.where(hm, q, 0.0) for hm in head_masks], axis=0).astype(BF16)
            k_cache = c_ref[bi, 0, heads].reshape(LANES, buf).astype(BF16)
            v_cache = c_ref[bi, 1, heads].reshape(LANES, buf).astype(BF16)
            s_cache = jnp.where(vis_cache, _dot(q_heads, k_cache), -jnp.inf)
            s_fresh = jnp.where(vis_fresh, _dot_nt(q_heads, k_new), -jnp.inf)
            m = jnp.maximum(jnp.max(s_cache, axis=1, keepdims=True), jnp.max(s_fresh, axis=1, keepdims=True))
            p_cache = jnp.exp(s_cache - m)
            p_fresh = jnp.exp(s_fresh - m)
            l = jnp.sum(p_cache, axis=1, keepdims=True) + jnp.sum(p_fresh, axis=1, keepdims=True)
            inv = 1.0 / l
            o = _dot_nt((p_cache * inv).astype(BF16), v_cache)
            w_fresh = p_fresh * inv
            for t2 in range(DEC_SEQ):
                o = o + w_fresh[:, t2:t2 + 1] * v_new[t2:t2 + 1, :]
            lse = jnp.broadcast_to(m + jnp.log(l), (rows, LANES))
            o_slab = jnp.zeros((SUBLANES, LANES), F32)
            lse_slab = jnp.zeros((SUBLANES, LANES), F32)
            for hh, hm in enumerate(head_masks):
                part = slice(hh * SUBLANES, (hh + 1) * SUBLANES)
                o_slab = o_slab + jnp.where(hm, o[part], 0.0)
                lse_slab = lse_slab + jnp.where(hm, lse[part], 0.0)
            o_ref[s, bi] = o_slab
            lse_ref[s, bi] = lse_slab

    def pair(i, carry):
        one(2 * i)
        one(2 * i + 1)
        return carry

    if seqs == 2:
        pair(0, 0)
    else:
        lax.fori_loop(0, seqs // 2, pair, 0)


def _sample_rows(slabs, n_prompt):
    part = slabs[:, n_prompt:]
    nb = part.shape[1] // DEC_SEQ
    part = part.reshape(part.shape[0], DEC_SEQ, nb, LANES).transpose(0, 2, 1, 3)
    return jnp.pad(part, ((0, 0), (0, 0), (0, SUBLANES - DEC_SEQ), (0, 0)))


def _attn_sample(q, k, v, cache, group):
    nb, buf = cache.shape[0], cache.shape[1]
    dil = DILATIONS[group]
    assert buf == ATTN_BLOCK * dil and WINDOWS[group] == buf and (dil == 1 or dil >= DEC_SEQ)
    seqs = max(2, 2 * PAST_LEN // buf)
    fresh = pl.BlockSpec((GROUP_SLABS, seqs, SUBLANES, LANES), lambda i: (group, i, 0, 0))
    out = pl.BlockSpec((GROUP_SLABS, seqs, SUBLANES, LANES), lambda i: (0, i, 0, 0))
    shape = jax.ShapeDtypeStruct((GROUP_SLABS, nb, SUBLANES, LANES), F32)
    o, lse = pl.pallas_call(
        functools.partial(_attn_sample_body, dil=dil, buf=buf, seqs=seqs),
        grid=(nb // seqs,),
        in_specs=[fresh, fresh, fresh,
                  pl.BlockSpec((seqs, 2, HEADS_PER_GROUP, HEAD_DIM, buf), lambda i: (i, 0, 0, 0, 0))],
        out_specs=[out, out],
        out_shape=[shape, shape],
        compiler_params=_params("parallel"),
        name=f"attn_sample_g{group}",
    )(q, k, v, cache.transpose(0, 2, 3, 4, 1))
    rows = lambda a: a[:, :, :DEC_SEQ].transpose(0, 2, 1, 3).reshape(GROUP_SLABS, DEC_SEQ * nb, LANES)
    return rows(o), rows(lse)


def _from_slabs(slabs, group):
    part = slabs[group * GROUP_SLABS:(group + 1) * GROUP_SLABS]
    return part.transpose(1, 0, 2).reshape(part.shape[1], GROUP_DIM)


def _merge_body(x_ref, o0_ref, o1_ref, o2_ref, l0_ref, l1_ref, l2_ref, w_ref, out_ref):
    rows = lambda ref: jnp.concatenate([ref[s] for s in range(GROUP_SLABS)], axis=1)
    l0, l1, l2 = rows(l0_ref), rows(l1_ref), rows(l2_ref)
    m = jnp.maximum(jnp.maximum(l0, l1), l2)
    e0, e1, e2 = jnp.exp(l0 - m), jnp.exp(l1 - m), jnp.exp(l2 - m)
    inv = 1.0 / (e0 + e1 + e2)
    y = x_ref[...]
    for g, (o_ref, e) in enumerate(((o0_ref, e0), (o1_ref, e1), (o2_ref, e2))):
        y = y + _dot((rows(o_ref) * (e * inv)).astype(BF16),
                     w_ref[g * GROUP_DIM:(g + 1) * GROUP_DIM, :])
    out_ref[...] = y


def _merge_out_proj(x, outs, lses, w_out, first_tile, tiles):
    n = x.shape[0]
    part = pl.BlockSpec((GROUP_SLABS, ROW_TILE, LANES), lambda i: (0, i, 0))
    return pl.pallas_call(
        _merge_body,
        grid=(tiles,),
        in_specs=[_row_spec(D_MODEL, first_tile)] + [part] * 6 + [_resident((D_ATTN, D_MODEL))],
        out_specs=_row_spec(D_MODEL, first_tile),
        out_shape=jax.ShapeDtypeStruct((n, D_MODEL), F32),
        input_output_aliases={0: 0},
        compiler_params=_params("parallel"),
        name="attn_merge_out_proj",
    )(x, *outs, *lses, w_out)


def _attn_layer(x, n_prompt, batch, caches, gmix, w_qkv, q_norm_g, k_norm_g, w_out):
    n = x.shape[0]
    seq = n_prompt // batch
    nb = (n - n_prompt) // DEC_SEQ
    q, k, v = _qkv(x, n_prompt, seq, gmix, w_qkv, q_norm_g, k_norm_g)
    ks, vs = k[:, n_prompt:], v[:, n_prompt:]
    q_new, k_new, v_new = (_sample_rows(a, n_prompt) for a in (q, k, v))
    outs_p, lses_p, outs_s, lses_s, kv_p, kv_s = [], [], [], [], [], []
    for g in range(N_ATTN_GROUPS):
        o, lse = _attn_prompt(q, k, v, g, batch, seq)
        outs_p.append(o)
        lses_p.append(lse)
        o, lse = _attn_sample(q_new, k_new, v_new, caches[g], g)
        outs_s.append(o)
        lses_s.append(lse)
        k_rows, v_rows = _from_slabs(ks, g), _from_slabs(vs, g)
        keep = min(WINDOWS[g], seq)
        heads = (HEADS_PER_GROUP, HEAD_DIM)
        tail = lambda a: _from_slabs(jnp.concatenate(
            [a[g * GROUP_SLABS:(g + 1) * GROUP_SLABS, (b + 1) * seq - keep:(b + 1) * seq]
             for b in range(batch)], axis=1), 0)
        kv_p.append(jnp.stack([tail(k), tail(v)], axis=1).reshape(batch, keep, 2, *heads))
        kv_s.append(jnp.stack([k_rows, v_rows], axis=1).reshape(DEC_SEQ, nb, 2, *heads)
                    .transpose(1, 0, 2, 3, 4))
    wout = w_out.astype(BF16)
    x = _merge_out_proj(x, outs_p, lses_p, wout, 0, n_prompt // ROW_TILE)
    x = _merge_out_proj(x, outs_s, lses_s, wout, n_prompt // ROW_TILE, (n - n_prompt) // ROW_TILE)
    return x, kv_p, kv_s


def _ffn_body(x_ref, g_ref, wg_ref, wu_ref, wd_ref, o_ref):
    x = x_ref[...]
    h = _rms(x, g_ref[...]).astype(BF16)
    y = x
    for c in range(D_FF // FF_CHUNK):
        cols = slice(c * FF_CHUNK, (c + 1) * FF_CHUNK)
        act = _silu(_dot(h, wg_ref[:, cols])) * _dot(h, wu_ref[:, cols])
        y = y + _dot(act.astype(BF16), wd_ref[cols, :])
    o_ref[...] = y


def _dense_ffn(x, g, w_gu, w_down):
    n = x.shape[0]
    wgu = w_gu.astype(BF16)
    half = pl.Buffered(1)
    return pl.pallas_call(
        _ffn_body,
        grid=(n // ROW_TILE,),
        in_specs=[_row_spec(D_MODEL), _resident((1, D_MODEL)),
                  pl.BlockSpec((D_MODEL, D_FF), lambda i: (0, 0), pipeline_mode=half),
                  pl.BlockSpec((D_MODEL, D_FF), lambda i: (0, 1), pipeline_mode=half),
                  _resident((D_FF, D_MODEL))],
        out_specs=_row_spec(D_MODEL),
        out_shape=jax.ShapeDtypeStruct((n, D_MODEL), F32),
        input_output_aliases={0: 0},
        compiler_params=_params("parallel"),
        name="dense_ffn",
    )(x, g.reshape(1, D_MODEL), wgu, wgu, w_down.astype(BF16))


PACK_ROWS = D_MODEL // LANES
MOE_FETCH_UNROLL = 8
DMA_THREADS = 2


def _router_body(x_ref, g_ref, r_ref, h_ref, idx_ref, gate_ref):
    h = _rms(x_ref[...], g_ref[...])
    for s in range(PACK_ROWS):
        h_ref[pl.ds(s, ROW_TILE, stride=PACK_ROWS), :] = h[:, s * LANES:(s + 1) * LANES]
    logits = jnp.dot(h, r_ref[...], precision=lax.Precision.HIGHEST, preferred_element_type=F32)
    lt = logits.T[:N_EXPERTS, :]
    e = lax.broadcasted_iota(jnp.int32, lt.shape, 0)
    m1 = jnp.max(lt, axis=0, keepdims=True)
    e1 = jnp.min(jnp.where(lt == m1, e, N_EXPERTS), axis=0, keepdims=True)
    rest = jnp.where(e == e1, -jnp.inf, lt)
    m2 = jnp.max(rest, axis=0, keepdims=True)
    e2 = jnp.min(jnp.where(rest == m2, e, N_EXPERTS), axis=0, keepdims=True)
    a = jnp.exp(m2 - m1)
    inv = 1.0 / (1.0 + a)
    idx_ref[...] = jnp.concatenate([e1, e2], axis=0)
    gate_ref[...] = jnp.concatenate([inv, a * inv], axis=0)


def _router(x, g, router):
    n = x.shape[0]
    r = jnp.pad(router, ((0, 0), (0, LANES - N_EXPERTS)))
    pair = pl.BlockSpec((2, ROW_TILE), lambda i: (0, i))
    return pl.pallas_call(
        _router_body,
        grid=(n // ROW_TILE,),
        in_specs=[_row_spec(D_MODEL), _resident((1, D_MODEL)), _resident((D_MODEL, LANES))],
        out_specs=[pl.BlockSpec((ROW_TILE * PACK_ROWS, LANES), lambda i: (i, 0)), pair, pair],
        out_shape=[jax.ShapeDtypeStruct((n * PACK_ROWS, LANES), F32),
                   jax.ShapeDtypeStruct((2, n), jnp.int32),
                   jax.ShapeDtypeStruct((2, n), F32)],
        compiler_params=_params("parallel"),
        name="moe_router",
    )(x, g.reshape(1, D_MODEL), r)


def _moe_body(te_ref, nsub_ref, src_ref, h_hbm, wg_ref, wu_ref, wd_ref, gs_ref, y_ref,
              acc_ref, xbuf, xb_ref, wgb, wub, wdb, sem):
    i, f = pl.program_id(0), pl.program_id(1)
    slot = i % 2
    del te_ref

    def sub_rows(j):
        return pl.ds(j * MOE_SUB, MOE_SUB)

    def fetch(tile, slot):
        for j in range(MOE_SUBS):
            @pl.when(j < nsub_ref[tile])
            def _():
                def rows(g, carry):
                    for k in range(MOE_FETCH_UNROLL):
                        r = j * MOE_SUB + g * MOE_FETCH_UNROLL + k
                        tok = pl.multiple_of(src_ref[tile * MOE_TILE + r], PACK_ROWS)
                        pltpu.make_async_copy(
                            h_hbm.at[pl.ds(tok, PACK_ROWS)],
                            xbuf.at[slot, pl.ds(r * PACK_ROWS, PACK_ROWS)],
                            sem.at[slot, j]).start(priority=k % DMA_THREADS)
                    return carry
                lax.fori_loop(0, MOE_SUB // MOE_FETCH_UNROLL, rows, 0)

    @pl.when((i == 0) & (f == 0))
    def _():
        fetch(0, 0)

    @pl.when(f == 0)
    def _():
        for j in range(MOE_SUBS):
            @pl.when(j < nsub_ref[i])
            def _():
                blk = xbuf.at[slot, pl.ds(j * MOE_SUB * PACK_ROWS, MOE_SUB * PACK_ROWS)]
                pltpu.make_async_copy(blk, blk, sem.at[slot, j]).wait()
                for s in range(PACK_ROWS):
                    piece = xbuf[slot, pl.ds(j * MOE_SUB * PACK_ROWS + s, MOE_SUB, stride=PACK_ROWS), :]
                    xb_ref[sub_rows(j), s * LANES:(s + 1) * LANES] = piece.astype(BF16)
                acc_ref[sub_rows(j), :] = jnp.zeros((MOE_SUB, D_MODEL), F32)

        @pl.when(i + 1 < pl.num_programs(0))
        def _():
            fetch(i + 1, 1 - slot)

    @pl.when(nsub_ref[i] > 0)
    def _():
        wgb[...] = wg_ref[0, 0].astype(BF16)
        wub[...] = wu_ref[0, 0].astype(BF16)
        wdb[...] = wd_ref[0, 0].astype(BF16)
        for j in range(MOE_SUBS):
            @pl.when(j < nsub_ref[i])
            def _():
                xb = xb_ref[sub_rows(j), :]
                act = _silu(_dot(xb, wgb[...])) * _dot(xb, wub[...])
                acc_ref[sub_rows(j), :] += _dot(act.astype(BF16), wdb[...])

    @pl.when(f == MOE_FF_STEPS - 1)
    def _():
        for j in range(MOE_SUBS):
            @pl.when(j < nsub_ref[i])
            def _():
                y_ref[sub_rows(j), :] = acc_ref[sub_rows(j), :] * gs_ref[sub_rows(j), :]

            @pl.when(j >= nsub_ref[i])
            def _():
                y_ref[sub_rows(j), :] = jnp.zeros((MOE_SUB, D_MODEL), F32)


def _moe_experts(h, src, gs, tile_expert, tile_subs, w_gu, w_down, layer):
    rows = src.shape[0]
    tiles = rows // MOE_TILE
    last = MOE_FF_STEPS - 1
    fstep = lambda i, f, ns: jnp.where(ns[i] > 0, f, last)
    grid_spec = pltpu.PrefetchScalarGridSpec(
        num_scalar_prefetch=3,
        grid=(tiles, MOE_FF_STEPS),
        in_specs=[
            pl.BlockSpec(memory_space=pl.ANY),
            pl.BlockSpec((1, 1, D_MODEL, MOE_FF_CHUNK),
                         lambda i, f, te, ns, *_: (layer, te[i], 0, fstep(i, f, ns))),
            pl.BlockSpec((1, 1, D_MODEL, MOE_FF_CHUNK),
                         lambda i, f, te, ns, *_: (layer, te[i], 0, MOE_FF_STEPS + fstep(i, f, ns))),
            pl.BlockSpec((1, 1, MOE_FF_CHUNK, D_MODEL),
                         lambda i, f, te, ns, *_: (layer, te[i], fstep(i, f, ns), 0)),
            pl.BlockSpec((MOE_TILE, 1), lambda i, f, *_: (i, 0)),
        ],
        out_specs=pl.BlockSpec((MOE_TILE, D_MODEL), lambda i, f, *_: (i, 0)),
        scratch_shapes=[
            pltpu.VMEM((MOE_TILE, D_MODEL), F32),
            pltpu.VMEM((2, MOE_TILE * PACK_ROWS, LANES), F32),
            pltpu.VMEM((MOE_TILE, D_MODEL), BF16),
            pltpu.VMEM((D_MODEL, MOE_FF_CHUNK), BF16),
            pltpu.VMEM((D_MODEL, MOE_FF_CHUNK), BF16),
            pltpu.VMEM((MOE_FF_CHUNK, D_MODEL), BF16),
            pltpu.SemaphoreType.DMA((2, MOE_SUBS)),
        ],
    )
    return pl.pallas_call(
        _moe_body,
        grid_spec=grid_spec,
        out_shape=jax.ShapeDtypeStruct((rows, D_MODEL), F32),
        compiler_params=_params("arbitrary", "arbitrary"),
        name="moe_experts",
    )(tile_expert, tile_subs, src, h, w_gu, w_gu, w_down, gs)


def _route(idx, gates, n):
    assign = 2 * n
    tiles = -(-(assign + N_EXPERTS * (MOE_TILE - 1)) // MOE_TILE)
    experts = jnp.arange(N_EXPERTS, dtype=jnp.int32)
    ea = idx.reshape(assign)
    ga = gates.reshape(assign)
    onehot = (ea[:, None] == experts[None, :]).astype(jnp.int32)
    csum = jnp.cumsum(onehot, axis=0)
    counts = csum[-1]
    etiles = (counts + MOE_TILE - 1) // MOE_TILE
    tile_end = jnp.cumsum(etiles)
    tile_start = tile_end - etiles
    cnt_start = jnp.cumsum(counts) - counts
    pos = jnp.sum(onehot * ((tile_start * MOE_TILE)[None, :] + csum - 1), axis=1)
    token = jnp.arange(assign, dtype=jnp.int32) % n
    _, sorted_token, sorted_gate = lax.sort((ea, token, ga), num_keys=1, is_stable=True)
    ti = jnp.arange(tiles, dtype=jnp.int32)
    active = ti < tile_end[-1]
    te_raw = jnp.sum((ti[:, None] >= tile_end[None, :]).astype(jnp.int32), axis=1)
    last_expert = jnp.max(jnp.where(etiles > 0, experts, 0))
    te = jnp.where(active, jnp.minimum(te_raw, N_EXPERTS - 1), last_expert)
    tile_onehot = (te[:, None] == experts[None, :]).astype(jnp.int32)
    pick = lambda per_expert: jnp.sum(tile_onehot * per_expert[None, :], axis=1)
    tile_in_expert = ti - pick(tile_start)
    tv = jnp.where(active, jnp.clip(pick(counts) - tile_in_expert * MOE_TILE, 0, MOE_TILE), 0)
    first = pick(cnt_start) + tile_in_expert * MOE_TILE
    r = jnp.arange(MOE_TILE, dtype=jnp.int32)
    valid = (r[None, :] < tv[:, None]).reshape(-1)
    sidx = jnp.clip(first[:, None] + r[None, :], 0, assign - 1).reshape(-1)
    src = jnp.where(valid, jnp.take(sorted_token, sidx, mode="clip"), 0) * PACK_ROWS
    gs = jnp.where(valid, jnp.take(sorted_gate, sidx, mode="clip"), 0.0)
    return src, gs.reshape(-1, 1), te, (tv + MOE_SUB - 1) // MOE_SUB, pos


def _moe_layer(x, g, router, w_gu, w_down, layer, split=None):
    n = x.shape[0]
    h, idx, gates = _router(x, g, router)
    src, gs, te, subs, pos = _route(idx, gates, n)
    y = _moe_experts(h, src, gs, te, subs, w_gu, w_down, layer)
    y1 = jnp.take(y, pos[:n], axis=0, mode="clip")
    y2 = jnp.take(y, pos[n:], axis=0, mode="clip")
    if split is None:
        return x + (y1 + y2)
    return x[:split] + (y1[:split] + y2[:split]), x[split:] + (y1[split:] + y2[split:])


def kernel(x_prompt, x_sample, state_conv, cache_kv_w128, cache_kv_w512, cache_kv_w2048, norm_mix_g, norm_ffn_g, a_w_in, a_ln_g, a_ln_b, a_w_s, a_b_s, a_w_out, b_w_in, b_conv_w, b_w_out, c_w_qkv, c_q_norm_g, c_k_norm_g, c_w_out, f_w_gu, f_w_down, m_router, m_w_gu, m_w_down):
    batch, seq, _ = x_prompt.shape
    nb = x_sample.shape[0]
    n_prompt = batch * seq
    caches = (cache_kv_w128, cache_kv_w512, cache_kv_w2048)
    x = x_prompt.reshape(n_prompt, D_MODEL)
    x_sample_rows = x_sample.transpose(1, 0, 2).reshape(DEC_SEQ * nb, D_MODEL)
    depth = norm_mix_g.shape[0]
    sgu_v, conv_p, conv_s = [], [], []
    kv_p = [[] for _ in range(N_ATTN_GROUPS)]
    kv_s = [[] for _ in range(N_ATTN_GROUPS)]
    for i in range(depth):
        kind, j = i % 3, i // 3
        if kind == 0:
            x, v_rows = _sgu_layer(x, n_prompt, norm_mix_g[i], a_w_in[j], a_ln_g[j], a_ln_b[j],
                                   a_w_s[j], a_b_s[j], a_w_out[j],
                                   x_sample=x_sample_rows if i == 0 else None)
            sgu_v.append(v_rows.reshape(DEC_SEQ, nb, D_GATE).transpose(1, 0, 2))
        elif kind == 1:
            x, tail_p, tail_s = _conv_layer(x, n_prompt, batch, state_conv[j], norm_mix_g[i],
                                            b_w_in[j], b_conv_w[j], b_w_out[j])
            conv_p.append(tail_p)
            conv_s.append(tail_s)
        else:
            x, kvp, kvs = _attn_layer(x, n_prompt, batch, [c[j] for c in caches], norm_mix_g[i],
                                      c_w_qkv[j], c_q_norm_g[j], c_k_norm_g[j], c_w_out[j])
            for g in range(N_ATTN_GROUPS):
                kv_p[g].append(kvp[g])
                kv_s[g].append(kvs[g])
        e = i // 2
        if i % 2 == 0:
            x = _dense_ffn(x, norm_ffn_g[i], f_w_gu[e], f_w_down[e])
        else:
            x = _moe_layer(x, norm_ffn_g[i], m_router[e], m_w_gu, m_w_down, e,
                           split=n_prompt if i == depth - 1 else None)
    xp, xs = x if isinstance(x, tuple) else (x[:n_prompt], x[n_prompt:])
    y_prompt = xp.reshape(batch, seq, D_MODEL)
    y_sample = xs.reshape(DEC_SEQ, nb, D_MODEL).transpose(1, 0, 2)
    stack = lambda xs: jnp.stack(xs, axis=0)
    return (y_prompt, y_sample, stack(conv_p), stack(conv_s),
            stack(kv_p[0]), stack(kv_s[0]), stack(kv_p[1]), stack(kv_s[1]),
            stack(kv_p[2]), stack(kv_s[2]), stack(sgu_v))
```

```python
import functools

import numpy as np
import jax
import jax.numpy as jnp
from jax import lax
from jax.experimental import pallas as pl
from jax.experimental.pallas import tpu as pltpu

F32 = jnp.float32
BF16 = jnp.bfloat16

D_MODEL = 1024
EPS = 1e-6
CHUNK = 128
D_GATE = 2 * D_MODEL
SGU_GROUPS = 8
SGU_GDIM = D_GATE // SGU_GROUPS
D_CONV = D_MODEL
HEAD_DIM = 64
HEADS_PER_GROUP = 4
GROUP_DIM = HEADS_PER_GROUP * HEAD_DIM
WINDOWS = (128, 512, 2048)
DILATIONS = (1, 4, 16)
N_ATTN_GROUPS = 3
D_ATTN = N_ATTN_GROUPS * GROUP_DIM
ROPE_DIM = HEAD_DIM // 4
ROPE_THETA = 500000.0
ATTN_BLOCK = 128
D_FF = 2816
N_EXPERTS = 8
D_FF_EXPERT = 3584
DEC_SEQ = 4
PAST_LEN = 2048

SUBLANES = 8
LANES = 128
VMEM_LIMIT = 56 * 1024 * 1024

ROW_TILE = 512
FF_CHUNK = D_FF // 2
MOE_TILE = 1024
MOE_SUB = 256
MOE_SUBS = MOE_TILE // MOE_SUB
MOE_FF_CHUNK = 512
MOE_FF_STEPS = D_FF_EXPERT // MOE_FF_CHUNK
assert D_MODEL == SUBLANES * LANES


def _dot(a, b):
    return jnp.dot(a, b, preferred_element_type=F32)


def _dot_nt(a, b):
    return lax.dot_general(a, b, (((1,), (1,)), ((), ())), preferred_element_type=F32)


def _rms(x, g):
    return x * lax.rsqrt(jnp.mean(x * x, axis=-1, keepdims=True) + EPS) * g


def _gelu_tanh(x):
    c = np.float32(np.sqrt(2 / np.pi))
    return x * (0.5 * (1.0 + jnp.tanh(c * (x + np.float32(0.044715) * (x * x * x)))))


def _silu(x):
    return x * (1.0 / (1.0 + jnp.exp(-x)))


def _params(*sem):
    return pltpu.CompilerParams(dimension_semantics=sem, vmem_limit_bytes=VMEM_LIMIT)


def _resident(shape):
    zeros = (0,) * len(shape)
    return pl.BlockSpec(shape, lambda *_: zeros, pipeline_mode=pl.Buffered(1))


def _row_spec(width, first_tile=0):
    return pl.BlockSpec((ROW_TILE, width), lambda i: (i + first_tile, 0))


def _sgu_front(x, gmix, winv_ref, lng, lnb):
    h = _rms(x, gmix).astype(BF16)
    v = _gelu_tanh(_dot(h, winv_ref[...]))
    mu = jnp.mean(v, axis=-1, keepdims=True)
    vc = v - mu
    var = jnp.mean(vc * vc, axis=-1, keepdims=True)
    return h, vc * lax.rsqrt(var + EPS) * lng + lnb


def _sgu_back(x, h, winu_ref, s_ref, wout_ref, o_ref):
    u = _gelu_tanh(_dot(h, winu_ref[...]))
    o_ref[...] = x + _dot((u * s_ref[...]).astype(BF16), wout_ref[...])


def _sgu_prompt_body(x_ref, gmix_ref, winu_ref, winv_ref, lng_ref, lnb_ref, ws_ref, bs_ref,
                     wout_ref, o_ref, s_ref):
    x = x_ref[...]
    h, vn = _sgu_front(x, gmix_ref[...], winv_ref, lng_ref[...], lnb_ref[...])
    vb = vn.astype(BF16)
    row = lax.broadcasted_iota(jnp.int32, (CHUNK, CHUNK), 0)
    col = lax.broadcasted_iota(jnp.int32, (CHUNK, CHUNK), 1)
    for g in range(SGU_GROUPS):
        wm = jnp.where(row >= col, ws_ref[g], 0.0).astype(BF16)
        bias = jnp.concatenate([bs_ref[g]] * (SGU_GDIM // LANES), axis=1)
        cols = slice(g * SGU_GDIM, (g + 1) * SGU_GDIM)
        for c in range(ROW_TILE // CHUNK):
            rows = slice(c * CHUNK, (c + 1) * CHUNK)
            s_ref[rows, cols] = _dot(wm, vb[rows, cols]) + bias
    _sgu_back(x, h, winu_ref, s_ref, wout_ref, o_ref)


def _sgu_prompt_staging_body(x_ref, xs_ref, *rest):
    o_ref = rest[-2]
    last = pl.program_id(0) == pl.num_programs(0) - 1

    @pl.when(jnp.logical_not(last))
    def _():
        _sgu_prompt_body(x_ref, *rest)

    @pl.when(last)
    def _():
        o_ref[...] = xs_ref[...]


def _sgu_sample_body(w4_ref, b4_ref, x_ref, gmix_ref, winu_ref, winv_ref, lng_ref, lnb_ref,
                     wout_ref, o_ref, v_ref, s_ref):
    x = x_ref[...]
    nb = x.shape[0] // DEC_SEQ
    h, vn = _sgu_front(x, gmix_ref[...], winv_ref, lng_ref[...], lnb_ref[...])
    v_ref[...] = vn
    for g in range(SGU_GROUPS):
        cols = slice(g * SGU_GDIM, (g + 1) * SGU_GDIM)
        for t in range(DEC_SEQ):
            acc = w4_ref[g, t * DEC_SEQ] * vn[0:nb, cols]
            for j in range(1, t + 1):
                acc = acc + w4_ref[g, t * DEC_SEQ + j] * vn[j * nb:(j + 1) * nb, cols]
            s_ref[t * nb:(t + 1) * nb, cols] = acc + b4_ref[g, t]
    _sgu_back(x, h, winu_ref, s_ref, wout_ref, o_ref)


def _sgu_layer(x, n_prompt, gmix, w_in, ln_g, ln_b, w_s, b_s, w_out, x_sample=None):
    n = x.shape[0] if x_sample is None else x.shape[0] + x_sample.shape[0]
    n_sample = n - n_prompt
    assert n_sample == ROW_TILE and n_prompt % ROW_TILE == 0
    winu = w_in[:, :D_GATE].astype(BF16)
    winv = w_in[:, D_GATE:].astype(BF16)
    wout = w_out.astype(BF16)
    gmix = gmix.reshape(1, D_MODEL)
    lng = ln_g.reshape(1, D_GATE)
    lnb = ln_b.reshape(1, D_GATE)
    bs = jnp.broadcast_to(b_s[:, :, None], (SGU_GROUPS, CHUNK, LANES))
    weights = [_resident((1, D_MODEL)), _resident((D_MODEL, D_GATE)), _resident((D_MODEL, D_GATE)),
               _resident((1, D_GATE)), _resident((1, D_GATE))]
    prompt_tiles = n_prompt // ROW_TILE
    prompt_specs = weights + [
        _resident((SGU_GROUPS, CHUNK, CHUNK)), _resident((SGU_GROUPS, CHUNK, LANES)),
        _resident((D_GATE, D_MODEL))]
    prompt_args = (gmix, winu, winv, lng, lnb, w_s, bs, wout)
    if x_sample is None:
        body, grid, aliases = _sgu_prompt_body, prompt_tiles, {0: 0}
        prompt_specs = [_row_spec(D_MODEL)] + prompt_specs
        prompt_args = (x,) + prompt_args
    else:
        body, grid, aliases = _sgu_prompt_staging_body, prompt_tiles + 1, {}
        prompt_specs = [
            pl.BlockSpec((ROW_TILE, D_MODEL), lambda i: (jnp.minimum(i, prompt_tiles - 1), 0)),
            _resident((ROW_TILE, D_MODEL))] + prompt_specs
        prompt_args = (x, x_sample) + prompt_args
    x = pl.pallas_call(
        body,
        grid=(grid,),
        in_specs=prompt_specs,
        out_specs=_row_spec(D_MODEL),
        out_shape=jax.ShapeDtypeStruct((n, D_MODEL), F32),
        scratch_shapes=[pltpu.VMEM((ROW_TILE, D_GATE), F32)],
        input_output_aliases=aliases,
        compiler_params=_params("parallel"),
        name="sgu_prompt",
    )(*prompt_args)
    w4 = w_s[:, :DEC_SEQ, :DEC_SEQ].reshape(SGU_GROUPS, DEC_SEQ * DEC_SEQ)
    b4 = b_s[:, :DEC_SEQ]
    smem = pl.BlockSpec(memory_space=pltpu.SMEM)
    first = n_prompt // ROW_TILE
    x, v_rows = pl.pallas_call(
        _sgu_sample_body,
        grid=(1,),
        in_specs=[smem, smem, _row_spec(D_MODEL, first)] + weights + [_resident((D_GATE, D_MODEL))],
        out_specs=[_row_spec(D_MODEL, first), pl.BlockSpec((ROW_TILE, D_GATE), lambda i: (0, 0))],
        out_shape=[jax.ShapeDtypeStruct((n, D_MODEL), F32),
                   jax.ShapeDtypeStruct((n_sample, D_GATE), F32)],
        scratch_shapes=[pltpu.VMEM((ROW_TILE, D_GATE), F32)],
        input_output_aliases={2: 0},
        compiler_params=_params("arbitrary"),
        name="sgu_sample",
    )(w4, b4, x, gmix, winu, winv, lng, lnb, wout)
    return x, v_rows


def _conv_front(x, gmix, win_ref):
    h = _rms(x, gmix).astype(BF16)
    gate_b = _dot(h, win_ref[:, 0:D_CONV])
    gate_c = _dot(h, win_ref[:, D_CONV:2 * D_CONV])
    xi = _dot(h, win_ref[:, 2 * D_CONV:3 * D_CONV])
    return gate_b, gate_c * xi


def _conv_prompt_body(x_ref, gmix_ref, win_ref, cw_ref, wout_ref, o_ref, tail_ref, zbuf, *,
                      tiles_per_seq):
    @pl.when(pl.program_id(0) % tiles_per_seq == 0)
    def _():
        zbuf[0:SUBLANES, :] = jnp.zeros((SUBLANES, D_CONV), F32)

    x = x_ref[...]
    gate_b, z = _conv_front(x, gmix_ref[...], win_ref)
    zbuf[SUBLANES:SUBLANES + ROW_TILE, :] = z
    conv = (cw_ref[2:3, :] * z
            + cw_ref[1:2, :] * zbuf[SUBLANES - 1:SUBLANES - 1 + ROW_TILE, :]
            + cw_ref[0:1, :] * zbuf[SUBLANES - 2:SUBLANES - 2 + ROW_TILE, :])
    o_ref[...] = x + _dot((gate_b * conv).astype(BF16), wout_ref[...])
    tail = zbuf[ROW_TILE + SUBLANES - 2:ROW_TILE + SUBLANES, :]
    tail_ref[0] = tail
    zbuf[SUBLANES - 2:SUBLANES, :] = tail


def _conv_sample_body(x_ref, gmix_ref, win_ref, cw_ref, st_ref, wout_ref, o_ref, tail_ref):
    x = x_ref[...]
    nb = x.shape[0] // DEC_SEQ
    gate_b, z = _conv_front(x, gmix_ref[...], win_ref)
    z1 = jnp.concatenate([st_ref[1], z[:(DEC_SEQ - 1) * nb]], axis=0)
    z2 = jnp.concatenate([st_ref[0], st_ref[1], z[:(DEC_SEQ - 2) * nb]], axis=0)
    conv = cw_ref[2:3, :] * z + cw_ref[1:2, :] * z1 + cw_ref[0:1, :] * z2
    o_ref[...] = x + _dot((gate_b * conv).astype(BF16), wout_ref[...])
    tail_ref[...] = z[(DEC_SEQ - 2) * nb:]


def _conv_layer(x, n_prompt, batch, state, gmix, w_in, conv_w, w_out):
    n = x.shape[0]
    n_sample = n - n_prompt
    nb = n_sample // DEC_SEQ
    assert n_sample == ROW_TILE
    tiles_per_seq = n_prompt // batch // ROW_TILE
    win = w_in.astype(BF16)
    wout = w_out.astype(BF16)
    gmix = gmix.reshape(1, D_MODEL)
    weights = [_resident((1, D_MODEL)), _resident((D_MODEL, 3 * D_CONV)), _resident((3, D_CONV))]
    x, tail_p = pl.pallas_call(
        functools.partial(_conv_prompt_body, tiles_per_seq=tiles_per_seq),
        grid=(n_prompt // ROW_TILE,),
        in_specs=[_row_spec(D_MODEL)] + weights + [_resident((D_CONV, D_MODEL))],
        out_specs=[_row_spec(D_MODEL),
                   pl.BlockSpec((1, 2, D_CONV), lambda i: (i // tiles_per_seq, 0, 0))],
        out_shape=[jax.ShapeDtypeStruct((n, D_MODEL), F32),
                   jax.ShapeDtypeStruct((batch, 2, D_CONV), F32)],
        scratch_shapes=[pltpu.VMEM((ROW_TILE + SUBLANES, D_CONV), F32)],
        input_output_aliases={0: 0},
        compiler_params=_params("arbitrary"),
        name="conv_prompt",
    )(x, gmix, win, conv_w, wout)
    first = n_prompt // ROW_TILE
    st = state.transpose(1, 0, 2)
    x, tail_s = pl.pallas_call(
        _conv_sample_body,
        grid=(1,),
        in_specs=[_row_spec(D_MODEL, first)] + weights + [
            _resident((2, nb, D_CONV)), _resident((D_CONV, D_MODEL))],
        out_specs=[_row_spec(D_MODEL, first), pl.BlockSpec((2 * nb, D_CONV), lambda i: (0, 0))],
        out_shape=[jax.ShapeDtypeStruct((n, D_MODEL), F32),
                   jax.ShapeDtypeStruct((2 * nb, D_CONV), F32)],
        input_output_aliases={0: 0},
        compiler_params=_params("arbitrary"),
        name="conv_sample",
    )(x, gmix, win, conv_w, st, wout)
    return x, tail_p, tail_s.reshape(2, nb, D_CONV).transpose(1, 0, 2)


def _rope_tables(seq, n_sample_rows):
    nb = n_sample_rows // DEC_SEQ
    pos = jnp.concatenate([jnp.arange(seq, dtype=jnp.int32),
                           PAST_LEN + jnp.arange(n_sample_rows, dtype=jnp.int32) // nb])
    inv_freq = jnp.power(ROPE_THETA, -jnp.arange(0, ROPE_DIM, 2, dtype=F32) / ROPE_DIM)
    d = np.arange(LANES) % HEAD_DIM
    half = ROPE_DIM // 2
    ang = pos.astype(F32)[:, None] * inv_freq[d % half][None, :]
    cos, sin = jnp.cos(ang), jnp.sin(ang)
    c = jnp.where(d[None, :] < ROPE_DIM, cos, 1.0)
    s1 = jnp.where(d[None, :] < half, -sin, 0.0)
    s2 = jnp.where((d[None, :] >= half) & (d[None, :] < ROPE_DIM), sin, 0.0)
    return c, s1, s2


def _qkv_body(x_ref, gmix_ref, w_ref, gq_ref, gk_ref, c_ref, s1_ref, s2_ref, bm_ref,
              q_ref, k_ref, v_ref):
    h = _rms(x_ref[...], gmix_ref[...]).astype(BF16)
    c, s1, s2 = c_ref[...], s1_ref[...], s2_ref[...]
    bm = bm_ref[...]

    def head_norm_rope(t, gain, out_ref, scale):
        for a in range(D_ATTN // GROUP_DIM):
            ts = t[:, a * GROUP_DIM:(a + 1) * GROUP_DIM]
            sq = ts * ts
            hi = sq.astype(BF16)
            lo = (sq - hi.astype(F32)).astype(BF16)
            ms = _dot(hi, bm) + _dot(lo, bm)
            tn = ts * lax.rsqrt(ms + EPS) * gain[:, a * GROUP_DIM:(a + 1) * GROUP_DIM]
            for b in range(GROUP_DIM // LANES):
                th = tn[:, b * LANES:(b + 1) * LANES]
                rot = (th * c + pltpu.roll(th, LANES - ROPE_DIM // 2, 1) * s1
                       + pltpu.roll(th, ROPE_DIM // 2, 1) * s2)
                out_ref[a * (GROUP_DIM // LANES) + b] = rot if scale is None else rot * scale

    head_norm_rope(_dot(h, w_ref[:, 0:D_ATTN]), gq_ref[...], q_ref, np.float32(HEAD_DIM ** -0.5))
    head_norm_rope(_dot(h, w_ref[:, D_ATTN:2 * D_ATTN]), gk_ref[...], k_ref, None)
    v = _dot(h, w_ref[:, 2 * D_ATTN:3 * D_ATTN])
    for s in range(D_ATTN // LANES):
        v_ref[s] = v[:, s * LANES:(s + 1) * LANES]


def _qkv(x, n_prompt, seq, gmix, w_qkv, q_norm_g, k_norm_g):
    n = x.shape[0]
    n_sample = n - n_prompt
    tiles_per_seq = seq // ROW_TILE
    prompt_tiles = n_prompt // ROW_TILE
    c, s1, s2 = _rope_tables(seq, n_sample)
    heads = D_ATTN // HEAD_DIM
    slabs = D_ATTN // LANES
    gq = jnp.tile(q_norm_g, heads).reshape(1, D_ATTN)
    gk = jnp.tile(k_norm_g, heads).reshape(1, D_ATTN)
    lane_head = np.arange(GROUP_DIM) // HEAD_DIM
    bm = jnp.asarray((lane_head[:, None] == lane_head[None, :]) / HEAD_DIM, BF16)
    table = pl.BlockSpec(
        (ROW_TILE, LANES),
        lambda i: (jnp.where(i < prompt_tiles, i % tiles_per_seq, tiles_per_seq), 0))
    return pl.pallas_call(
        _qkv_body,
        grid=(n // ROW_TILE,),
        in_specs=[_row_spec(D_MODEL), _resident((1, D_MODEL)), _resident((D_MODEL, 3 * D_ATTN)),
                  _resident((1, D_ATTN)), _resident((1, D_ATTN)), table, table, table,
                  _resident((GROUP_DIM, GROUP_DIM))],
        out_specs=[pl.BlockSpec((slabs, ROW_TILE, LANES), lambda i: (0, i, 0))] * 3,
        out_shape=[jax.ShapeDtypeStruct((slabs, n, LANES), F32)] * 3,
        compiler_params=_params("parallel"),
        name="qkv",
    )(x, gmix.reshape(1, D_MODEL), w_qkv.astype(BF16), gq, gk, c, s1, s2, bm)


GROUP_SLABS = GROUP_DIM // LANES


def _attn_prompt_body(q_ref, kc_ref, kp_ref, vc_ref, vp_ref, o_ref, l_ref, *, dil, blocks):
    first_key = jnp.where(pl.program_id(1) == 0, ATTN_BLOCK, 0)
    lane = lax.broadcasted_iota(jnp.int32, (1, GROUP_DIM), 1)
    head_masks = [lane // HEAD_DIM == hd for hd in range(HEADS_PER_GROUP)]
    qi = lax.broadcasted_iota(jnp.int32, (ATTN_BLOCK, 2 * ATTN_BLOCK), 0)
    kj = lax.broadcasted_iota(jnp.int32, (ATTN_BLOCK, 2 * ATTN_BLOCK), 1)
    dist = qi + ATTN_BLOCK - kj
    band = (dist >= 0) & (dist <= ATTN_BLOCK)
    band_first = band & (kj >= first_key)

    def stream_rows(r, j):
        if dil == 1:
            return pl.ds(j * ATTN_BLOCK, ATTN_BLOCK)
        return pl.ds(r + j * ATTN_BLOCK * dil, ATTN_BLOCK, stride=dil)

    def load(ref, rows):
        return jnp.concatenate([ref[s, rows, :] for s in range(GROUP_SLABS)], axis=1)

    for r in range(dil):
        for j in range(blocks):
            rows = stream_rows(r, j)
            q = load(q_ref, rows).astype(BF16)
            if j == 0:
                k_prev, v_prev, mask = load(kp_ref, stream_rows(r, 0)), load(vp_ref, stream_rows(r, 0)), band_first
            else:
                prev = stream_rows(r, j - 1)
                k_prev, v_prev, mask = load(kc_ref, prev), load(vc_ref, prev), band
            kk = jnp.concatenate([k_prev, load(kc_ref, rows)], axis=0).astype(BF16)
            vv = jnp.concatenate([v_prev, load(vc_ref, rows)], axis=0).astype(BF16)
            q_heads = jnp.concatenate([jnp.where(hm, q, jnp.zeros_like(q)) for hm in head_masks], axis=0)
            s = jnp.where(jnp.concatenate([mask] * HEADS_PER_GROUP, axis=0), _dot_nt(q_heads, kk), -jnp.inf)
            m = jnp.max(s, axis=-1, keepdims=True)
            p = jnp.exp(s - m)
            l = jnp.sum(p, axis=-1, keepdims=True)
            pv = _dot(p.astype(BF16), vv) * (1.0 / l)
            lse = m + jnp.log(l)
            o_acc = jnp.zeros((ATTN_BLOCK, GROUP_DIM), F32)
            l_acc = jnp.zeros((ATTN_BLOCK, GROUP_DIM), F32)
            for hd, hm in enumerate(head_masks):
                part = slice(hd * ATTN_BLOCK, (hd + 1) * ATTN_BLOCK)
                o_acc = o_acc + jnp.where(hm, pv[part], 0.0)
                l_acc = l_acc + jnp.where(hm, lse[part], 0.0)
            for s in range(GROUP_SLABS):
                o_ref[s, rows, :] = o_acc[:, s * LANES:(s + 1) * LANES]
                l_ref[s, rows, :] = l_acc[:, s * LANES:(s + 1) * LANES]


def _attn_prompt(q, k, v, group, batch, seq):
    dil = DILATIONS[group]
    assert WINDOWS[group] // dil == ATTN_BLOCK
    span = ATTN_BLOCK * dil
    tile = max(ROW_TILE, span)
    tiles = seq // tile
    cur = pl.BlockSpec((GROUP_SLABS, tile, LANES), lambda b, t: (group, b * tiles + t, 0))
    prev = pl.BlockSpec(
        (GROUP_SLABS, span, LANES),
        lambda b, t: (group, jnp.maximum((b * tiles + t) * (tile // span) - 1, 0), 0))
    out = pl.BlockSpec((GROUP_SLABS, tile, LANES), lambda b, t: (0, b * tiles + t, 0))
    shape = jax.ShapeDtypeStruct((GROUP_SLABS, batch * seq, LANES), F32)
    return pl.pallas_call(
        functools.partial(_attn_prompt_body, dil=dil, blocks=tile // span),
        grid=(batch, tiles),
        in_specs=[cur, cur, prev, cur, prev],
        out_specs=[out, out],
        out_shape=[shape, shape],
        compiler_params=_params("parallel", "arbitrary"),
        name=f"attn_prompt_g{group}",
    )(q, k, k, v, v)


HEADS_PER_SLAB = LANES // HEAD_DIM


def _attn_sample_body(q_ref, k_ref, v_ref, c_ref, o_ref, lse_ref, *, dil, buf, seqs):
    rows = HEADS_PER_SLAB * SUBLANES
    lane = lax.broadcasted_iota(jnp.int32, (1, LANES), 1)
    head_masks = [lane // HEAD_DIM == hh for hh in range(HEADS_PER_SLAB)]

    def visibility(width):
        t = lax.broadcasted_iota(jnp.int32, (rows, width), 0) % SUBLANES
        c = lax.broadcasted_iota(jnp.int32, (rows, width), 1)
        return jnp.where(t < DEC_SEQ, t, t - DEC_SEQ), c

    tq, c = visibility(buf)
    vis_cache = (c >= tq) if dil == 1 else (c % dil == tq)
    tq, c = visibility(SUBLANES)
    vis_fresh = (c < DEC_SEQ) & ((c <= tq) if dil == 1 else (c == tq))

    def one(bi):
        for s in range(GROUP_SLABS):
            heads = slice(s * HEADS_PER_SLAB, (s + 1) * HEADS_PER_SLAB)
            q, k_new, v_new = q_ref[s, bi], k_ref[s, bi].astype(BF16), v_ref[s, bi]
            q_heads = jnp.concatenate([jnp.where(hm, q, 0.0) for hm in head_masks], axis=0).astype(BF16)
            k_cache = c_ref[bi, 0, heads].reshape(LANES, buf).astype(BF16)
            v_cache = c_ref[bi, 1, heads].reshape(LANES, buf).astype(BF16)
            s_cache = jnp.where(vis_cache, _dot(q_heads, k_cache), -jnp.inf)
            s_fresh = jnp.where(vis_fresh, _dot_nt(q_heads, k_new), -jnp.inf)
            m = jnp.maximum(jnp.max(s_cache, axis=1, keepdims=True), jnp.max(s_fresh, axis=1, keepdims=True))
            p_cache = jnp.exp(s_cache - m)
            p_fresh = jnp.exp(s_fresh - m)
            l = jnp.sum(p_cache, axis=1, keepdims=True) + jnp.sum(p_fresh, axis=1, keepdims=True)
            inv = 1.0 / l
            o = _dot_nt((p_cache * inv).astype(BF16), v_cache)
            w_fresh = p_fresh * inv
            for t2 in range(DEC_SEQ):
                o = o + w_fresh[:, t2:t2 + 1] * v_new[t2:t2 + 1, :]
            lse = jnp.broadcast_to(m + jnp.log(l), (rows, LANES))
            o_slab = jnp.zeros((SUBLANES, LANES), F32)
            lse_slab = jnp.zeros((SUBLANES, LANES), F32)
            for hh, hm in enumerate(head_masks):
                part = slice(hh * SUBLANES, (hh + 1) * SUBLANES)
                o_slab = o_slab + jnp.where(hm, o[part], 0.0)
                lse_slab = lse_slab + jnp.where(hm, lse[part], 0.0)
            o_ref[s, bi] = o_slab
            lse_ref[s, bi] = lse_slab

    def pair(i, carry):
        one(2 * i)
        one(2 * i + 1)
        return carry

    if seqs == 2:
        pair(0, 0)
    else:
        lax.fori_loop(0, seqs // 2, pair, 0)


def _sample_rows(slabs, n_prompt):
    part = slabs[:, n_prompt:]
    nb = part.shape[1] // DEC_SEQ
    part = part.reshape(part.shape[0], DEC_SEQ, nb, LANES).transpose(0, 2, 1, 3)
    return jnp.pad(part, ((0, 0), (0, 0), (0, SUBLANES - DEC_SEQ), (0, 0)))


def _attn_sample(q, k, v, cache, group):
    nb, buf = cache.shape[0], cache.shape[1]
    dil = DILATIONS[group]
    assert buf == ATTN_BLOCK * dil and WINDOWS[group] == buf and (dil == 1 or dil >= DEC_SEQ)
    seqs = max(2, 2 * PAST_LEN // buf)
    fresh = pl.BlockSpec((GROUP_SLABS, seqs, SUBLANES, LANES), lambda i: (group, i, 0, 0))
    out = pl.BlockSpec((GROUP_SLABS, seqs, SUBLANES, LANES), lambda i: (0, i, 0, 0))
    shape = jax.ShapeDtypeStruct((GROUP_SLABS, nb, SUBLANES, LANES), F32)
    o, lse = pl.pallas_call(
        functools.partial(_attn_sample_body, dil=dil, buf=buf, seqs=seqs),
        grid=(nb // seqs,),
        in_specs=[fresh, fresh, fresh,
                  pl.BlockSpec((seqs, 2, HEADS_PER_GROUP, HEAD_DIM, buf), lambda i: (i, 0, 0, 0, 0))],
        out_specs=[out, out],
        out_shape=[shape, shape],
        compiler_params=_params("parallel"),
        name=f"attn_sample_g{group}",
    )(q, k, v, cache.transpose(0, 2, 3, 4, 1))
    rows = lambda a: a[:, :, :DEC_SEQ].transpose(0, 2, 1, 3).reshape(GROUP_SLABS, DEC_SEQ * nb, LANES)
    return rows(o), rows(lse)


def _from_slabs(slabs, group):
    part = slabs[group * GROUP_SLABS:(group + 1) * GROUP_SLABS]
    return part.transpose(1, 0, 2).reshape(part.shape[1], GROUP_DIM)


def _merge_body(x_ref, o0_ref, o1_ref, o2_ref, l0_ref, l1_ref, l2_ref, w_ref, out_ref):
    rows = lambda ref: jnp.concatenate([ref[s] for s in range(GROUP_SLABS)], axis=1)
    l0, l1, l2 = rows(l0_ref), rows(l1_ref), rows(l2_ref)
    m = jnp.maximum(jnp.maximum(l0, l1), l2)
    e0, e1, e2 = jnp.exp(l0 - m), jnp.exp(l1 - m), jnp.exp(l2 - m)
    inv = 1.0 / (e0 + e1 + e2)
    y = x_ref[...]
    for g, (o_ref, e) in enumerate(((o0_ref, e0), (o1_ref, e1), (o2_ref, e2))):
        y = y + _dot((rows(o_ref) * (e * inv)).astype(BF16),
                     w_ref[g * GROUP_DIM:(g + 1) * GROUP_DIM, :])
    out_ref[...] = y


def _merge_out_proj(x, outs, lses, w_out, first_tile, tiles):
    n = x.shape[0]
    part = pl.BlockSpec((GROUP_SLABS, ROW_TILE, LANES), lambda i: (0, i, 0))
    return pl.pallas_call(
        _merge_body,
        grid=(tiles,),
        in_specs=[_row_spec(D_MODEL, first_tile)] + [part] * 6 + [_resident((D_ATTN, D_MODEL))],
        out_specs=_row_spec(D_MODEL, first_tile),
        out_shape=jax.ShapeDtypeStruct((n, D_MODEL), F32),
        input_output_aliases={0: 0},
        compiler_params=_params("parallel"),
        name="attn_merge_out_proj",
    )(x, *outs, *lses, w_out)


def _attn_layer(x, n_prompt, batch, caches, gmix, w_qkv, q_norm_g, k_norm_g, w_out):
    n = x.shape[0]
    seq = n_prompt // batch
    nb = (n - n_prompt) // DEC_SEQ
    q, k, v = _qkv(x, n_prompt, seq, gmix, w_qkv, q_norm_g, k_norm_g)
    ks, vs = k[:, n_prompt:], v[:, n_prompt:]
    q_new, k_new, v_new = (_sample_rows(a, n_prompt) for a in (q, k, v))
    outs_p, lses_p, outs_s, lses_s, kv_p, kv_s = [], [], [], [], [], []
    for g in range(N_ATTN_GROUPS):
        o, lse = _attn_prompt(q, k, v, g, batch, seq)
        outs_p.append(o)
        lses_p.append(lse)
        o, lse = _attn_sample(q_new, k_new, v_new, caches[g], g)
        outs_s.append(o)
        lses_s.append(lse)
        k_rows, v_rows = _from_slabs(ks, g), _from_slabs(vs, g)
        keep = min(WINDOWS[g], seq)
        heads = (HEADS_PER_GROUP, HEAD_DIM)
        tail = lambda a: _from_slabs(jnp.concatenate(
            [a[g * GROUP_SLABS:(g + 1) * GROUP_SLABS, (b + 1) * seq - keep:(b + 1) * seq]
             for b in range(batch)], axis=1), 0)
        kv_p.append(jnp.stack([tail(k), tail(v)], axis=1).reshape(batch, keep, 2, *heads))
        kv_s.append(jnp.stack([k_rows, v_rows], axis=1).reshape(DEC_SEQ, nb, 2, *heads)
                    .transpose(1, 0, 2, 3, 4))
    wout = w_out.astype(BF16)
    x = _merge_out_proj(x, outs_p, lses_p, wout, 0, n_prompt // ROW_TILE)
    x = _merge_out_proj(x, outs_s, lses_s, wout, n_prompt // ROW_TILE, (n - n_prompt) // ROW_TILE)
    return x, kv_p, kv_s


def _ffn_body(x_ref, g_ref, wg_ref, wu_ref, wd_ref, o_ref):
    x = x_ref[...]
    h = _rms(x, g_ref[...]).astype(BF16)
    y = x
    for c in range(D_FF // FF_CHUNK):
        cols = slice(c * FF_CHUNK, (c + 1) * FF_CHUNK)
        act = _silu(_dot(h, wg_ref[:, cols])) * _dot(h, wu_ref[:, cols])
        y = y + _dot(act.astype(BF16), wd_ref[cols, :])
    o_ref[...] = y


def _dense_ffn(x, g, w_gu, w_down):
    n = x.shape[0]
    wgu = w_gu.astype(BF16)
    half = pl.Buffered(1)
    return pl.pallas_call(
        _ffn_body,
        grid=(n // ROW_TILE,),
        in_specs=[_row_spec(D_MODEL), _resident((1, D_MODEL)),
                  pl.BlockSpec((D_MODEL, D_FF), lambda i: (0, 0), pipeline_mode=half),
                  pl.BlockSpec((D_MODEL, D_FF), lambda i: (0, 1), pipeline_mode=half),
                  _resident((D_FF, D_MODEL))],
        out_specs=_row_spec(D_MODEL),
        out_shape=jax.ShapeDtypeStruct((n, D_MODEL), F32),
        input_output_aliases={0: 0},
        compiler_params=_params("parallel"),
        name="dense_ffn",
    )(x, g.reshape(1, D_MODEL), wgu, wgu, w_down.astype(BF16))


PACK_ROWS = D_MODEL // LANES


def _router_body(x_ref, g_ref, r_ref, h_ref, idx_ref, gate_ref):
    h = _rms(x_ref[...], g_ref[...])
    for s in range(PACK_ROWS):
        h_ref[pl.ds(s, ROW_TILE, stride=PACK_ROWS), :] = h[:, s * LANES:(s + 1) * LANES]
    logits = jnp.dot(h, r_ref[...], precision=lax.Precision.HIGHEST, preferred_element_type=F32)
    lt = logits.T[:N_EXPERTS, :]
    e = lax.broadcasted_iota(jnp.int32, lt.shape, 0)
    m1 = jnp.max(lt, axis=0, keepdims=True)
    e1 = jnp.min(jnp.where(lt == m1, e, N_EXPERTS), axis=0, keepdims=True)
    rest = jnp.where(e == e1, -jnp.inf, lt)
    m2 = jnp.max(rest, axis=0, keepdims=True)
    e2 = jnp.min(jnp.where(rest == m2, e, N_EXPERTS), axis=0, keepdims=True)
    a = jnp.exp(m2 - m1)
    inv = 1.0 / (1.0 + a)
    idx_ref[...] = jnp.concatenate([e1, e2], axis=0)
    gate_ref[...] = jnp.concatenate([inv, a * inv], axis=0)


def _router(x, g, router):
    n = x.shape[0]
    r = jnp.pad(router, ((0, 0), (0, LANES - N_EXPERTS)))
    pair = pl.BlockSpec((2, ROW_TILE), lambda i: (0, i))
    return pl.pallas_call(
        _router_body,
        grid=(n // ROW_TILE,),
        in_specs=[_row_spec(D_MODEL), _resident((1, D_MODEL)), _resident((D_MODEL, LANES))],
        out_specs=[pl.BlockSpec((ROW_TILE * PACK_ROWS, LANES), lambda i: (i, 0)), pair, pair],
        out_shape=[jax.ShapeDtypeStruct((n * PACK_ROWS, LANES), F32),
                   jax.ShapeDtypeStruct((2, n), jnp.int32),
                   jax.ShapeDtypeStruct((2, n), F32)],
        compiler_params=_params("parallel"),
        name="moe_router",
    )(x, g.reshape(1, D_MODEL), r)


def _moe_body(te_ref, nsub_ref, *refs, first_tile, gathered):
    del te_ref
    if gathered:
        xs_ref, wg_ref, wu_ref, wd_ref, gs_ref, _, y_ref, acc_ref, xb_ref, wgb, wub, wdb = refs
    else:
        (src_ref, h_hbm, wg_ref, wu_ref, wd_ref, gs_ref, y_ref,
         acc_ref, xbuf, xb_ref, wgb, wub, wdb, sem) = refs
    i, f = pl.program_id(0) + first_tile, pl.program_id(1)
    slot = i % 2

    def sub_rows(j):
        return pl.ds(j * MOE_SUB, MOE_SUB)

    def take_rows(tokens, j):
        for s in range(PACK_ROWS):
            piece = tokens(pl.ds(j * MOE_SUB * PACK_ROWS + s, MOE_SUB, stride=PACK_ROWS))
            xb_ref[sub_rows(j), s * LANES:(s + 1) * LANES] = piece.astype(BF16)
        acc_ref[sub_rows(j), :] = jnp.zeros((MOE_SUB, D_MODEL), F32)

    def fetch(tile, slot):
        for j in range(MOE_SUBS):
            @pl.when(j < nsub_ref[tile])
            def _():
                def row(r, carry):
                    tok = pl.multiple_of(src_ref[tile * MOE_TILE + j * MOE_SUB + r], PACK_ROWS)
                    pltpu.make_async_copy(
                        h_hbm.at[pl.ds(tok, PACK_ROWS)],
                        xbuf.at[slot, pl.ds((j * MOE_SUB + r) * PACK_ROWS, PACK_ROWS)],
                        sem.at[slot, j]).start()
                    return carry
                lax.fori_loop(0, MOE_SUB, row, 0, unroll=8)

    if gathered:
        @pl.when(f == 0)
        def _():
            for j in range(MOE_SUBS):
                @pl.when(j < nsub_ref[i])
                def _():
                    take_rows(lambda rows: xs_ref[rows, :], j)
    else:
        @pl.when((i == 0) & (f == 0))
        def _():
            fetch(0, 0)

        @pl.when(f == 0)
        def _():
            for j in range(MOE_SUBS):
                @pl.when(j < nsub_ref[i])
                def _():
                    blk = xbuf.at[slot, pl.ds(j * MOE_SUB * PACK_ROWS, MOE_SUB * PACK_ROWS)]
                    pltpu.make_async_copy(blk, blk, sem.at[slot, j]).wait()
                    take_rows(lambda rows: xbuf[slot, rows, :], j)

            @pl.when(i + 1 < pl.num_programs(0))
            def _():
                fetch(i + 1, 1 - slot)

    @pl.when(nsub_ref[i] > 0)
    def _():
        wgb[...] = wg_ref[0, 0].astype(BF16)
        wub[...] = wu_ref[0, 0].astype(BF16)
        wdb[...] = wd_ref[0, 0].astype(BF16)
        for j in range(MOE_SUBS):
            @pl.when(j < nsub_ref[i])
            def _():
                xb = xb_ref[sub_rows(j), :]
                act = _silu(_dot(xb, wgb[...])) * _dot(xb, wub[...])
                acc_ref[sub_rows(j), :] += _dot(act.astype(BF16), wdb[...])

    @pl.when(f == MOE_FF_STEPS - 1)
    def _():
        for j in range(MOE_SUBS):
            @pl.when(j < nsub_ref[i])
            def _():
                y_ref[sub_rows(j), :] = acc_ref[sub_rows(j), :] * gs_ref[sub_rows(j), :]

            @pl.when(j >= nsub_ref[i])
            def _():
                y_ref[sub_rows(j), :] = jnp.zeros((MOE_SUB, D_MODEL), F32)


def _moe_experts(h, src, gs, tile_expert, tile_subs, w_gu, w_down, layer):
    rows = src.shape[0]
    tiles = rows // MOE_TILE
    lead = tiles - tiles // 2
    last = MOE_FF_STEPS - 1
    n = h.shape[0] // PACK_ROWS
    tail_tokens = src[lead * MOE_TILE:] // PACK_ROWS
    xs_tail = jnp.take(h.reshape(n, PACK_ROWS, LANES), tail_tokens, axis=0, mode="clip")
    xs_tail = xs_tail.reshape((tiles - lead) * MOE_TILE * PACK_ROWS, LANES)
    ti = jnp.arange(tiles, dtype=jnp.int32)
    lead_subs = jnp.where(ti < lead, tile_subs, 0)
    lead_expert = jnp.where(ti < lead, tile_expert, tile_expert[lead - 1])

    def common_specs(first):
        tile = lambda i: i + first
        fstep = lambda i, f, ns: jnp.where(ns[tile(i)] > 0, f, last)
        return [
            pl.BlockSpec((1, 1, D_MODEL, MOE_FF_CHUNK),
                         lambda i, f, te, ns, *_: (layer, te[tile(i)], 0, fstep(i, f, ns))),
            pl.BlockSpec((1, 1, D_MODEL, MOE_FF_CHUNK),
                         lambda i, f, te, ns, *_: (layer, te[tile(i)], 0, MOE_FF_STEPS + fstep(i, f, ns))),
            pl.BlockSpec((1, 1, MOE_FF_CHUNK, D_MODEL),
                         lambda i, f, te, ns, *_: (layer, te[tile(i)], fstep(i, f, ns), 0)),
            pl.BlockSpec((MOE_TILE, 1), lambda i, f, *_: (tile(i), 0)),
        ]

    common_scratch = [
        pltpu.VMEM((MOE_TILE, D_MODEL), BF16),
        pltpu.VMEM((D_MODEL, MOE_FF_CHUNK), BF16),
        pltpu.VMEM((D_MODEL, MOE_FF_CHUNK), BF16),
        pltpu.VMEM((MOE_FF_CHUNK, D_MODEL), BF16),
    ]
    out_shape = jax.ShapeDtypeStruct((rows, D_MODEL), F32)
    y = pl.pallas_call(
        functools.partial(_moe_body, first_tile=0, gathered=False),
        grid_spec=pltpu.PrefetchScalarGridSpec(
            num_scalar_prefetch=3,
            grid=(tiles, MOE_FF_STEPS),
            in_specs=[pl.BlockSpec(memory_space=pl.ANY)] + common_specs(0),
            out_specs=pl.BlockSpec((MOE_TILE, D_MODEL), lambda i, f, *_: (i, 0)),
            scratch_shapes=[pltpu.VMEM((MOE_TILE, D_MODEL), F32),
                            pltpu.VMEM((2, MOE_TILE * PACK_ROWS, LANES), F32)] + common_scratch
            + [pltpu.SemaphoreType.DMA((2, MOE_SUBS))],
        ),
        out_shape=out_shape,
        compiler_params=_params("arbitrary", "arbitrary"),
        name="moe_experts_lead",
    )(lead_expert, lead_subs, src, h, w_gu, w_gu, w_down, gs)
    return pl.pallas_call(
        functools.partial(_moe_body, first_tile=lead, gathered=True),
        grid_spec=pltpu.PrefetchScalarGridSpec(
            num_scalar_prefetch=2,
            grid=(tiles - lead, MOE_FF_STEPS),
            in_specs=[pl.BlockSpec((MOE_TILE * PACK_ROWS, LANES), lambda i, f, *_: (i, 0))]
            + common_specs(lead) + [pl.BlockSpec(memory_space=pl.ANY)],
            out_specs=pl.BlockSpec((MOE_TILE, D_MODEL), lambda i, f, *_: (i + lead, 0)),
            scratch_shapes=[pltpu.VMEM((MOE_TILE, D_MODEL), F32)] + common_scratch,
        ),
        out_shape=out_shape,
        input_output_aliases={7: 0},
        compiler_params=_params("arbitrary", "arbitrary"),
        name="moe_experts_tail",
    )(tile_expert, tile_subs, xs_tail, w_gu, w_gu, w_down, gs, y)


def _route(idx, gates, n):
    assign = 2 * n
    tiles = -(-(assign + N_EXPERTS * (MOE_TILE - 1)) // MOE_TILE)
    experts = jnp.arange(N_EXPERTS, dtype=jnp.int32)
    ea = idx.reshape(assign)
    ga = gates.reshape(assign)
    onehot = (ea[:, None] == experts[None, :]).astype(jnp.int32)
    csum = jnp.cumsum(onehot, axis=0)
    counts = csum[-1]
    etiles = (counts + MOE_TILE - 1) // MOE_TILE
    tile_end = jnp.cumsum(etiles)
    tile_start = tile_end - etiles
    cnt_start = jnp.cumsum(counts) - counts
    pos = jnp.sum(onehot * ((tile_start * MOE_TILE)[None, :] + csum - 1), axis=1)
    token = jnp.arange(assign, dtype=jnp.int32) % n
    _, sorted_token, sorted_gate = lax.sort((ea, token, ga), num_keys=1, is_stable=True)
    ti = jnp.arange(tiles, dtype=jnp.int32)
    active = ti < tile_end[-1]
    te_raw = jnp.sum((ti[:, None] >= tile_end[None, :]).astype(jnp.int32), axis=1)
    last_expert = jnp.max(jnp.where(etiles > 0, experts, 0))
    te = jnp.where(active, jnp.minimum(te_raw, N_EXPERTS - 1), last_expert)
    tile_onehot = (te[:, None] == experts[None, :]).astype(jnp.int32)
    pick = lambda per_expert: jnp.sum(tile_onehot * per_expert[None, :], axis=1)
    tile_in_expert = ti - pick(tile_start)
    tv = jnp.where(active, jnp.clip(pick(counts) - tile_in_expert * MOE_TILE, 0, MOE_TILE), 0)
    first = pick(cnt_start) + tile_in_expert * MOE_TILE
    r = jnp.arange(MOE_TILE, dtype=jnp.int32)
    valid = (r[None, :] < tv[:, None]).reshape(-1)
    sidx = jnp.clip(first[:, None] + r[None, :], 0, assign - 1).reshape(-1)
    src = jnp.where(valid, jnp.take(sorted_token, sidx, mode="clip"), 0) * PACK_ROWS
    gs = jnp.where(valid, jnp.take(sorted_gate, sidx, mode="clip"), 0.0)
    return src, gs.reshape(-1, 1), te, (tv + MOE_SUB - 1) // MOE_SUB, pos


def _moe_layer(x, g, router, w_gu, w_down, layer, split=None):
    n = x.shape[0]
    h, idx, gates = _router(x, g, router)
    src, gs, te, subs, pos = _route(idx, gates, n)
    y = _moe_experts(h, src, gs, te, subs, w_gu, w_down, layer)
    y1 = jnp.take(y, pos[:n], axis=0, mode="clip")
    y2 = jnp.take(y, pos[n:], axis=0, mode="clip")
    if split is None:
        return x + (y1 + y2)
    return x[:split] + (y1[:split] + y2[:split]), x[split:] + (y1[split:] + y2[split:])


def kernel(x_prompt, x_sample, state_conv, cache_kv_w128, cache_kv_w512, cache_kv_w2048, norm_mix_g, norm_ffn_g, a_w_in, a_ln_g, a_ln_b, a_w_s, a_b_s, a_w_out, b_w_in, b_conv_w, b_w_out, c_w_qkv, c_q_norm_g, c_k_norm_g, c_w_out, f_w_gu, f_w_down, m_router, m_w_gu, m_w_down):
    batch, seq, _ = x_prompt.shape
    nb = x_sample.shape[0]
    n_prompt = batch * seq
    caches = (cache_kv_w128, cache_kv_w512, cache_kv_w2048)
    x = x_prompt.reshape(n_prompt, D_MODEL)
    x_sample_rows = x_sample.transpose(1, 0, 2).reshape(DEC_SEQ * nb, D_MODEL)
    depth = norm_mix_g.shape[0]
    sgu_v, conv_p, conv_s = [], [], []
    kv_p = [[] for _ in range(N_ATTN_GROUPS)]
    kv_s = [[] for _ in range(N_ATTN_GROUPS)]
    for i in range(depth):
        kind, j = i % 3, i // 3
        if kind == 0:
            x, v_rows = _sgu_layer(x, n_prompt, norm_mix_g[i], a_w_in[j], a_ln_g[j], a_ln_b[j],
                                   a_w_s[j], a_b_s[j], a_w_out[j],
                                   x_sample=x_sample_rows if i == 0 else None)
            sgu_v.append(v_rows.reshape(DEC_SEQ, nb, D_GATE).transpose(1, 0, 2))
        elif kind == 1:
            x, tail_p, tail_s = _conv_layer(x, n_prompt, batch, state_conv[j], norm_mix_g[i],
                                            b_w_in[j], b_conv_w[j], b_w_out[j])
            conv_p.append(tail_p)
            conv_s.append(tail_s)
        else:
            x, kvp, kvs = _attn_layer(x, n_prompt, batch, [c[j] for c in caches], norm_mix_g[i],
                                      c_w_qkv[j], c_q_norm_g[j], c_k_norm_g[j], c_w_out[j])
            for g in range(N_ATTN_GROUPS):
                kv_p[g].append(kvp[g])
                kv_s[g].append(kvs[g])
        e = i // 2
        if i % 2 == 0:
            x = _dense_ffn(x, norm_ffn_g[i], f_w_gu[e], f_w_down[e])
        else:
            x = _moe_layer(x, norm_ffn_g[i], m_router[e], m_w_gu, m_w_down, e,
                           split=n_prompt if i == depth - 1 else None)
    xp, xs = x if isinstance(x, tuple) else (x[:n_prompt], x[n_prompt:])
    y_prompt = xp.reshape(batch, seq, D_MODEL)
    y_sample = xs.reshape(DEC_SEQ, nb, D_MODEL).transpose(1, 0, 2)
    stack = lambda xs: jnp.stack(xs, axis=0)
    return (y_prompt, y_sample, stack(conv_p), stack(conv_s),
            stack(kv_p[0]), stack(kv_s[0]), stack(kv_p[1]), stack(kv_s[1]),
            stack(kv_p[2]), stack(kv_s[2]), stack(sgu_v))
```

```python
import functools

import numpy as np
import jax
import jax.numpy as jnp
from jax import lax
from jax.experimental import pallas as pl
from jax.experimental.pallas import tpu as pltpu

F32 = jnp.float32
BF16 = jnp.bfloat16

D_MODEL = 1024
EPS = 1e-6
CHUNK = 128
D_GATE = 2 * D_MODEL
SGU_GROUPS = 8
SGU_GDIM = D_GATE // SGU_GROUPS
D_CONV = D_MODEL
HEAD_DIM = 64
HEADS_PER_GROUP = 4
GROUP_DIM = HEADS_PER_GROUP * HEAD_DIM
WINDOWS = (128, 512, 2048)
DILATIONS = (1, 4, 16)
N_ATTN_GROUPS = 3
D_ATTN = N_ATTN_GROUPS * GROUP_DIM
ROPE_DIM = HEAD_DIM // 4
ROPE_THETA = 500000.0
ATTN_BLOCK = 128
D_FF = 2816
N_EXPERTS = 8
D_FF_EXPERT = 3584
DEC_SEQ = 4
PAST_LEN = 2048

SUBLANES = 8
LANES = 128
VMEM_LIMIT = 56 * 1024 * 1024

ROW_TILE = 512
FF_CHUNK = D_FF // 2
MOE_TILE = 1024
MOE_SUB = 256
MOE_SUBS = MOE_TILE // MOE_SUB
MOE_FF_CHUNK = 512
MOE_FF_STEPS = D_FF_EXPERT // MOE_FF_CHUNK
assert D_MODEL == SUBLANES * LANES


def _dot(a, b):
    return jnp.dot(a, b, preferred_element_type=F32)


def _dot_nt(a, b):
    return lax.dot_general(a, b, (((1,), (1,)), ((), ())), preferred_element_type=F32)


def _rms(x, g):
    return x * lax.rsqrt(jnp.mean(x * x, axis=-1, keepdims=True) + EPS) * g


def _gelu_tanh(x):
    c = np.float32(np.sqrt(2 / np.pi))
    return x * (0.5 * (1.0 + jnp.tanh(c * (x + np.float32(0.044715) * (x * x * x)))))


def _silu(x):
    return x * (1.0 / (1.0 + jnp.exp(-x)))


def _params(*sem):
    return pltpu.CompilerParams(dimension_semantics=sem, vmem_limit_bytes=VMEM_LIMIT)


def _resident(shape):
    zeros = (0,) * len(shape)
    return pl.BlockSpec(shape, lambda *_: zeros, pipeline_mode=pl.Buffered(1))


def _row_spec(width, first_tile=0):
    return pl.BlockSpec((ROW_TILE, width), lambda i: (i + first_tile, 0))


def _sgu_front(x, gmix, winv_ref, lng, lnb):
    h = _rms(x, gmix).astype(BF16)
    v = _gelu_tanh(_dot(h, winv_ref[...]))
    mu = jnp.mean(v, axis=-1, keepdims=True)
    vc = v - mu
    var = jnp.mean(vc * vc, axis=-1, keepdims=True)
    return h, vc * lax.rsqrt(var + EPS) * lng + lnb


def _sgu_back(x, h, winu_ref, s_ref, wout_ref, o_ref):
    u = _gelu_tanh(_dot(h, winu_ref[...]))
    o_ref[...] = x + _dot((u * s_ref[...]).astype(BF16), wout_ref[...])


def _sgu_prompt_body(x_ref, gmix_ref, winu_ref, winv_ref, lng_ref, lnb_ref, ws_ref, bs_ref,
                     wout_ref, o_ref, s_ref):
    x = x_ref[...]
    h, vn = _sgu_front(x, gmix_ref[...], winv_ref, lng_ref[...], lnb_ref[...])
    vb = vn.astype(BF16)
    row = lax.broadcasted_iota(jnp.int32, (CHUNK, CHUNK), 0)
    col = lax.broadcasted_iota(jnp.int32, (CHUNK, CHUNK), 1)
    for g in range(SGU_GROUPS):
        wm = jnp.where(row >= col, ws_ref[g], 0.0).astype(BF16)
        bias = jnp.concatenate([bs_ref[g]] * (SGU_GDIM // LANES), axis=1)
        cols = slice(g * SGU_GDIM, (g + 1) * SGU_GDIM)
        for c in range(ROW_TILE // CHUNK):
            rows = slice(c * CHUNK, (c + 1) * CHUNK)
            s_ref[rows, cols] = _dot(wm, vb[rows, cols]) + bias
    _sgu_back(x, h, winu_ref, s_ref, wout_ref, o_ref)


def _sgu_prompt_staging_body(x_ref, xs_ref, *rest):
    o_ref = rest[-2]
    last = pl.program_id(0) == pl.num_programs(0) - 1

    @pl.when(jnp.logical_not(last))
    def _():
        _sgu_prompt_body(x_ref, *rest)

    @pl.when(last)
    def _():
        o_ref[...] = xs_ref[...]


def _sgu_sample_body(w4_ref, b4_ref, x_ref, gmix_ref, winu_ref, winv_ref, lng_ref, lnb_ref,
                     wout_ref, o_ref, v_ref, s_ref):
    x = x_ref[...]
    nb = x.shape[0] // DEC_SEQ
    h, vn = _sgu_front(x, gmix_ref[...], winv_ref, lng_ref[...], lnb_ref[...])
    v_ref[...] = vn
    for g in range(SGU_GROUPS):
        cols = slice(g * SGU_GDIM, (g + 1) * SGU_GDIM)
        for t in range(DEC_SEQ):
            acc = w4_ref[g, t * DEC_SEQ] * vn[0:nb, cols]
            for j in range(1, t + 1):
                acc = acc + w4_ref[g, t * DEC_SEQ + j] * vn[j * nb:(j + 1) * nb, cols]
            s_ref[t * nb:(t + 1) * nb, cols] = acc + b4_ref[g, t]
    _sgu_back(x, h, winu_ref, s_ref, wout_ref, o_ref)


def _sgu_layer(x, n_prompt, gmix, w_in, ln_g, ln_b, w_s, b_s, w_out, x_sample=None):
    n = x.shape[0] if x_sample is None else x.shape[0] + x_sample.shape[0]
    n_sample = n - n_prompt
    assert n_sample == ROW_TILE and n_prompt % ROW_TILE == 0
    winu = w_in[:, :D_GATE].astype(BF16)
    winv = w_in[:, D_GATE:].astype(BF16)
    wout = w_out.astype(BF16)
    gmix = gmix.reshape(1, D_MODEL)
    lng = ln_g.reshape(1, D_GATE)
    lnb = ln_b.reshape(1, D_GATE)
    bs = jnp.broadcast_to(b_s[:, :, None], (SGU_GROUPS, CHUNK, LANES))
    weights = [_resident((1, D_MODEL)), _resident((D_MODEL, D_GATE)), _resident((D_MODEL, D_GATE)),
               _resident((1, D_GATE)), _resident((1, D_GATE))]
    prompt_tiles = n_prompt // ROW_TILE
    prompt_specs = weights + [
        _resident((SGU_GROUPS, CHUNK, CHUNK)), _resident((SGU_GROUPS, CHUNK, LANES)),
        _resident((D_GATE, D_MODEL))]
    prompt_args = (gmix, winu, winv, lng, lnb, w_s, bs, wout)
    if x_sample is None:
        body, grid, aliases = _sgu_prompt_body, prompt_tiles, {0: 0}
        prompt_specs = [_row_spec(D_MODEL)] + prompt_specs
        prompt_args = (x,) + prompt_args
    else:
        body, grid, aliases = _sgu_prompt_staging_body, prompt_tiles + 1, {}
        prompt_specs = [
            pl.BlockSpec((ROW_TILE, D_MODEL), lambda i: (jnp.minimum(i, prompt_tiles - 1), 0)),
            _resident((ROW_TILE, D_MODEL))] + prompt_specs
        prompt_args = (x, x_sample) + prompt_args
    x = pl.pallas_call(
        body,
        grid=(grid,),
        in_specs=prompt_specs,
        out_specs=_row_spec(D_MODEL),
        out_shape=jax.ShapeDtypeStruct((n, D_MODEL), F32),
        scratch_shapes=[pltpu.VMEM((ROW_TILE, D_GATE), F32)],
        input_output_aliases=aliases,
        compiler_params=_params("parallel"),
        name="sgu_prompt",
    )(*prompt_args)
    w4 = w_s[:, :DEC_SEQ, :DEC_SEQ].reshape(SGU_GROUPS, DEC_SEQ * DEC_SEQ)
    b4 = b_s[:, :DEC_SEQ]
    smem = pl.BlockSpec(memory_space=pltpu.SMEM)
    first = n_prompt // ROW_TILE
    x, v_rows = pl.pallas_call(
        _sgu_sample_body,
        grid=(1,),
        in_specs=[smem, smem, _row_spec(D_MODEL, first)] + weights + [_resident((D_GATE, D_MODEL))],
        out_specs=[_row_spec(D_MODEL, first), pl.BlockSpec((ROW_TILE, D_GATE), lambda i: (0, 0))],
        out_shape=[jax.ShapeDtypeStruct((n, D_MODEL), F32),
                   jax.ShapeDtypeStruct((n_sample, D_GATE), F32)],
        scratch_shapes=[pltpu.VMEM((ROW_TILE, D_GATE), F32)],
        input_output_aliases={2: 0},
        compiler_params=_params("arbitrary"),
        name="sgu_sample",
    )(w4, b4, x, gmix, winu, winv, lng, lnb, wout)
    return x, v_rows


def _conv_front(x, gmix, win_ref):
    h = _rms(x, gmix).astype(BF16)
    gate_b = _dot(h, win_ref[:, 0:D_CONV])
    gate_c = _dot(h, win_ref[:, D_CONV:2 * D_CONV])
    xi = _dot(h, win_ref[:, 2 * D_CONV:3 * D_CONV])
    return gate_b, gate_c * xi


def _conv_prompt_body(x_ref, gmix_ref, win_ref, cw_ref, wout_ref, o_ref, tail_ref, zbuf, *,
                      tiles_per_seq):
    @pl.when(pl.program_id(0) % tiles_per_seq == 0)
    def _():
        zbuf[0:SUBLANES, :] = jnp.zeros((SUBLANES, D_CONV), F32)

    x = x_ref[...]
    gate_b, z = _conv_front(x, gmix_ref[...], win_ref)
    zbuf[SUBLANES:SUBLANES + ROW_TILE, :] = z
    conv = (cw_ref[2:3, :] * z
            + cw_ref[1:2, :] * zbuf[SUBLANES - 1:SUBLANES - 1 + ROW_TILE, :]
            + cw_ref[0:1, :] * zbuf[SUBLANES - 2:SUBLANES - 2 + ROW_TILE, :])
    o_ref[...] = x + _dot((gate_b * conv).astype(BF16), wout_ref[...])
    tail = zbuf[ROW_TILE + SUBLANES - 2:ROW_TILE + SUBLANES, :]
    tail_ref[0] = tail
    zbuf[SUBLANES - 2:SUBLANES, :] = tail


def _conv_sample_body(x_ref, gmix_ref, win_ref, cw_ref, st_ref, wout_ref, o_ref, tail_ref):
    x = x_ref[...]
    nb = x.shape[0] // DEC_SEQ
    gate_b, z = _conv_front(x, gmix_ref[...], win_ref)
    z1 = jnp.concatenate([st_ref[1], z[:(DEC_SEQ - 1) * nb]], axis=0)
    z2 = jnp.concatenate([st_ref[0], st_ref[1], z[:(DEC_SEQ - 2) * nb]], axis=0)
    conv = cw_ref[2:3, :] * z + cw_ref[1:2, :] * z1 + cw_ref[0:1, :] * z2
    o_ref[...] = x + _dot((gate_b * conv).astype(BF16), wout_ref[...])
    tail_ref[...] = z[(DEC_SEQ - 2) * nb:]


def _conv_layer(x, n_prompt, batch, state, gmix, w_in, conv_w, w_out):
    n = x.shape[0]
    n_sample = n - n_prompt
    nb = n_sample // DEC_SEQ
    assert n_sample == ROW_TILE
    tiles_per_seq = n_prompt // batch // ROW_TILE
    win = w_in.astype(BF16)
    wout = w_out.astype(BF16)
    gmix = gmix.reshape(1, D_MODEL)
    weights = [_resident((1, D_MODEL)), _resident((D_MODEL, 3 * D_CONV)), _resident((3, D_CONV))]
    x, tail_p = pl.pallas_call(
        functools.partial(_conv_prompt_body, tiles_per_seq=tiles_per_seq),
        grid=(n_prompt // ROW_TILE,),
        in_specs=[_row_spec(D_MODEL)] + weights + [_resident((D_CONV, D_MODEL))],
        out_specs=[_row_spec(D_MODEL),
                   pl.BlockSpec((1, 2, D_CONV), lambda i: (i // tiles_per_seq, 0, 0))],
        out_shape=[jax.ShapeDtypeStruct((n, D_MODEL), F32),
                   jax.ShapeDtypeStruct((batch, 2, D_CONV), F32)],
        scratch_shapes=[pltpu.VMEM((ROW_TILE + SUBLANES, D_CONV), F32)],
        input_output_aliases={0: 0},
        compiler_params=_params("arbitrary"),
        name="conv_prompt",
    )(x, gmix, win, conv_w, wout)
    first = n_prompt // ROW_TILE
    st = state.transpose(1, 0, 2)
    x, tail_s = pl.pallas_call(
        _conv_sample_body,
        grid=(1,),
        in_specs=[_row_spec(D_MODEL, first)] + weights + [
            _resident((2, nb, D_CONV)), _resident((D_CONV, D_MODEL))],
        out_specs=[_row_spec(D_MODEL, first), pl.BlockSpec((2 * nb, D_CONV), lambda i: (0, 0))],
        out_shape=[jax.ShapeDtypeStruct((n, D_MODEL), F32),
                   jax.ShapeDtypeStruct((2 * nb, D_CONV), F32)],
        input_output_aliases={0: 0},
        compiler_params=_params("arbitrary"),
        name="conv_sample",
    )(x, gmix, win, conv_w, st, wout)
    return x, tail_p, tail_s.reshape(2, nb, D_CONV).transpose(1, 0, 2)


def _rope_tables(seq, n_sample_rows):
    nb = n_sample_rows // DEC_SEQ
    pos = jnp.concatenate([jnp.arange(seq, dtype=jnp.int32),
                           PAST_LEN + jnp.arange(n_sample_rows, dtype=jnp.int32) // nb])
    inv_freq = jnp.power(ROPE_THETA, -jnp.arange(0, ROPE_DIM, 2, dtype=F32) / ROPE_DIM)
    d = np.arange(LANES) % HEAD_DIM
    half = ROPE_DIM // 2
    ang = pos.astype(F32)[:, None] * inv_freq[d % half][None, :]
    cos, sin = jnp.cos(ang), jnp.sin(ang)
    c = jnp.where(d[None, :] < ROPE_DIM, cos, 1.0)
    s1 = jnp.where(d[None, :] < half, -sin, 0.0)
    s2 = jnp.where((d[None, :] >= half) & (d[None, :] < ROPE_DIM), sin, 0.0)
    return c, s1, s2


def _qkv_body(x_ref, gmix_ref, w_ref, gq_ref, gk_ref, c_ref, s1_ref, s2_ref, bm_ref,
              q_ref, k_ref, v_ref):
    h = _rms(x_ref[...], gmix_ref[...]).astype(BF16)
    c, s1, s2 = c_ref[...], s1_ref[...], s2_ref[...]
    bm = bm_ref[...]

    def head_norm_rope(t, gain, out_ref, scale):
        for a in range(D_ATTN // GROUP_DIM):
            ts = t[:, a * GROUP_DIM:(a + 1) * GROUP_DIM]
            sq = ts * ts
            hi = sq.astype(BF16)
            lo = (sq - hi.astype(F32)).astype(BF16)
            ms = _dot(hi, bm) + _dot(lo, bm)
            tn = ts * lax.rsqrt(ms + EPS) * gain[:, a * GROUP_DIM:(a + 1) * GROUP_DIM]
            for b in range(GROUP_DIM // LANES):
                th = tn[:, b * LANES:(b + 1) * LANES]
                rot = (th * c + pltpu.roll(th, LANES - ROPE_DIM // 2, 1) * s1
                       + pltpu.roll(th, ROPE_DIM // 2, 1) * s2)
                out_ref[a * (GROUP_DIM // LANES) + b] = rot if scale is None else rot * scale

    head_norm_rope(_dot(h, w_ref[:, 0:D_ATTN]), gq_ref[...], q_ref, np.float32(HEAD_DIM ** -0.5))
    head_norm_rope(_dot(h, w_ref[:, D_ATTN:2 * D_ATTN]), gk_ref[...], k_ref, None)
    v = _dot(h, w_ref[:, 2 * D_ATTN:3 * D_ATTN])
    for s in range(D_ATTN // LANES):
        v_ref[s] = v[:, s * LANES:(s + 1) * LANES]


def _qkv(x, n_prompt, seq, gmix, w_qkv, q_norm_g, k_norm_g):
    n = x.shape[0]
    n_sample = n - n_prompt
    tiles_per_seq = seq // ROW_TILE
    prompt_tiles = n_prompt // ROW_TILE
    c, s1, s2 = _rope_tables(seq, n_sample)
    heads = D_ATTN // HEAD_DIM
    slabs = D_ATTN // LANES
    gq = jnp.tile(q_norm_g, heads).reshape(1, D_ATTN)
    gk = jnp.tile(k_norm_g, heads).reshape(1, D_ATTN)
    lane_head = np.arange(GROUP_DIM) // HEAD_DIM
    bm = jnp.asarray((lane_head[:, None] == lane_head[None, :]) / HEAD_DIM, BF16)
    table = pl.BlockSpec(
        (ROW_TILE, LANES),
        lambda i: (jnp.where(i < prompt_tiles, i % tiles_per_seq, tiles_per_seq), 0))
    return pl.pallas_call(
        _qkv_body,
        grid=(n // ROW_TILE,),
        in_specs=[_row_spec(D_MODEL), _resident((1, D_MODEL)), _resident((D_MODEL, 3 * D_ATTN)),
                  _resident((1, D_ATTN)), _resident((1, D_ATTN)), table, table, table,
                  _resident((GROUP_DIM, GROUP_DIM))],
        out_specs=[pl.BlockSpec((slabs, ROW_TILE, LANES), lambda i: (0, i, 0))] * 3,
        out_shape=[jax.ShapeDtypeStruct((slabs, n, LANES), F32)] * 3,
        compiler_params=_params("parallel"),
        name="qkv",
    )(x, gmix.reshape(1, D_MODEL), w_qkv.astype(BF16), gq, gk, c, s1, s2, bm)


GROUP_SLABS = GROUP_DIM // LANES


def _attn_prompt_body(q_ref, kc_ref, kp_ref, vc_ref, vp_ref, o_ref, l_ref, *, dil, blocks):
    first_key = jnp.where(pl.program_id(1) == 0, ATTN_BLOCK, 0)
    lane = lax.broadcasted_iota(jnp.int32, (1, GROUP_DIM), 1)
    head_masks = [lane // HEAD_DIM == hd for hd in range(HEADS_PER_GROUP)]
    qi = lax.broadcasted_iota(jnp.int32, (ATTN_BLOCK, 2 * ATTN_BLOCK), 0)
    kj = lax.broadcasted_iota(jnp.int32, (ATTN_BLOCK, 2 * ATTN_BLOCK), 1)
    dist = qi + ATTN_BLOCK - kj
    band = (dist >= 0) & (dist <= ATTN_BLOCK)
    band_first = band & (kj >= first_key)

    def stream_rows(r, j):
        if dil == 1:
            return pl.ds(j * ATTN_BLOCK, ATTN_BLOCK)
        return pl.ds(r + j * ATTN_BLOCK * dil, ATTN_BLOCK, stride=dil)

    def load(ref, rows):
        return jnp.concatenate([ref[s, rows, :] for s in range(GROUP_SLABS)], axis=1)

    for r in range(dil):
        for j in range(blocks):
            rows = stream_rows(r, j)
            q = load(q_ref, rows).astype(BF16)
            if j == 0:
                k_prev, v_prev, mask = load(kp_ref, stream_rows(r, 0)), load(vp_ref, stream_rows(r, 0)), band_first
            else:
                prev = stream_rows(r, j - 1)
                k_prev, v_prev, mask = load(kc_ref, prev), load(vc_ref, prev), band
            kk = jnp.concatenate([k_prev, load(kc_ref, rows)], axis=0).astype(BF16)
            vv = jnp.concatenate([v_prev, load(vc_ref, rows)], axis=0).astype(BF16)
            q_heads = jnp.concatenate([jnp.where(hm, q, jnp.zeros_like(q)) for hm in head_masks], axis=0)
            s = jnp.where(jnp.concatenate([mask] * HEADS_PER_GROUP, axis=0), _dot_nt(q_heads, kk), -jnp.inf)
            m = jnp.max(s, axis=-1, keepdims=True)
            p = jnp.exp(s - m)
            l = jnp.sum(p, axis=-1, keepdims=True)
            pv = _dot(p.astype(BF16), vv) * (1.0 / l)
            lse = m + jnp.log(l)
            o_acc = jnp.zeros((ATTN_BLOCK, GROUP_DIM), F32)
            l_acc = jnp.zeros((ATTN_BLOCK, GROUP_DIM), F32)
            for hd, hm in enumerate(head_masks):
                part = slice(hd * ATTN_BLOCK, (hd + 1) * ATTN_BLOCK)
                o_acc = o_acc + jnp.where(hm, pv[part], 0.0)
                l_acc = l_acc + jnp.where(hm, lse[part], 0.0)
            for s in range(GROUP_SLABS):
                o_ref[s, rows, :] = o_acc[:, s * LANES:(s + 1) * LANES]
                l_ref[s, rows, :] = l_acc[:, s * LANES:(s + 1) * LANES]


def _attn_prompt(q, k, v, group, batch, seq):
    dil = DILATIONS[group]
    assert WINDOWS[group] // dil == ATTN_BLOCK
    span = ATTN_BLOCK * dil
    tile = max(ROW_TILE, span)
    tiles = seq // tile
    cur = pl.BlockSpec((GROUP_SLABS, tile, LANES), lambda b, t: (group, b * tiles + t, 0))
    prev = pl.BlockSpec(
        (GROUP_SLABS, span, LANES),
        lambda b, t: (group, jnp.maximum((b * tiles + t) * (tile // span) - 1, 0), 0))
    out = pl.BlockSpec((GROUP_SLABS, tile, LANES), lambda b, t: (0, b * tiles + t, 0))
    shape = jax.ShapeDtypeStruct((GROUP_SLABS, batch * seq, LANES), F32)
    return pl.pallas_call(
        functools.partial(_attn_prompt_body, dil=dil, blocks=tile // span),
        grid=(batch, tiles),
        in_specs=[cur, cur, prev, cur, prev],
        out_specs=[out, out],
        out_shape=[shape, shape],
        compiler_params=_params("parallel", "arbitrary"),
        name=f"attn_prompt_g{group}",
    )(q, k, k, v, v)


HEADS_PER_SLAB = LANES // HEAD_DIM


def _attn_sample_body(q_ref, k_ref, v_ref, c_ref, o_ref, lse_ref, *, dil, buf, seqs):
    rows = HEADS_PER_SLAB * SUBLANES
    lane = lax.broadcasted_iota(jnp.int32, (1, LANES), 1)
    head_masks = [lane // HEAD_DIM == hh for hh in range(HEADS_PER_SLAB)]

    def visibility(width):
        t = lax.broadcasted_iota(jnp.int32, (rows, width), 0) % SUBLANES
        c = lax.broadcasted_iota(jnp.int32, (rows, width), 1)
        return jnp.where(t < DEC_SEQ, t, t - DEC_SEQ), c

    tq, c = visibility(buf)
    vis_cache = (c >= tq) if dil == 1 else (c % dil == tq)
    tq, c = visibility(SUBLANES)
    vis_fresh = (c < DEC_SEQ) & ((c <= tq) if dil == 1 else (c == tq))

    def one(bi):
        for s in range(GROUP_SLABS):
            heads = slice(s * HEADS_PER_SLAB, (s + 1) * HEADS_PER_SLAB)
            q, k_new, v_new = q_ref[s, bi], k_ref[s, bi].astype(BF16), v_ref[s, bi]
            q_heads = jnp.concatenate([jnp.where(hm, q, 0.0) for hm in head_masks], axis=0).astype(BF16)
            k_cache = c_ref[bi, 0, heads].reshape(LANES, buf).astype(BF16)
            v_cache = c_ref[bi, 1, heads].reshape(LANES, buf).astype(BF16)
            s_cache = jnp.where(vis_cache, _dot(q_heads, k_cache), -jnp.inf)
            s_fresh = jnp.where(vis_fresh, _dot_nt(q_heads, k_new), -jnp.inf)
            m = jnp.maximum(jnp.max(s_cache, axis=1, keepdims=True), jnp.max(s_fresh, axis=1, keepdims=True))
            p_cache = jnp.exp(s_cache - m)
            p_fresh = jnp.exp(s_fresh - m)
            l = jnp.sum(p_cache, axis=1, keepdims=True) + jnp.sum(p_fresh, axis=1, keepdims=True)
            inv = 1.0 / l
            o = _dot_nt((p_cache * inv).astype(BF16), v_cache)
            w_fresh = p_fresh * inv
            for t2 in range(DEC_SEQ):
                o = o + w_fresh[:, t2:t2 + 1] * v_new[t2:t2 + 1, :]
            lse = jnp.broadcast_to(m + jnp.log(l), (rows, LANES))
            o_slab = jnp.zeros((SUBLANES, LANES), F32)
            lse_slab = jnp.zeros((SUBLANES, LANES), F32)
            for hh, hm in enumerate(head_masks):
                part = slice(hh * SUBLANES, (hh + 1) * SUBLANES)
                o_slab = o_slab + jnp.where(hm, o[part], 0.0)
                lse_slab = lse_slab + jnp.where(hm, lse[part], 0.0)
            o_ref[s, bi] = o_slab
            lse_ref[s, bi] = lse_slab

    def pair(i, carry):
        one(2 * i)
        one(2 * i + 1)
        return carry

    if seqs == 2:
        pair(0, 0)
    else:
        lax.fori_loop(0, seqs // 2, pair, 0)


def _sample_rows(slabs, n_prompt):
    part = slabs[:, n_prompt:]
    nb = part.shape[1] // DEC_SEQ
    part = part.reshape(part.shape[0], DEC_SEQ, nb, LANES).transpose(0, 2, 1, 3)
    return jnp.pad(part, ((0, 0), (0, 0), (0, SUBLANES - DEC_SEQ), (0, 0)))


def _attn_sample(q, k, v, cache, group):
    nb, buf = cache.shape[0], cache.shape[1]
    dil = DILATIONS[group]
    assert buf == ATTN_BLOCK * dil and WINDOWS[group] == buf and (dil == 1 or dil >= DEC_SEQ)
    seqs = max(2, 2 * PAST_LEN // buf)
    fresh = pl.BlockSpec((GROUP_SLABS, seqs, SUBLANES, LANES), lambda i: (group, i, 0, 0))
    out = pl.BlockSpec((GROUP_SLABS, seqs, SUBLANES, LANES), lambda i: (0, i, 0, 0))
    shape = jax.ShapeDtypeStruct((GROUP_SLABS, nb, SUBLANES, LANES), F32)
    o, lse = pl.pallas_call(
        functools.partial(_attn_sample_body, dil=dil, buf=buf, seqs=seqs),
        grid=(nb // seqs,),
        in_specs=[fresh, fresh, fresh,
                  pl.BlockSpec((seqs, 2, HEADS_PER_GROUP, HEAD_DIM, buf), lambda i: (i, 0, 0, 0, 0))],
        out_specs=[out, out],
        out_shape=[shape, shape],
        compiler_params=_params("parallel"),
        name=f"attn_sample_g{group}",
    )(q, k, v, cache.transpose(0, 2, 3, 4, 1))
    rows = lambda a: a[:, :, :DEC_SEQ].transpose(0, 2, 1, 3).reshape(GROUP_SLABS, DEC_SEQ * nb, LANES)
    return rows(o), rows(lse)


def _from_slabs(slabs, group):
    part = slabs[group * GROUP_SLABS:(group + 1) * GROUP_SLABS]
    return part.transpose(1, 0, 2).reshape(part.shape[1], GROUP_DIM)


def _merge_body(x_ref, o0_ref, o1_ref, o2_ref, l0_ref, l1_ref, l2_ref, w_ref, out_ref):
    rows = lambda ref: jnp.concatenate([ref[s] for s in range(GROUP_SLABS)], axis=1)
    l0, l1, l2 = rows(l0_ref), rows(l1_ref), rows(l2_ref)
    m = jnp.maximum(jnp.maximum(l0, l1), l2)
    e0, e1, e2 = jnp.exp(l0 - m), jnp.exp(l1 - m), jnp.exp(l2 - m)
    inv = 1.0 / (e0 + e1 + e2)
    y = x_ref[...]
    for g, (o_ref, e) in enumerate(((o0_ref, e0), (o1_ref, e1), (o2_ref, e2))):
        y = y + _dot((rows(o_ref) * (e * inv)).astype(BF16),
                     w_ref[g * GROUP_DIM:(g + 1) * GROUP_DIM, :])
    out_ref[...] = y


def _merge_out_proj(x, outs, lses, w_out, first_tile, tiles):
    n = x.shape[0]
    part = pl.BlockSpec((GROUP_SLABS, ROW_TILE, LANES), lambda i: (0, i, 0))
    return pl.pallas_call(
        _merge_body,
        grid=(tiles,),
        in_specs=[_row_spec(D_MODEL, first_tile)] + [part] * 6 + [_resident((D_ATTN, D_MODEL))],
        out_specs=_row_spec(D_MODEL, first_tile),
        out_shape=jax.ShapeDtypeStruct((n, D_MODEL), F32),
        input_output_aliases={0: 0},
        compiler_params=_params("parallel"),
        name="attn_merge_out_proj",
    )(x, *outs, *lses, w_out)


def _attn_layer(x, n_prompt, batch, caches, gmix, w_qkv, q_norm_g, k_norm_g, w_out):
    n = x.shape[0]
    seq = n_prompt // batch
    nb = (n - n_prompt) // DEC_SEQ
    q, k, v = _qkv(x, n_prompt, seq, gmix, w_qkv, q_norm_g, k_norm_g)
    ks, vs = k[:, n_prompt:], v[:, n_prompt:]
    q_new, k_new, v_new = (_sample_rows(a, n_prompt) for a in (q, k, v))
    outs_p, lses_p, outs_s, lses_s, kv_p, kv_s = [], [], [], [], [], []
    for g in range(N_ATTN_GROUPS):
        o, lse = _attn_prompt(q, k, v, g, batch, seq)
        outs_p.append(o)
        lses_p.append(lse)
        o, lse = _attn_sample(q_new, k_new, v_new, caches[g], g)
        outs_s.append(o)
        lses_s.append(lse)
        k_rows, v_rows = _from_slabs(ks, g), _from_slabs(vs, g)
        keep = min(WINDOWS[g], seq)
        heads = (HEADS_PER_GROUP, HEAD_DIM)
        tail = lambda a: _from_slabs(jnp.concatenate(
            [a[g * GROUP_SLABS:(g + 1) * GROUP_SLABS, (b + 1) * seq - keep:(b + 1) * seq]
             for b in range(batch)], axis=1), 0)
        kv_p.append(jnp.stack([tail(k), tail(v)], axis=1).reshape(batch, keep, 2, *heads))
        kv_s.append(jnp.stack([k_rows, v_rows], axis=1).reshape(DEC_SEQ, nb, 2, *heads)
                    .transpose(1, 0, 2, 3, 4))
    wout = w_out.astype(BF16)
    x = _merge_out_proj(x, outs_p, lses_p, wout, 0, n_prompt // ROW_TILE)
    x = _merge_out_proj(x, outs_s, lses_s, wout, n_prompt // ROW_TILE, (n - n_prompt) // ROW_TILE)
    return x, kv_p, kv_s


def _ffn_body(x_ref, g_ref, wg_ref, wu_ref, wd_ref, o_ref):
    x = x_ref[...]
    h = _rms(x, g_ref[...]).astype(BF16)
    y = x
    for c in range(D_FF // FF_CHUNK):
        cols = slice(c * FF_CHUNK, (c + 1) * FF_CHUNK)
        act = _silu(_dot(h, wg_ref[:, cols])) * _dot(h, wu_ref[:, cols])
        y = y + _dot(act.astype(BF16), wd_ref[cols, :])
    o_ref[...] = y


def _dense_ffn(x, g, w_gu, w_down):
    n = x.shape[0]
    wgu = w_gu.astype(BF16)
    half = pl.Buffered(1)
    return pl.pallas_call(
        _ffn_body,
        grid=(n // ROW_TILE,),
        in_specs=[_row_spec(D_MODEL), _resident((1, D_MODEL)),
                  pl.BlockSpec((D_MODEL, D_FF), lambda i: (0, 0), pipeline_mode=half),
                  pl.BlockSpec((D_MODEL, D_FF), lambda i: (0, 1), pipeline_mode=half),
                  _resident((D_FF, D_MODEL))],
        out_specs=_row_spec(D_MODEL),
        out_shape=jax.ShapeDtypeStruct((n, D_MODEL), F32),
        input_output_aliases={0: 0},
        compiler_params=_params("parallel"),
        name="dense_ffn",
    )(x, g.reshape(1, D_MODEL), wgu, wgu, w_down.astype(BF16))


PACK_ROWS = D_MODEL // LANES


def _split_bf16(a):
    hi = a.astype(BF16)
    return hi, (a - hi.astype(F32)).astype(BF16)


def _router_body(x_ref, g_ref, r_ref, h_ref, idx_ref, gate_ref):
    h = _rms(x_ref[...], g_ref[...])
    for s in range(PACK_ROWS):
        h_ref[pl.ds(s, ROW_TILE, stride=PACK_ROWS), :] = h[:, s * LANES:(s + 1) * LANES]
    h_hi, h_lo = _split_bf16(h)
    r_hi, r_lo = _split_bf16(r_ref[...])
    logits = _dot(h_hi, r_hi) + (_dot(h_lo, r_hi) + _dot(h_hi, r_lo))
    lt = logits.T[:N_EXPERTS, :]
    e = lax.broadcasted_iota(jnp.int32, lt.shape, 0)
    m1 = jnp.max(lt, axis=0, keepdims=True)
    e1 = jnp.min(jnp.where(lt == m1, e, N_EXPERTS), axis=0, keepdims=True)
    rest = jnp.where(e == e1, -jnp.inf, lt)
    m2 = jnp.max(rest, axis=0, keepdims=True)
    e2 = jnp.min(jnp.where(rest == m2, e, N_EXPERTS), axis=0, keepdims=True)
    a = jnp.exp(m2 - m1)
    inv = 1.0 / (1.0 + a)
    idx_ref[...] = jnp.concatenate([e1, e2], axis=0)
    gate_ref[...] = jnp.concatenate([inv, a * inv], axis=0)


def _router(x, g, router):
    n = x.shape[0]
    r = jnp.pad(router, ((0, 0), (0, LANES - N_EXPERTS)))
    pair = pl.BlockSpec((2, ROW_TILE), lambda i: (0, i))
    return pl.pallas_call(
        _router_body,
        grid=(n // ROW_TILE,),
        in_specs=[_row_spec(D_MODEL), _resident((1, D_MODEL)), _resident((D_MODEL, LANES))],
        out_specs=[pl.BlockSpec((ROW_TILE * PACK_ROWS, LANES), lambda i: (i, 0)), pair, pair],
        out_shape=[jax.ShapeDtypeStruct((n * PACK_ROWS, LANES), F32),
                   jax.ShapeDtypeStruct((2, n), jnp.int32),
                   jax.ShapeDtypeStruct((2, n), F32)],
        compiler_params=_params("parallel"),
        name="moe_router",
    )(x, g.reshape(1, D_MODEL), r)


def _moe_body(te_ref, nsub_ref, src_ref, h_hbm, wg_ref, wu_ref, wd_ref, gs_ref, y_ref,
              acc_ref, xbuf, xb_ref, wgb, wub, wdb, sem):
    i, f = pl.program_id(0), pl.program_id(1)
    slot = i % 2
    del te_ref

    def sub_rows(j):
        return pl.ds(j * MOE_SUB, MOE_SUB)

    def fetch(tile, slot):
        for j in range(MOE_SUBS):
            @pl.when(j < nsub_ref[tile])
            def _():
                def row(r, carry):
                    tok = pl.multiple_of(src_ref[tile * MOE_TILE + j * MOE_SUB + r], PACK_ROWS)
                    pltpu.make_async_copy(
                        h_hbm.at[pl.ds(tok, PACK_ROWS)],
                        xbuf.at[slot, pl.ds((j * MOE_SUB + r) * PACK_ROWS, PACK_ROWS)],
                        sem.at[slot, j]).start()
                    return carry
                lax.fori_loop(0, MOE_SUB, row, 0, unroll=8)

    @pl.when((i == 0) & (f == 0))
    def _():
        fetch(0, 0)

    @pl.when(f == 0)
    def _():
        for j in range(MOE_SUBS):
            @pl.when(j < nsub_ref[i])
            def _():
                blk = xbuf.at[slot, pl.ds(j * MOE_SUB * PACK_ROWS, MOE_SUB * PACK_ROWS)]
                pltpu.make_async_copy(blk, blk, sem.at[slot, j]).wait()
                for s in range(PACK_ROWS):
                    piece = xbuf[slot, pl.ds(j * MOE_SUB * PACK_ROWS + s, MOE_SUB, stride=PACK_ROWS), :]
                    xb_ref[sub_rows(j), s * LANES:(s + 1) * LANES] = piece.astype(BF16)
                acc_ref[sub_rows(j), :] = jnp.zeros((MOE_SUB, D_MODEL), F32)

        @pl.when(i + 1 < pl.num_programs(0))
        def _():
            fetch(i + 1, 1 - slot)

    @pl.when(nsub_ref[i] > 0)
    def _():
        wgb[...] = wg_ref[0, 0].astype(BF16)
        wub[...] = wu_ref[0, 0].astype(BF16)
        wdb[...] = wd_ref[0, 0].astype(BF16)
        for j in range(MOE_SUBS):
            @pl.when(j < nsub_ref[i])
            def _():
                xb = xb_ref[sub_rows(j), :]
                act = _silu(_dot(xb, wgb[...])) * _dot(xb, wub[...])
                acc_ref[sub_rows(j), :] += _dot(act.astype(BF16), wdb[...])

    @pl.when(f == MOE_FF_STEPS - 1)
    def _():
        for j in range(MOE_SUBS):
            @pl.when(j < nsub_ref[i])
            def _():
                y_ref[sub_rows(j), :] = acc_ref[sub_rows(j), :] * gs_ref[sub_rows(j), :]

            @pl.when(j >= nsub_ref[i])
            def _():
                y_ref[sub_rows(j), :] = jnp.zeros((MOE_SUB, D_MODEL), F32)


def _moe_experts(h, src, gs, tile_expert, tile_subs, w_gu, w_down, layer):
    rows = src.shape[0]
    tiles = rows // MOE_TILE
    last = MOE_FF_STEPS - 1
    fstep = lambda i, f, ns: jnp.where(ns[i] > 0, f, last)
    grid_spec = pltpu.PrefetchScalarGridSpec(
        num_scalar_prefetch=3,
        grid=(tiles, MOE_FF_STEPS),
        in_specs=[
            pl.BlockSpec(memory_space=pl.ANY),
            pl.BlockSpec((1, 1, D_MODEL, MOE_FF_CHUNK),
                         lambda i, f, te, ns, *_: (layer, te[i], 0, fstep(i, f, ns))),
            pl.BlockSpec((1, 1, D_MODEL, MOE_FF_CHUNK),
                         lambda i, f, te, ns, *_: (layer, te[i], 0, MOE_FF_STEPS + fstep(i, f, ns))),
            pl.BlockSpec((1, 1, MOE_FF_CHUNK, D_MODEL),
                         lambda i, f, te, ns, *_: (layer, te[i], fstep(i, f, ns), 0)),
            pl.BlockSpec((MOE_TILE, 1), lambda i, f, *_: (i, 0)),
        ],
        out_specs=pl.BlockSpec((MOE_TILE, D_MODEL), lambda i, f, *_: (i, 0)),
        scratch_shapes=[
            pltpu.VMEM((MOE_TILE, D_MODEL), F32),
            pltpu.VMEM((2, MOE_TILE * PACK_ROWS, LANES), F32),
            pltpu.VMEM((MOE_TILE, D_MODEL), BF16),
            pltpu.VMEM((D_MODEL, MOE_FF_CHUNK), BF16),
            pltpu.VMEM((D_MODEL, MOE_FF_CHUNK), BF16),
            pltpu.VMEM((MOE_FF_CHUNK, D_MODEL), BF16),
            pltpu.SemaphoreType.DMA((2, MOE_SUBS)),
        ],
    )
    return pl.pallas_call(
        _moe_body,
        grid_spec=grid_spec,
        out_shape=jax.ShapeDtypeStruct((rows, D_MODEL), F32),
        compiler_params=_params("arbitrary", "arbitrary"),
        name="moe_experts",
    )(tile_expert, tile_subs, src, h, w_gu, w_gu, w_down, gs)


def _route(idx, gates, n):
    assign = 2 * n
    tiles = -(-(assign + N_EXPERTS * (MOE_TILE - 1)) // MOE_TILE)
    experts = jnp.arange(N_EXPERTS, dtype=jnp.int32)
    ea = idx.reshape(assign)
    ga = gates.reshape(assign)
    onehot = (ea[:, None] == experts[None, :]).astype(jnp.int32)
    csum = jnp.cumsum(onehot, axis=0)
    counts = csum[-1]
    etiles = (counts + MOE_TILE - 1) // MOE_TILE
    tile_end = jnp.cumsum(etiles)
    tile_start = tile_end - etiles
    cnt_start = jnp.cumsum(counts) - counts
    pos = jnp.sum(onehot * ((tile_start * MOE_TILE)[None, :] + csum - 1), axis=1)
    token = jnp.arange(assign, dtype=jnp.int32) % n
    _, sorted_token, sorted_gate = lax.sort((ea, token, ga), num_keys=1, is_stable=True)
    ti = jnp.arange(tiles, dtype=jnp.int32)
    active = ti < tile_end[-1]
    te_raw = jnp.sum((ti[:, None] >= tile_end[None, :]).astype(jnp.int32), axis=1)
    last_expert = jnp.max(jnp.where(etiles > 0, experts, 0))
    te = jnp.where(active, jnp.minimum(te_raw, N_EXPERTS - 1), last_expert)
    tile_onehot = (te[:, None] == experts[None, :]).astype(jnp.int32)
    pick = lambda per_expert: jnp.sum(tile_onehot * per_expert[None, :], axis=1)
    tile_in_expert = ti - pick(tile_start)
    tv = jnp.where(active, jnp.clip(pick(counts) - tile_in_expert * MOE_TILE, 0, MOE_TILE), 0)
    first = pick(cnt_start) + tile_in_expert * MOE_TILE
    r = jnp.arange(MOE_TILE, dtype=jnp.int32)
    valid = (r[None, :] < tv[:, None]).reshape(-1)
    sidx = jnp.clip(first[:, None] + r[None, :], 0, assign - 1).reshape(-1)
    src = jnp.where(valid, jnp.take(sorted_token, sidx, mode="clip"), 0) * PACK_ROWS
    gs = jnp.where(valid, jnp.take(sorted_gate, sidx, mode="clip"), 0.0)
    return src, gs.reshape(-1, 1), te, (tv + MOE_SUB - 1) // MOE_SUB, pos


def _moe_layer(x, g, router, w_gu, w_down, layer, split=None):
    n = x.shape[0]
    h, idx, gates = _router(x, g, router)
    src, gs, te, subs, pos = _route(idx, gates, n)
    y = _moe_experts(h, src, gs, te, subs, w_gu, w_down, layer)
    y1 = jnp.take(y, pos[:n], axis=0, mode="clip")
    y2 = jnp.take(y, pos[n:], axis=0, mode="clip")
    if split is None:
        return x + (y1 + y2)
    return x[:split] + (y1[:split] + y2[:split]), x[split:] + (y1[split:] + y2[split:])


def kernel(x_prompt, x_sample, state_conv, cache_kv_w128, cache_kv_w512, cache_kv_w2048, norm_mix_g, norm_ffn_g, a_w_in, a_ln_g, a_ln_b, a_w_s, a_b_s, a_w_out, b_w_in, b_conv_w, b_w_out, c_w_qkv, c_q_norm_g, c_k_norm_g, c_w_out, f_w_gu, f_w_down, m_router, m_w_gu, m_w_down):
    batch, seq, _ = x_prompt.shape
    nb = x_sample.shape[0]
    n_prompt = batch * seq
    caches = (cache_kv_w128, cache_kv_w512, cache_kv_w2048)
    x = x_prompt.reshape(n_prompt, D_MODEL)
    x_sample_rows = x_sample.transpose(1, 0, 2).reshape(DEC_SEQ * nb, D_MODEL)
    depth = norm_mix_g.shape[0]
    sgu_v, conv_p, conv_s = [], [], []
    kv_p = [[] for _ in range(N_ATTN_GROUPS)]
    kv_s = [[] for _ in range(N_ATTN_GROUPS)]
    for i in range(depth):
        kind, j = i % 3, i // 3
        if kind == 0:
            x, v_rows = _sgu_layer(x, n_prompt, norm_mix_g[i], a_w_in[j], a_ln_g[j], a_ln_b[j],
                                   a_w_s[j], a_b_s[j], a_w_out[j],
                                   x_sample=x_sample_rows if i == 0 else None)
            sgu_v.append(v_rows.reshape(DEC_SEQ, nb, D_GATE).transpose(1, 0, 2))
        elif kind == 1:
            x, tail_p, tail_s = _conv_layer(x, n_prompt, batch, state_conv[j], norm_mix_g[i],
                                            b_w_in[j], b_conv_w[j], b_w_out[j])
            conv_p.append(tail_p)
            conv_s.append(tail_s)
        else:
            x, kvp, kvs = _attn_layer(x, n_prompt, batch, [c[j] for c in caches], norm_mix_g[i],
                                      c_w_qkv[j], c_q_norm_g[j], c_k_norm_g[j], c_w_out[j])
            for g in range(N_ATTN_GROUPS):
                kv_p[g].append(kvp[g])
                kv_s[g].append(kvs[g])
        e = i // 2
        if i % 2 == 0:
            x = _dense_ffn(x, norm_ffn_g[i], f_w_gu[e], f_w_down[e])
        else:
            x = _moe_layer(x, norm_ffn_g[i], m_router[e], m_w_gu, m_w_down, e,
                           split=n_prompt if i == depth - 1 else None)
    xp, xs = x if isinstance(x, tuple) else (x[:n_prompt], x[n_prompt:])
    y_prompt = xp.reshape(batch, seq, D_MODEL)
    y_sample = xs.reshape(DEC_SEQ, nb, D_MODEL).transpose(1, 0, 2)
    stack = lambda xs: jnp.stack(xs, axis=0)
    return (y_prompt, y_sample, stack(conv_p), stack(conv_s),
            stack(kv_p[0]), stack(kv_s[0]), stack(kv_p[1]), stack(kv_s[1]),
            stack(kv_p[2]), stack(kv_s[2]), stack(sgu_v))
```
